```python
import math
import jax
import jax.numpy as jnp
from jax import lax
import numpy as np

D_MODEL = 1024
BATCH = 32
SEQ = 256
DEPTH = 4
DEC_BATCH = 2
DEC_SEQ = 2048
PAST_LEN = 256

GRID_W = 64
N_MIXERS = 4
N_MOD = 6
ALPHA = (2 * DEPTH) ** 0.25
BETA = (8 * DEPTH) ** -0.25
LN_EPS = 1e-5
RMS_EPS = 1e-6
ROPE_BASE = 10000.0
QBLOCK = 128
NEG_INF = -1e30

MLA_HEADS = 8
MLA_NOPE = 128
MLA_ROPE = 64
MLA_V = 128
MLA_Q_LORA = 384
MLA_KV_LORA = 256
MLA_SCALE = (MLA_NOPE + MLA_ROPE) ** -0.5

GM_CHUNK = 128
GM_HALF = 2 * D_MODEL
GM_GROUPS = 4

CONV_WIDTH = 31

SWA_HEADS = 16
SWA_KV_HEADS = 4
SWA_GROUP = SWA_HEADS // SWA_KV_HEADS
SWA_HEAD_DIM = 64
SWA_WINDOW = 128
SWA_BLOCK = 128
SWA_SCALE = SWA_HEAD_DIM ** -0.5

N_EXPERTS = 16
EC_CAPACITY_FACTOR = 2
EXPERT_FF = D_MODEL

N_LAYERS_MLA = (DEPTH + 3) // 4
N_LAYERS_GMLP = (DEPTH + 2) // 4
N_LAYERS_CONV = (DEPTH + 1) // 4
N_LAYERS_SWA = DEPTH // 4

kernel_name = 'hybrid_diffusion_mla_gmlp_conv_swa_ec_step'


def layer_norm(x, g, b):
    xf = x.astype(jnp.float32)
    mu = jnp.mean(xf, axis=-1, keepdims=True)
    var = jnp.mean(jnp.square(xf - mu), axis=-1, keepdims=True)
    y = (xf - mu) * lax.rsqrt(var + LN_EPS)
    return (y * g.astype(jnp.float32) + b.astype(jnp.float32)).astype(x.dtype)


def rms_norm(x, g):
    xf = x.astype(jnp.float32)
    y = xf * lax.rsqrt(jnp.mean(jnp.square(xf), axis=-1, keepdims=True) + RMS_EPS)
    return (y * g.astype(jnp.float32)).astype(x.dtype)


def axial_rope(x):
    L, dim = x.shape[1], x.shape[-1]
    half = dim // 2
    n = half // 2
    t = jnp.arange(L)
    rows, cols = t // GRID_W, t % GRID_W
    inv_freq = ROPE_BASE ** (-jnp.arange(n, dtype=jnp.float32) / n)
    bshape = (L,) + (1,) * (x.ndim - 3) + (n,)

    def rotate(xp, pos):
        ang = (pos.astype(jnp.float32)[:, None] * inv_freq[None, :]).reshape(bshape)
        cos, sin = jnp.cos(ang), jnp.sin(ang)
        x1, x2 = xp[..., :n], xp[..., n:]
        return jnp.concatenate([x1 * cos - x2 * sin, x1 * sin + x2 * cos], axis=-1)

    xf = x.astype(jnp.float32)
    return jnp.concatenate([rotate(xf[..., :half], rows), rotate(xf[..., half:], cols)], axis=-1).astype(x.dtype)


def blocked_attention(sets, sink, scale):
    q0 = sets[0][0]
    B, Lq, KH, G = q0.shape[:4]
    nb = Lq // QBLOCK
    q_blocks = tuple(q.reshape(B, nb, QBLOCK, KH, G, q.shape[-1]).swapaxes(0, 1) for q, _, _ in sets)
    lens = [k.shape[1] for _, k, _ in sets]

    def one_block(qs):
        s = jnp.concatenate([jnp.einsum('bqkgd,bjkd->bkgqj', qb, k) for qb, (_, k, _) in zip(qs, sets)],
                            axis=-1).astype(jnp.float32) * scale
        if sink is not None:
            s = jnp.concatenate([s, jnp.broadcast_to(sink.astype(jnp.float32)[None, :, :, None, None],
                                                     s.shape[:-1] + (1,))], axis=-1)
        p = jax.nn.softmax(s, axis=-1)
        out = None
        off = 0
        for (_, _, v), n in zip(sets, lens):
            o = jnp.einsum('bkgqj,bjkd->bqkgd', p[..., off:off + n].astype(v.dtype), v)
            out = o if out is None else out + o
            off += n
        return out

    o = lax.map(one_block, q_blocks)
    return o.swapaxes(0, 1).reshape(B, Lq, KH, G, o.shape[-1])


def banded_window_attention(q_rot, k_rot, v, q_raw, k_ctx, v_ctx, sink):
    B, L, KH, G, hd = q_rot.shape
    W = SWA_BLOCK
    nb = L // W
    qb = q_rot.reshape(B, nb, W, KH, G, hd)
    qcb = q_raw.reshape(B, nb, W, KH, G, hd)
    pad = ((0, 0), (W, W), (0, 0), (0, 0))
    kp = jnp.pad(k_rot, pad).reshape(B, nb + 2, W, KH, hd)
    vp = jnp.pad(v, pad).reshape(B, nb + 2, W, KH, hd)
    kb = jnp.concatenate([kp[:, :nb], kp[:, 1:nb + 1], kp[:, 2:]], axis=2)
    vb = jnp.concatenate([vp[:, :nb], vp[:, 1:nb + 1], vp[:, 2:]], axis=2)
    qi = jnp.arange(W)[:, None]
    kj = jnp.arange(3 * W)[None, :]
    band = (kj >= qi + W - SWA_WINDOW) & (kj <= qi + W + SWA_WINDOW)
    kpos = (jnp.arange(nb)[:, None] - 1) * W + jnp.arange(3 * W)[None, :]
    valid = (kpos >= 0) & (kpos < L)
    mask = band[None, :, :] & valid[:, None, :]
    s_lat = jnp.einsum('bnqkgd,bnjkd->bnkgqj', qb, kb).astype(jnp.float32) * SWA_SCALE
    s_lat = jnp.where(mask[None, :, None, None, :, :], s_lat, NEG_INF)
    s_ctx = jnp.einsum('bnqkgd,bjkd->bnkgqj', qcb, k_ctx).astype(jnp.float32) * SWA_SCALE
    s_sink = jnp.broadcast_to(sink.astype(jnp.float32)[None, None, :, :, None, None], s_lat.shape[:-1] + (1,))
    p = jax.nn.softmax(jnp.concatenate([s_lat, s_ctx, s_sink], axis=-1), axis=-1)
    n_ctx = k_ctx.shape[1]
    o = (jnp.einsum('bnkgqj,bnjkd->bnqkgd', p[..., :3 * W].astype(v.dtype), vb)
         + jnp.einsum('bnkgqj,bjkd->bnqkgd', p[..., 3 * W:3 * W + n_ctx].astype(v.dtype), v_ctx))
    return o.reshape(B, L, KH, G, hd)


def mla_queries(h, wq_a, q_norm, wq_b):
    B, L, _ = h.shape
    q = (rms_norm(h @ wq_a, q_norm) @ wq_b).reshape(B, L, MLA_HEADS, MLA_NOPE + MLA_ROPE)
    return q[..., :MLA_NOPE], q[..., MLA_NOPE:]


def mla_compress(h, wkv_a, kv_norm):
    kv = h @ wkv_a
    return rms_norm(kv[..., :MLA_KV_LORA], kv_norm), kv[..., MLA_KV_LORA:]


def mla_expand(ckv, kpe, wkv_b):
    B, L, _ = ckv.shape
    kv = (ckv @ wkv_b).reshape(B, L, MLA_HEADS, MLA_NOPE + MLA_V)
    k = jnp.concatenate([kv[..., :MLA_NOPE],
                         jnp.broadcast_to(kpe[:, :, None, :], (B, L, MLA_HEADS, MLA_ROPE))], axis=-1)
    return k, kv[..., MLA_NOPE:]


def mla_context(h, wq_a, q_norm, wq_b, wkv_a, kv_norm, wkv_b, wo):
    B, L, _ = h.shape
    q_nope, q_pe = mla_queries(h, wq_a, q_norm, wq_b)
    ckv, kpe = mla_compress(h, wkv_a, kv_norm)
    k, v = mla_expand(ckv, kpe, wkv_b)
    q = jnp.concatenate([q_nope, q_pe], axis=-1)[:, :, :, None, :]
    o = blocked_attention([(q, k, v)], None, MLA_SCALE)
    return o.reshape(B, L, MLA_HEADS * MLA_V) @ wo, ckv, kpe


def mla_latent(h, ckv_ctx, kpe_ctx, wq_a, q_norm, wq_b, wkv_a, kv_norm, wkv_b, wo):
    B, L, _ = h.shape
    q_nope, q_pe = mla_queries(h, wq_a, q_norm, wq_b)
    ckv, kpe = mla_compress(h, wkv_a, kv_norm)
    k_lat, v_lat = mla_expand(ckv, axial_rope(kpe), wkv_b)
    k_ctx, v_ctx = mla_expand(ckv_ctx, kpe_ctx, wkv_b)
    q_lat = jnp.concatenate([q_nope, axial_rope(q_pe)], axis=-1)[:, :, :, None, :]
    q_ctx = jnp.concatenate([q_nope, q_pe], axis=-1)[:, :, :, None, :]
    o = blocked_attention([(q_lat, k_lat, v_lat), (q_ctx, k_ctx, v_ctx)], None, MLA_SCALE)
    return o.reshape(B, L, MLA_HEADS * MLA_V) @ wo


def chunk_gmlp(h, w_in, b_in, v_g, v_b, w_s, b_s, w_out, b_out):
    B, L, _ = h.shape
    z = jax.nn.gelu(h @ w_in + b_in)
    u, v = z[..., :GM_HALF], z[..., GM_HALF:]
    v = layer_norm(v, v_g, v_b).reshape(B, L // GM_CHUNK, GM_CHUNK, GM_GROUPS, GM_HALF // GM_GROUPS)
    v = jnp.einsum('gij,bnjgc->bnigc', w_s, v) + b_s.T[None, None, :, :, None]
    return (u * v.reshape(B, L, GM_HALF)) @ w_out + b_out


def conformer_conv(h, w_pw1, b_pw1, w_dw, b_dw, n_g, n_b, w_pw2, b_pw2):
    a = h @ w_pw1 + b_pw1
    a = a[..., :D_MODEL] * jax.nn.sigmoid(a[..., D_MODEL:])
    a = lax.conv_general_dilated(a, w_dw[:, None, :], window_strides=(1,),
                                 padding=[(CONV_WIDTH // 2, CONV_WIDTH // 2)],
                                 dimension_numbers=('NWC', 'WIO', 'NWC'),
                                 feature_group_count=D_MODEL) + b_dw
    a = jax.nn.silu(layer_norm(a, n_g, n_b))
    return a @ w_pw2 + b_pw2


def swa_project(h, wq, wk, wv):
    B, L, _ = h.shape
    q = (h @ wq).reshape(B, L, SWA_KV_HEADS, SWA_GROUP, SWA_HEAD_DIM)
    k = (h @ wk).reshape(B, L, SWA_KV_HEADS, SWA_HEAD_DIM)
    v = (h @ wv).reshape(B, L, SWA_KV_HEADS, SWA_HEAD_DIM)
    return q, k, v


def swa_context(h, wq, wk, wv, sink, wo):
    B, L, _ = h.shape
    q, k, v = swa_project(h, wq, wk, wv)
    o = blocked_attention([(q, k, v)], sink.reshape(SWA_KV_HEADS, SWA_GROUP), SWA_SCALE)
    return o.reshape(B, L, SWA_HEADS * SWA_HEAD_DIM) @ wo, k, v


def swa_latent(h, k_ctx, v_ctx, wq, wk, wv, sink, wo):
    B, L, _ = h.shape
    q, k, v = swa_project(h, wq, wk, wv)
    o = banded_window_attention(axial_rope(q), axial_rope(k), v, q, k_ctx, v_ctx,
                                sink.reshape(SWA_KV_HEADS, SWA_GROUP))
    return o.reshape(B, L, SWA_HEADS * SWA_HEAD_DIM) @ wo


def expert_choice_ffn(h, router_w, w_gate, w_up, w_down):
    B, L, D = h.shape
    n = B * L
    cap = EC_CAPACITY_FACTOR * n // N_EXPERTS
    x = h.reshape(n, D)
    aff = jax.nn.softmax((x @ router_w).astype(jnp.float32), axis=-1)
    gate, idx = lax.top_k(aff.T, cap)
    xe = jnp.take(x, idx, axis=0)
    hid = jax.nn.silu(jnp.einsum('ecd,edf->ecf', xe, w_gate)) * jnp.einsum('ecd,edf->ecf', xe, w_up)
    ye = jnp.einsum('ecf,efd->ecd', hid, w_down) * gate[..., None].astype(x.dtype)
    y = jnp.zeros_like(x).at[idx.reshape(-1)].add(ye.reshape(-1, D))
    return y.reshape(B, L, D)


def modulation(cond, w_mod, b_mod):
    return jax.nn.silu(cond) @ w_mod + b_mod


def setup_inputs(seed: int = 0) -> dict:
    key = jax.random.key(seed)
    ks = iter(jax.random.split(key, 64))

    def nrm(shape, std):
        return std * jax.random.normal(next(ks), shape, jnp.float32)

    def gain(shape):
        return 1.0 + nrm(shape, 0.02)

    D = D_MODEL
    NA, NB, NC, ND = N_LAYERS_MLA, N_LAYERS_GMLP, N_LAYERS_CONV, N_LAYERS_SWA
    return {
        'x_prompt': nrm((BATCH, SEQ, D), 1.0),
        'x_sample': nrm((DEC_BATCH, DEC_SEQ, D), 1.0),
        'c': nrm((DEC_BATCH, D), 1.0),
        'cache_mla_ckv': nrm((DEC_BATCH, NA, PAST_LEN, MLA_KV_LORA), 1.0),
        'cache_mla_kpe': nrm((DEC_BATCH, NA, PAST_LEN, MLA_ROPE), 1.0),
        'cache_swa_k': nrm((DEC_BATCH, ND, PAST_LEN, SWA_KV_HEADS, SWA_HEAD_DIM), 1.0),
        'cache_swa_v': nrm((DEC_BATCH, ND, PAST_LEN, SWA_KV_HEADS, SWA_HEAD_DIM), 1.0),
        'c_ctx': nrm((D,), 1.0),
        'w_mod': nrm((DEPTH, D, N_MOD * D), 0.5 * D ** -0.5),
        'b_mod': nrm((DEPTH, N_MOD * D), 0.02),
        'ln_gain': gain((DEPTH, 2, D)),
        'ln_bias': nrm((DEPTH, 2, D), 0.02),
        'router_w': nrm((DEPTH, D, N_EXPERTS), D ** -0.5),
        'moe_w_gate': nrm((DEPTH, N_EXPERTS, D, EXPERT_FF), D ** -0.5),
        'moe_w_up': nrm((DEPTH, N_EXPERTS, D, EXPERT_FF), D ** -0.5),
        'moe_w_down': nrm((DEPTH, N_EXPERTS, EXPERT_FF, D), BETA * EXPERT_FF ** -0.5),
        'mla_wq_a': nrm((NA, D, MLA_Q_LORA), D ** -0.5),
        'mla_q_norm': gain((NA, MLA_Q_LORA)),
        'mla_wq_b': nrm((NA, MLA_Q_LORA, MLA_HEADS * (MLA_NOPE + MLA_ROPE)), MLA_Q_LORA ** -0.5),
        'mla_wkv_a': nrm((NA, D, MLA_KV_LORA + MLA_ROPE), D ** -0.5),
        'mla_kv_norm': gain((NA, MLA_KV_LORA)),
        'mla_wkv_b': nrm((NA, MLA_KV_LORA, MLA_HEADS * (MLA_NOPE + MLA_V)), MLA_KV_LORA ** -0.5),
        'mla_wo': nrm((NA, MLA_HEADS * MLA_V, D), BETA * (MLA_HEADS * MLA_V) ** -0.5),
        'gm_w_in': nrm((NB, D, 2 * GM_HALF), D ** -0.5),
        'gm_b_in': nrm((NB, 2 * GM_HALF), 0.02),
        'gm_v_norm_g': gain((NB, GM_HALF)),
        'gm_v_norm_b': nrm((NB, GM_HALF), 0.02),
        'gm_w_s': nrm((NB, GM_GROUPS, GM_CHUNK, GM_CHUNK), 0.5 * GM_CHUNK ** -0.5),
        'gm_b_s': 1.0 + nrm((NB, GM_GROUPS, GM_CHUNK), 0.1),
        'gm_w_out': nrm((NB, GM_HALF, D), BETA * GM_HALF ** -0.5),
        'gm_b_out': nrm((NB, D), 0.02),
        'cv_w_pw1': nrm((NC, D, 2 * D), D ** -0.5),
        'cv_b_pw1': nrm((NC, 2 * D), 0.02),
        'cv_w_dw': nrm((NC, CONV_WIDTH, D), CONV_WIDTH ** -0.5),
        'cv_b_dw': nrm((NC, D), 0.02),
        'cv_norm_g': gain((NC, D)),
        'cv_norm_b': nrm((NC, D), 0.02),
        'cv_w_pw2': nrm((NC, D, D), BETA * D ** -0.5),
        'cv_b_pw2': nrm((NC, D), 0.02),
        'swa_wq': nrm((ND, D, SWA_HEADS * SWA_HEAD_DIM), D ** -0.5),
        'swa_wk': nrm((ND, D, SWA_KV_HEADS * SWA_HEAD_DIM), D ** -0.5),
        'swa_wv': nrm((ND, D, SWA_KV_HEADS * SWA_HEAD_DIM), D ** -0.5),
        'swa_sink': nrm((ND, SWA_HEADS), 1.0),
        'swa_wo': nrm((ND, SWA_HEADS * SWA_HEAD_DIM, D), BETA * (SWA_HEADS * SWA_HEAD_DIM) ** -0.5),
    }


def reference(x_prompt, x_sample, c, cache_mla_ckv, cache_mla_kpe, cache_swa_k, cache_swa_v, c_ctx,
              w_mod, b_mod, ln_gain, ln_bias, router_w, moe_w_gate, moe_w_up, moe_w_down,
              mla_wq_a, mla_q_norm, mla_wq_b, mla_wkv_a, mla_kv_norm, mla_wkv_b, mla_wo,
              gm_w_in, gm_b_in, gm_v_norm_g, gm_v_norm_b, gm_w_s, gm_b_s, gm_w_out, gm_b_out,
              cv_w_pw1, cv_b_pw1, cv_w_dw, cv_b_dw, cv_norm_g, cv_norm_b, cv_w_pw2, cv_b_pw2,
              swa_wq, swa_wk, swa_wv, swa_sink, swa_wo):
    xp, xs = x_prompt, x_sample
    ckv_list, kpe_list, k_list, v_list = [], [], [], []
    for i in range(DEPTH):
        kind, j = i % N_MIXERS, i // N_MIXERS
        mp = jnp.split(modulation(c_ctx, w_mod[i], b_mod[i])[None, None, :], N_MOD, axis=-1)
        ms = jnp.split(modulation(c, w_mod[i], b_mod[i])[:, None, :], N_MOD, axis=-1)
        hp = xp * (1.0 + mp[1]) + mp[0]
        hs = xs * (1.0 + ms[1]) + ms[0]
        if kind == 0:
            w = (mla_wq_a[j], mla_q_norm[j], mla_wq_b[j], mla_wkv_a[j], mla_kv_norm[j], mla_wkv_b[j], mla_wo[j])
            dp, ckv, kpe = mla_context(hp, *w)
            ds = mla_latent(hs, cache_mla_ckv[:, j], cache_mla_kpe[:, j], *w)
            ckv_list.append(ckv)
            kpe_list.append(kpe)
        elif kind == 1:
            w = (gm_w_in[j], gm_b_in[j], gm_v_norm_g[j], gm_v_norm_b[j], gm_w_s[j], gm_b_s[j],
                 gm_w_out[j], gm_b_out[j])
            dp = chunk_gmlp(hp, *w)
            ds = chunk_gmlp(hs, *w)
        elif kind == 2:
            w = (cv_w_pw1[j], cv_b_pw1[j], cv_w_dw[j], cv_b_dw[j], cv_norm_g[j], cv_norm_b[j],
                 cv_w_pw2[j], cv_b_pw2[j])
            dp = conformer_conv(hp, *w)
            ds = conformer_conv(hs, *w)
        else:
            w = (swa_wq[j], swa_wk[j], swa_wv[j], swa_sink[j], swa_wo[j])
            dp, k, v = swa_context(hp, *w)
            ds = swa_latent(hs, cache_swa_k[:, j], cache_swa_v[:, j], *w)
            k_list.append(k)
            v_list.append(v)
        xp = layer_norm(ALPHA * xp + mp[2] * dp, ln_gain[i, 0], ln_bias[i, 0])
        xs = layer_norm(ALPHA * xs + ms[2] * ds, ln_gain[i, 0], ln_bias[i, 0])
        hp = xp * (1.0 + mp[4]) + mp[3]
        hs = xs * (1.0 + ms[4]) + ms[3]
        moe = (router_w[i], moe_w_gate[i], moe_w_up[i], moe_w_down[i])
        xp = layer_norm(ALPHA * xp + mp[5] * expert_choice_ffn(hp, *moe), ln_gain[i, 1], ln_bias[i, 1])
        xs = layer_norm(ALPHA * xs + ms[5] * expert_choice_ffn(hs, *moe), ln_gain[i, 1], ln_bias[i, 1])
    y_prompt, y_sample = xp, xs
    state_mla_ckv = jnp.stack(ckv_list, axis=1)
    state_mla_kpe = jnp.stack(kpe_list, axis=1)
    state_swa_k = jnp.stack(k_list, axis=1)
    state_swa_v = jnp.stack(v_list, axis=1)
    return (y_prompt, y_sample, state_mla_ckv, state_mla_kpe, state_swa_k, state_swa_v)
```

```python
import functools
import math

import jax
import jax.numpy as jnp
from jax import lax
from jax.experimental import pallas as pl
from jax.experimental.pallas import tpu as pltpu

f32 = jnp.float32
bf16 = jnp.bfloat16
i32 = jnp.int32

D = 1024
BATCH, SEQ = 32, 256
DEC_BATCH, DEC_SEQ = 2, 2048
PAST = 256
DEPTH = 4
GRID_W = 64
ALPHA = (2 * DEPTH) ** 0.25
LN_EPS = 1e-5
RMS_EPS = 1e-6
ROPE_BASE = 10000.0
NEG_INF = -1e30
MLA_HEADS, MLA_NOPE, MLA_ROPE, MLA_V = 8, 128, 64, 128
MLA_Q_LORA, MLA_KV_LORA = 384, 256
MLA_SCALE = (MLA_NOPE + MLA_ROPE) ** -0.5
GM_CHUNK, GM_HALF, GM_GROUPS = 128, 2048, 4
CONV_W = 31
SWA_HEADS, SWA_KV, SWA_HD, SWA_WIN = 16, 4, 64, 128
SWA_SCALE = SWA_HD ** -0.5
NE = 16
FF = 1024

NP_TOK = BATCH * SEQ
NS_TOK = DEC_BATCH * DEC_SEQ
NT_TOK = NP_TOK + NS_TOK
CAP_P = 2 * NP_TOK // NE
CAP_S = 2 * NS_TOK // NE

LANE = 128
TM = 256
GATHER_TM = 128
GATHER_KB = 256
COMB_T = 256
COMB_W = 64
MIB = 2 ** 20


def _cp(vmem_mb, n_axes=1):
    return pltpu.CompilerParams(dimension_semantics=("arbitrary",) * n_axes,
                                vmem_limit_bytes=int(vmem_mb * MIB))


def _const(shape):
    nd = len(shape)
    return pl.BlockSpec(shape, lambda *_: (0,) * nd, pipeline_mode=pl.Buffered(1))


def _rows(tm, c):
    return pl.BlockSpec((tm, c), lambda i, *_: (i, 0))


def _seg_p(i):
    return 0


def _seg_s(i):
    return 1 + i // (DEC_SEQ // TM)


def _seg_all(i):
    np_tiles = NP_TOK // TM
    return jnp.where(i < np_tiles, 0, 1 + (i - np_tiles) // (DEC_SEQ // TM))


def _mod_spec(seg_of):
    return pl.BlockSpec((None, 8, D), lambda i, *_: (seg_of(i), 0, 0))


def _dot(a, b):
    return jnp.dot(a, b, preferred_element_type=f32)


def _dot_nt(a, b):
    return lax.dot_general(a, b, (((1,), (1,)), ((), ())), preferred_element_type=f32)


def _silu(x):
    return x * jax.nn.sigmoid(x)


def _ln_rows(x, g, b):
    mu = jnp.mean(x, axis=-1, keepdims=True)
    xc = x - mu
    var = jnp.mean(xc * xc, axis=-1, keepdims=True)
    return xc * lax.rsqrt(var + LN_EPS) * g + b


def _rms_rows(x, g):
    return x * lax.rsqrt(jnp.mean(x * x, axis=-1, keepdims=True) + RMS_EPS) * g


def _split_bf16(x):
    hi = x.astype(bf16)
    lo = (x - hi.astype(f32)).astype(bf16)
    return hi, lo


def _mod_kernel(c_ref, w_ref, b_ref, o_ref):
    x = _silu(c_ref[...])
    xh, xl = _split_bf16(x)
    wh, wl = _split_bf16(w_ref[...])
    o_ref[...] = _dot(xh, wh) + _dot(xh, wl) + _dot(xl, wh) + b_ref[...]


def _modulation(cond8, w_mod, b_mod):
    nk = 6
    out = pl.pallas_call(
        _mod_kernel,
        grid=(DEPTH, nk),
        in_specs=[pl.BlockSpec((8, D), lambda l, k: (0, 0)),
                  pl.BlockSpec((None, D, D), lambda l, k: (l, 0, k)),
                  pl.BlockSpec((None, None, 1, D), lambda l, k: (l, k, 0, 0))],
        out_specs=pl.BlockSpec((None, None, 8, D), lambda l, k: (l, k, 0, 0)),
        out_shape=jax.ShapeDtypeStruct((DEPTH, nk, 8, D), f32),
        compiler_params=_cp(32, 2),
        name="modulation",
    )(cond8, w_mod, b_mod.reshape(DEPTH, nk, 1, D))
    out = jnp.transpose(out, (0, 2, 1, 3))
    return jnp.pad(out, ((0, 0), (0, 0), (0, 2), (0, 0)))


def _router_aff_t(h2, rwt_ref):
    hh, hl = _split_bf16(h2)
    wh, wl = _split_bf16(rwt_ref[...])
    lg = _dot_nt(wh, hh) + _dot_nt(wh, hl) + _dot_nt(wl, hh)
    m = jnp.max(lg, axis=0, keepdims=True)
    e = jnp.exp(lg - m)
    return e / jnp.sum(e, axis=0, keepdims=True)


def _close(d, x, m, lnp, rwt_ref, x1_ref, h2_ref, aff_ref):
    x1 = _ln_rows(ALPHA * x + m[2:3] * d, lnp[0:1], lnp[1:2])
    x1_ref[...] = x1
    h2 = x1 * (1.0 + m[4:5]) + m[3:4]
    h2_ref[...] = h2.astype(bf16)
    aff = _router_aff_t(h2, rwt_ref)
    for j in range(aff.shape[1] // LANE):
        aff_ref[j] = aff[:, j * LANE:(j + 1) * LANE]


def _close_outs(n):
    shapes = (jax.ShapeDtypeStruct((n, D), f32), jax.ShapeDtypeStruct((n, D), bf16),
              jax.ShapeDtypeStruct((n // LANE, NE, LANE), f32))
    specs = (_rows(TM, D), _rows(TM, D),
             pl.BlockSpec((TM // LANE, NE, LANE), lambda i, *_: (i, 0, 0)))
    return shapes, specs


def _proj_close_kernel(o_ref, x_ref, mod_ref, lnp_ref, wo_ref, rwt_ref, x1_ref, h2_ref, aff_ref):
    d = _dot(o_ref[...], wo_ref[...])
    _close(d, x_ref[...], mod_ref[...], lnp_ref[...], rwt_ref, x1_ref, h2_ref, aff_ref)


def _proj_close(o, x, mod, lnp, wo, rwt):
    n, c = o.shape
    shapes, specs = _close_outs(n)
    return pl.pallas_call(
        _proj_close_kernel,
        grid=(n // TM,),
        in_specs=[_rows(TM, c), _rows(TM, D), _mod_spec(_seg_all), _const((8, D)),
                  _const(wo.shape), _const((NE, D))],
        out_specs=specs, out_shape=shapes,
        compiler_params=_cp(40), name="proj_close",
    )(o, x, mod, lnp, wo, rwt)


def _rope(x, cos, sin):
    w = x.shape[1]
    reps = w // LANE
    c = jnp.concatenate([cos] * reps, axis=1) if reps > 1 else cos
    s = jnp.concatenate([sin] * reps, axis=1) if reps > 1 else sin
    lane = lax.broadcasted_iota(i32, x.shape, 1)
    up = pltpu.roll(x, w - 16, 1)
    dn = pltpu.roll(x, 16, 1)
    partner = jnp.where((lane % 32) < 16, up, dn)
    return x * c + partner * s


def _rope_tables(length):
    t = jnp.arange(length)
    rows, cols = (t // GRID_W).astype(f32), (t % GRID_W).astype(f32)
    inv = ROPE_BASE ** (-jnp.arange(16, dtype=f32) / 16)
    ar, ac = rows[:, None] * inv[None, :], cols[:, None] * inv[None, :]
    cos = jnp.concatenate([jnp.cos(ar), jnp.cos(ar), jnp.cos(ac), jnp.cos(ac)], axis=1)
    sin = jnp.concatenate([-jnp.sin(ar), jnp.sin(ar), -jnp.sin(ac), jnp.sin(ac)], axis=1)
    return jnp.concatenate([cos, cos], axis=1), jnp.concatenate([sin, sin], axis=1)


def _mla_pre_kernel(rope, x_ref, mod_ref, wqa_ref, qn_ref, wqb_ref, wkva_ref, kvn_ref, wkvb_ref, *rest):
    if rope:
        cos_ref, sin_ref, qlat_ref, qctx_ref, kcat_ref, v_ref = rest
    else:
        qctx_ref, kcat_ref, v_ref, ckv_ref, kpe_ref = rest
    m = mod_ref[...]
    h = (x_ref[...] * (1.0 + m[1:2]) + m[0:1]).astype(bf16)
    qa = _rms_rows(_dot(h, wqa_ref[...]), qn_ref[...])
    q = _dot(qa.astype(bf16), wqb_ref[...])
    kv = _dot(h, wkva_ref[...])
    ckv = _rms_rows(kv[:, :MLA_KV_LORA], kvn_ref[...])
    kpe2 = kv[:, MLA_KV_LORA:MLA_KV_LORA + LANE]
    kvb = _dot(ckv.astype(bf16), wkvb_ref[...])
    q_pe = q[:, MLA_HEADS * MLA_NOPE:]
    if rope:
        cos, sin = cos_ref[...], sin_ref[...]
        q_pe_rot = _rope(q_pe, cos, sin).astype(bf16)
        kpe2 = _rope(kpe2, cos, sin)
    else:
        ckv_ref[...] = ckv
        kpe_ref[...] = kv[:, MLA_KV_LORA:MLA_KV_LORA + MLA_ROPE]
    q_pe = q_pe.astype(bf16)
    qn = q[:, :MLA_HEADS * MLA_NOPE].astype(bf16)
    kn = kvb[:, :MLA_HEADS * MLA_NOPE].astype(bf16)
    v_ref[...] = kvb[:, MLA_HEADS * MLA_NOPE:].astype(bf16)
    lane = lax.broadcasted_iota(i32, kpe2.shape, 1)
    kpe_lo = jnp.where(lane < MLA_ROPE, kpe2, 0.0).astype(bf16)
    kpe_hi = jnp.where(lane >= MLA_ROPE, kpe2, 0.0).astype(bf16)
    for hh in range(MLA_HEADS):
        a, b = hh * 256, hh * 256 + LANE
        pr = (hh // 2) * LANE
        qctx_ref[:, a:b] = qn[:, hh * LANE:(hh + 1) * LANE]
        qctx_ref[:, b:b + LANE] = q_pe[:, pr:pr + LANE]
        if rope:
            qlat_ref[:, a:b] = qn[:, hh * LANE:(hh + 1) * LANE]
            qlat_ref[:, b:b + LANE] = q_pe_rot[:, pr:pr + LANE]
        kcat_ref[:, a:b] = kn[:, hh * LANE:(hh + 1) * LANE]
        kcat_ref[:, b:b + LANE] = kpe_lo if hh % 2 == 0 else kpe_hi


def _mla_pre(x, mod, w, rope_tabs, seg_of):
    n = x.shape[0]
    wqa, qn, wqb, wkva, kvn, wkvb = w
    ins = [x, mod, wqa, qn, wqb, wkva, kvn, wkvb]
    specs = [_rows(TM, D), _mod_spec(seg_of)] + [_const(a.shape) for a in ins[2:]]
    wide = jax.ShapeDtypeStruct((n, 2 * D), bf16)
    if rope_tabs is not None:
        per = DEC_SEQ // TM
        tab = pl.BlockSpec((TM, LANE), lambda i: (i % per, 0))
        ins += list(rope_tabs)
        specs += [tab, tab]
        shapes = (wide, wide, wide, jax.ShapeDtypeStruct((n, D), bf16))
        ospecs = (_rows(TM, 2 * D), _rows(TM, 2 * D), _rows(TM, 2 * D), _rows(TM, D))
    else:
        shapes = (wide, wide, jax.ShapeDtypeStruct((n, D), bf16),
                  jax.ShapeDtypeStruct((n, MLA_KV_LORA), f32), jax.ShapeDtypeStruct((n, MLA_ROPE), f32))
        ospecs = (_rows(TM, 2 * D), _rows(TM, 2 * D), _rows(TM, D), _rows(TM, MLA_KV_LORA), _rows(TM, MLA_ROPE))
    return pl.pallas_call(
        functools.partial(_mla_pre_kernel, rope_tabs is not None),
        grid=(n // TM,), in_specs=specs, out_specs=ospecs, out_shape=shapes,
        compiler_params=_cp(40), name="mla_pre",
    )(*ins)


def _mla_ctx_kernel(ckv_ref, kpe2_ref, wkvb_ref, kcat_ref, v_ref):
    kvb = _dot(ckv_ref[...].astype(bf16), wkvb_ref[...])
    kn = kvb[:, :MLA_HEADS * MLA_NOPE].astype(bf16)
    v_ref[...] = kvb[:, MLA_HEADS * MLA_NOPE:].astype(bf16)
    kpe2 = kpe2_ref[...]
    lane = lax.broadcasted_iota(i32, kpe2.shape, 1)
    kpe_lo = jnp.where(lane < MLA_ROPE, kpe2, 0.0).astype(bf16)
    kpe_hi = jnp.where(lane >= MLA_ROPE, kpe2, 0.0).astype(bf16)
    for hh in range(MLA_HEADS):
        a, b = hh * 256, hh * 256 + LANE
        kcat_ref[:, a:b] = kn[:, hh * LANE:(hh + 1) * LANE]
        kcat_ref[:, b:b + LANE] = kpe_lo if hh % 2 == 0 else kpe_hi


def _mla_ctx(ckv, kpe2, wkvb):
    n = ckv.shape[0]
    return pl.pallas_call(
        _mla_ctx_kernel, grid=(n // TM,),
        in_specs=[_rows(TM, MLA_KV_LORA), _rows(TM, LANE), _const(wkvb.shape)],
        out_specs=(_rows(TM, 2 * D), _rows(TM, D)),
        out_shape=(jax.ShapeDtypeStruct((n, 2 * D), bf16), jax.ShapeDtypeStruct((n, D), bf16)),
        compiler_params=_cp(24), name="mla_ctx",
    )(ckv, kpe2, wkvb)


def _mla_attn_p_kernel(q_ref, k_ref, v_ref, o_ref):
    for hh in range(MLA_HEADS):
        s = _dot_nt(q_ref[:, hh * 256:(hh + 1) * 256], k_ref[:, hh * 256:(hh + 1) * 256]) * MLA_SCALE
        e = jnp.exp(s - jnp.max(s, axis=-1, keepdims=True))
        l = jnp.sum(e, axis=-1, keepdims=True)
        o = _dot(e.astype(bf16), v_ref[:, hh * LANE:(hh + 1) * LANE]) / l
        o_ref[:, hh * LANE:(hh + 1) * LANE] = o.astype(bf16)


def _mla_attn_p(q, k, v):
    n = q.shape[0]
    return pl.pallas_call(
        _mla_attn_p_kernel, grid=(n // SEQ,),
        in_specs=[_rows(SEQ, 2 * D), _rows(SEQ, 2 * D), _rows(SEQ, D)],
        out_specs=_rows(SEQ, D), out_shape=jax.ShapeDtypeStruct((n, D), bf16),
        compiler_params=_cp(24), name="mla_attn_ctx",
    )(q, k, v)


def _mla_attn_s_kernel(ql_ref, qc_ref, kl_ref, vl_ref, kc_ref, vc_ref, o_ref):
    for hh in range(MLA_HEADS):
        a, b = hh * 256, (hh + 1) * 256
        s1 = _dot_nt(ql_ref[:, a:b], kl_ref[:, a:b]) * MLA_SCALE
        s2 = _dot_nt(qc_ref[:, a:b], kc_ref[:, a:b]) * MLA_SCALE
        m = jnp.maximum(jnp.max(s1, axis=-1, keepdims=True), jnp.max(s2, axis=-1, keepdims=True))
        e1, e2 = jnp.exp(s1 - m), jnp.exp(s2 - m)
        l = jnp.sum(e1, axis=-1, keepdims=True) + jnp.sum(e2, axis=-1, keepdims=True)
        o = (_dot(e1.astype(bf16), vl_ref[:, hh * LANE:(hh + 1) * LANE])
             + _dot(e2.astype(bf16), vc_ref[:, hh * LANE:(hh + 1) * LANE])) / l
        o_ref[:, hh * LANE:(hh + 1) * LANE] = o.astype(bf16)


def _mla_attn_s(ql, qc, kl, vl, kc, vc):
    nq = DEC_SEQ // TM
    qs = pl.BlockSpec((TM, 2 * D), lambda b, i: (b * nq + i, 0))
    return pl.pallas_call(
        _mla_attn_s_kernel, grid=(DEC_BATCH, nq),
        in_specs=[qs, qs,
                  pl.BlockSpec((DEC_SEQ, 2 * D), lambda b, i: (b, 0)),
                  pl.BlockSpec((DEC_SEQ, D), lambda b, i: (b, 0)),
                  pl.BlockSpec((PAST, 2 * D), lambda b, i: (b, 0)),
                  pl.BlockSpec((PAST, D), lambda b, i: (b, 0))],
        out_specs=pl.BlockSpec((TM, D), lambda b, i: (b * nq + i, 0)),
        out_shape=jax.ShapeDtypeStruct((NS_TOK, D), bf16),
        compiler_params=_cp(48, 2), name="mla_attn_lat",
    )(ql, qc, kl, vl, kc, vc)


def _gelu_tanh(x):
    return 0.5 * x * (1.0 + jnp.tanh(math.sqrt(2.0 / math.pi) * (x + 0.044715 * (x * x * x))))


def _gmlp_kernel(x_ref, mod_ref, lnp_ref, win_ref, bin_ref, vg_ref, vb_ref, ws_ref, bs_ref, wout_ref, bout_ref,
                 rwt_ref, x1_ref, h2_ref, aff_ref, gated_ref):
    m = mod_ref[...]
    x = x_ref[...]
    h = (x * (1.0 + m[1:2]) + m[0:1]).astype(bf16)
    z = _gelu_tanh(_dot(h, win_ref[...]) + bin_ref[...])
    u = z[:, :GM_HALF]
    v = _ln_rows(z[:, GM_HALF:], vg_ref[...], vb_ref[...]).astype(bf16)
    gw = GM_HALF // GM_GROUPS
    for c in range(TM // GM_CHUNK):
        r0, r1 = c * GM_CHUNK, (c + 1) * GM_CHUNK
        for g in range(GM_GROUPS):
            sv = _dot(ws_ref[g], v[r0:r1, g * gw:(g + 1) * gw]) + bs_ref[:, g:g + 1]
            gated_ref[r0:r1, g * gw:(g + 1) * gw] = (u[r0:r1, g * gw:(g + 1) * gw] * sv).astype(bf16)
    d = _dot(gated_ref[...], wout_ref[...]) + bout_ref[...]
    _close(d, x, m, lnp_ref[...], rwt_ref, x1_ref, h2_ref, aff_ref)


def _gmlp_layer(x, mod, lnp, w, rwt):
    n = x.shape[0]
    shapes, specs = _close_outs(n)
    return pl.pallas_call(
        _gmlp_kernel, grid=(n // TM,),
        in_specs=[_rows(TM, D), _mod_spec(_seg_all), _const((8, D))] + [_const(a.shape) for a in w]
        + [_const((NE, D))],
        out_specs=specs, out_shape=shapes,
        scratch_shapes=[pltpu.VMEM((TM, GM_HALF), bf16)],
        compiler_params=_cp(52), name="gmlp_layer",
    )(x, mod, lnp, *w, rwt)


HALO = 16
CONV_RB = 32


def _conv_glu_kernel(x_ref, mod_ref, w_ref, b_ref, a_ref):
    m = mod_ref[...]
    h = (x_ref[...] * (1.0 + m[1:2]) + m[0:1]).astype(bf16)
    a = _dot(h, w_ref[...]) + b_ref[...]
    a_ref[...] = a[:, :D] * jax.nn.sigmoid(a[:, D:])


def _conv_glu(x, mod, w, b):
    n = x.shape[0]
    return pl.pallas_call(
        _conv_glu_kernel, grid=(n // TM,),
        in_specs=[_rows(TM, D), _mod_spec(_seg_all), _const(w.shape), _const(b.shape)],
        out_specs=_rows(TM, D), out_shape=jax.ShapeDtypeStruct((n, D), f32),
        compiler_params=_cp(32), name="conv_glu",
    )(x, mod, w, b)


def _conv_close_kernel(seq_tiles, ap_ref, a_ref, an_ref, x_ref, mod_ref, lnp_ref, wdw_ref, bdw_ref, ng_ref, nb_ref,
                       w2_ref, b2_ref, rwt_ref, x1_ref, h2_ref, aff_ref, pad_ref, act_ref):
    i = pl.program_id(0)
    has_prev = (i % seq_tiles) != 0
    has_next = (i % seq_tiles) != seq_tiles - 1
    pad_ref[0:HALO, :] = jnp.where(has_prev, ap_ref[...], 0.0)
    pad_ref[HALO:HALO + TM, :] = a_ref[...]
    pad_ref[HALO + TM:, :] = jnp.where(has_next, an_ref[...], 0.0)
    bdw, ng, nb = bdw_ref[...], ng_ref[...], nb_ref[...]
    for r in range(TM // CONV_RB):
        base = r * CONV_RB + HALO - CONV_W // 2
        acc = jnp.zeros((CONV_RB, D), f32)
        for k in range(CONV_W):
            acc = acc + wdw_ref[k:k + 1, :] * pad_ref[base + k:base + k + CONV_RB, :]
        y = _silu(_ln_rows(acc + bdw, ng, nb))
        act_ref[r * CONV_RB:(r + 1) * CONV_RB, :] = y.astype(bf16)
    d = _dot(act_ref[...], w2_ref[...]) + b2_ref[...]
    _close(d, x_ref[...], mod_ref[...], lnp_ref[...], rwt_ref, x1_ref, h2_ref, aff_ref)


def _conv_close(a, x, mod, lnp, w, rwt, seq_len, seg_of):
    n = a.shape[0]
    hb = TM // HALO
    last = n // HALO - 1
    shapes, specs = _close_outs(n)
    return pl.pallas_call(
        functools.partial(_conv_close_kernel, seq_len // TM), grid=(n // TM,),
        in_specs=[pl.BlockSpec((HALO, D), lambda i: (jnp.maximum(i * hb - 1, 0), 0)),
                  _rows(TM, D),
                  pl.BlockSpec((HALO, D), lambda i: (jnp.minimum((i + 1) * hb, last), 0)),
                  _rows(TM, D), _mod_spec(seg_of), _const((8, D))]
        + [_const(t.shape) for t in w] + [_const((NE, D))],
        out_specs=specs, out_shape=shapes,
        scratch_shapes=[pltpu.VMEM((TM + 2 * HALO, D), f32), pltpu.VMEM((TM, D), bf16)],
        compiler_params=_cp(32), name="conv_close",
    )(a, a, a, x, mod, lnp, *w, rwt)


def _swa_pre_kernel(rope, x_ref, mod_ref, wq_ref, wk_ref, wv_ref, *rest):
    if rope:
        cos_ref, sin_ref, qrot_ref, qraw_ref, kd_ref, vd_ref = rest
    else:
        qraw_ref, kd_ref, vd_ref, k_ref, v_ref = rest
    m = mod_ref[...]
    h = (x_ref[...] * (1.0 + m[1:2]) + m[0:1]).astype(bf16)
    q = _dot(h, wq_ref[...])
    kd = _dot(h, wk_ref[...])
    vd = _dot(h, wv_ref[...])
    qraw_ref[...] = q.astype(bf16)
    vd_ref[...] = vd.astype(bf16)
    if rope:
        cos, sin = cos_ref[...], sin_ref[...]
        qrot_ref[...] = _rope(q, cos, sin).astype(bf16)
        kd_ref[...] = _rope(kd, cos, sin).astype(bf16)
    else:
        kd_ref[...] = kd.astype(bf16)
        lane = lax.broadcasted_iota(i32, (TM, LANE), 1)
        for j in range(SWA_KV // 2):
            lo, hi = 2 * j * LANE, (2 * j + 1) * LANE
            k_ref[:, j * LANE:(j + 1) * LANE] = jnp.where(lane < SWA_HD, kd[:, lo:lo + LANE], kd[:, hi:hi + LANE])
            v_ref[:, j * LANE:(j + 1) * LANE] = jnp.where(lane < SWA_HD, vd[:, lo:lo + LANE], vd[:, hi:hi + LANE])


def _swa_pre(x, mod, w, rope_tabs, seg_of):
    n = x.shape[0]
    ins = [x, mod] + list(w)
    specs = [_rows(TM, D), _mod_spec(seg_of)] + [_const(a.shape) for a in w]
    qs, ks = jax.ShapeDtypeStruct((n, D), bf16), jax.ShapeDtypeStruct((n, 2 * SWA_KV * SWA_HD), bf16)
    kw = 2 * SWA_KV * SWA_HD
    if rope_tabs is not None:
        per = DEC_SEQ // TM
        tab = pl.BlockSpec((TM, LANE), lambda i: (i % per, 0))
        ins += list(rope_tabs)
        specs += [tab, tab]
        shapes = (qs, qs, ks, ks)
        ospecs = (_rows(TM, D), _rows(TM, D), _rows(TM, kw), _rows(TM, kw))
    else:
        nat = jax.ShapeDtypeStruct((n, SWA_KV * SWA_HD), f32)
        shapes = (qs, ks, ks, nat, nat)
        ospecs = (_rows(TM, D), _rows(TM, kw), _rows(TM, kw), _rows(TM, SWA_KV * SWA_HD), _rows(TM, SWA_KV * SWA_HD))
    return pl.pallas_call(
        functools.partial(_swa_pre_kernel, rope_tabs is not None),
        grid=(n // TM,), in_specs=specs, out_specs=ospecs, out_shape=shapes,
        compiler_params=_cp(32), name="swa_pre",
    )(*ins)


def _half_mask(x, parity):
    lane = lax.broadcasted_iota(i32, x.shape, 1)
    keep = (lane < SWA_HD) if parity == 0 else (lane >= SWA_HD)
    return jnp.where(keep, x, jnp.zeros_like(x))


def _swa_attn_p_kernel(sink_ref, q_ref, kd_ref, vd_ref, o_ref):
    for pair in range(SWA_HEADS // 2):
        g = (2 * pair) // (SWA_HEADS // SWA_KV)
        qp = q_ref[:, pair * LANE:(pair + 1) * LANE]
        kd = kd_ref[:, g * LANE:(g + 1) * LANE]
        vd = vd_ref[:, g * LANE:(g + 1) * LANE]
        acc = jnp.zeros((SEQ, LANE), f32)
        for par in range(2):
            s = _dot_nt(qp, _half_mask(kd, par)) * SWA_SCALE
            sk = sink_ref[2 * pair + par]
            m = jnp.maximum(jnp.max(s, axis=-1, keepdims=True), sk)
            e = jnp.exp(s - m)
            l = jnp.sum(e, axis=-1, keepdims=True) + jnp.exp(sk - m)
            acc = acc + _dot(e.astype(bf16), _half_mask(vd, par)) / l
        o_ref[:, pair * LANE:(pair + 1) * LANE] = acc.astype(bf16)


def _swa_attn_p(sink, q, kd, vd):
    n = q.shape[0]
    kw = kd.shape[1]
    return pl.pallas_call(
        _swa_attn_p_kernel,
        grid_spec=pltpu.PrefetchScalarGridSpec(
            num_scalar_prefetch=1, grid=(n // SEQ,),
            in_specs=[_rows(SEQ, D), _rows(SEQ, kw), _rows(SEQ, kw)],
            out_specs=_rows(SEQ, D)),
        out_shape=jax.ShapeDtypeStruct((n, D), bf16),
        compiler_params=_cp(24), name="swa_attn_ctx",
    )(sink, q, kd, vd)


SWA_QB = 128


def _swa_attn_s_kernel(sink_ref, qr_ref, qw_ref, kd_ref, vd_ref, kc_ref, vc_ref, o_ref):
    nblk = pl.program_id(1)
    span = 3 * SWA_QB
    start = pl.multiple_of(jnp.clip((nblk - 1) * SWA_QB, 0, DEC_SEQ - span), SWA_QB)
    qpos = nblk * SWA_QB + lax.broadcasted_iota(i32, (SWA_QB, span), 0)
    kpos = start + lax.broadcasted_iota(i32, (SWA_QB, span), 1)
    band = jnp.abs(kpos - qpos) <= SWA_WIN
    for pair in range(SWA_HEADS // 2):
        g = (2 * pair) // (SWA_HEADS // SWA_KV)
        qr = qr_ref[:, pair * LANE:(pair + 1) * LANE]
        qw = qw_ref[:, pair * LANE:(pair + 1) * LANE]
        kd = kd_ref[pl.ds(start, span), g * LANE:(g + 1) * LANE]
        vd = vd_ref[pl.ds(start, span), g * LANE:(g + 1) * LANE]
        kc = kc_ref[:, g * LANE:(g + 1) * LANE]
        vc = vc_ref[:, g * LANE:(g + 1) * LANE]
        acc = jnp.zeros((SWA_QB, LANE), f32)
        for par in range(2):
            s1 = jnp.where(band, _dot_nt(qr, _half_mask(kd, par)) * SWA_SCALE, NEG_INF)
            s2 = _dot_nt(qw, _half_mask(kc, par)) * SWA_SCALE
            sk = sink_ref[2 * pair + par]
            m = jnp.maximum(jnp.maximum(jnp.max(s1, axis=-1, keepdims=True),
                                        jnp.max(s2, axis=-1, keepdims=True)), sk)
            e1, e2 = jnp.exp(s1 - m), jnp.exp(s2 - m)
            l = jnp.sum(e1, axis=-1, keepdims=True) + jnp.sum(e2, axis=-1, keepdims=True) + jnp.exp(sk - m)
            acc = acc + (_dot(e1.astype(bf16), _half_mask(vd, par))
                         + _dot(e2.astype(bf16), _half_mask(vc, par))) / l
        o_ref[:, pair * LANE:(pair + 1) * LANE] = acc.astype(bf16)


def _swa_attn_s(sink, qr, qw, kd, vd, kc, vc):
    nq = DEC_SEQ // SWA_QB
    kw = kd.shape[1]
    qs = pl.BlockSpec((SWA_QB, D), lambda b, i, *_: (b * nq + i, 0))
    full = pl.BlockSpec((DEC_SEQ, kw), lambda b, i, *_: (b, 0))
    ctx = pl.BlockSpec((PAST, kw), lambda b, i, *_: (b, 0))
    return pl.pallas_call(
        _swa_attn_s_kernel,
        grid_spec=pltpu.PrefetchScalarGridSpec(
            num_scalar_prefetch=1, grid=(DEC_BATCH, nq),
            in_specs=[qs, qs, full, full, ctx, ctx],
            out_specs=pl.BlockSpec((SWA_QB, D), lambda b, i, *_: (b * nq + i, 0))),
        out_shape=jax.ShapeDtypeStruct((NS_TOK, D), bf16),
        compiler_params=_cp(32, 2), name="swa_attn_lat",
    )(sink, qr, qw, kd, vd, kc, vc)


def _excl_prefix(mask_f, nb, tri, blk):
    m2 = mask_f.reshape(nb * NE, LANE)
    within = _dot(m2.astype(bf16), tri)
    tot = jnp.sum(m2, axis=1, keepdims=True)
    totb = jnp.broadcast_to(tot, (nb * NE, LANE)).astype(bf16)
    offs = _dot(blk, totb)
    return (within + offs).reshape(nb, NE, LANE), offs.reshape(nb, NE, LANE)


def _route_kernel(cap, nb, aff_ref, dest_ref, gate_ref, off_ref):
    a = aff_ref[...]

    def as_f32(bits):
        return pltpu.bitcast(bits, f32)[None]

    def count_ge(th):
        c = jnp.sum(jnp.where(a >= as_f32(th), 1.0, 0.0), axis=0)
        return jnp.sum(c, axis=1, keepdims=True)

    def body(_, c):
        lo, hi = c
        mid = lo + ((hi - lo + 1) >> 1)
        ok = count_ge(mid) >= cap
        return jnp.where(ok, mid, lo), jnp.where(ok, hi, mid - 1)

    lo0 = jnp.zeros((NE, 1), i32)
    hi0 = jnp.full((NE, 1), 0x7F800000, i32)
    thr, _ = lax.fori_loop(0, 31, body, (lo0, hi0))

    r = lax.broadcasted_iota(i32, (LANE, LANE), 0)
    c = lax.broadcasted_iota(i32, (LANE, LANE), 1)
    tri = jnp.where(r < c, 1.0, 0.0).astype(bf16)
    rr = lax.broadcasted_iota(i32, (nb * NE, nb * NE), 0)
    cc = lax.broadcasted_iota(i32, (nb * NE, nb * NE), 1)
    blk = jnp.where(((rr % NE) == (cc % NE)) & (cc < rr), 1.0, 0.0).astype(bf16)

    gt = a > as_f32(thr)
    eq = a == as_f32(thr)
    n_gt = jnp.sum(jnp.sum(jnp.where(gt, 1.0, 0.0), axis=0), axis=1, keepdims=True)
    need = cap - n_gt
    tie_rank, _ = _excl_prefix(jnp.where(eq, 1.0, 0.0), nb, tri, blk)
    sel = gt | (eq & (tie_rank < need[None]))
    pos, offs = _excl_prefix(jnp.where(sel, 1.0, 0.0), nb, tri, blk)
    dest_ref[...] = jnp.where(sel, pos, -1.0).astype(i32)
    gate_ref[...] = jnp.where(sel, a, 0.0)
    off_ref[...] = offs.astype(i32)


def _route(aff3, cap):
    nb = aff3.shape[0]
    full = pl.BlockSpec((nb, NE, LANE), lambda: (0, 0, 0))
    return pl.pallas_call(
        functools.partial(_route_kernel, cap, nb),
        in_specs=[full], out_specs=(full, full, full),
        out_shape=(jax.ShapeDtypeStruct((nb, NE, LANE), i32), jax.ShapeDtypeStruct((nb, NE, LANE), f32),
                   jax.ShapeDtypeStruct((nb, NE, LANE), i32)),
        compiler_params=pltpu.CompilerParams(vmem_limit_bytes=40 * MIB), name="route",
    )(aff3)


def _route_tables(dest3, gate3, off3, n, cap):
    nkb = n // GATHER_KB
    dest_e = jnp.transpose(dest3, (1, 0, 2)).reshape(NE, nkb, GATHER_KB)
    gate_e = jnp.transpose(gate3, (1, 0, 2)).reshape(NE, nkb, GATHER_KB)
    cb = jnp.concatenate([off3[::GATHER_KB // LANE, :, 0], jnp.full((1, NE), cap, i32)], axis=0).T
    j0 = (jnp.arange(cap // GATHER_TM, dtype=i32) * GATHER_TM)[None, :, None]
    lo = jnp.sum((cb[:, None, 1:] <= j0).astype(i32), axis=2)
    hi = jnp.sum((cb[:, None, :-1] < j0 + GATHER_TM).astype(i32), axis=2) - 1
    ct = off3[::COMB_T // LANE, :, 0].T
    ct_end = jnp.concatenate([ct[:, 1:], jnp.full((NE, 1), cap, i32)], axis=1)
    wstart = (ct // 16) * 16
    nq = jnp.max((ct_end - wstart + COMB_W - 1) // COMB_W, axis=0)
    destp = jnp.transpose(dest3, (0, 2, 1)).reshape(n, NE) + 1
    destab = jnp.concatenate([destp // 32, destp % 32], axis=1).astype(bf16)
    return dest_e, gate_e, lo, hi, wstart.astype(i32), nq.astype(i32), destab


def _gather_kernel(n_tiles, lo_ref, hi_ref, h2_ref, dest_ref, gate_ref, xe_ref, gcol_ref, acc_ref):
    e = pl.program_id(0)
    for j in range(n_tiles):
        acc_ref[...] = jnp.zeros_like(acc_ref)
        rowid = j * GATHER_TM + lax.broadcasted_iota(i32, (GATHER_TM, GATHER_KB), 0)

        def body(kb, g):
            hit = dest_ref[pl.ds(kb, 1), :] == rowid
            onehot = jnp.where(hit, 1.0, 0.0).astype(bf16)
            start = pl.multiple_of(kb * GATHER_KB, GATHER_KB)
            acc_ref[...] += _dot(onehot, h2_ref[pl.ds(start, GATHER_KB), :])
            return g + jnp.sum(jnp.where(hit, gate_ref[pl.ds(kb, 1), :], 0.0), axis=1, keepdims=True)

        g = lax.fori_loop(lo_ref[e, j], hi_ref[e, j] + 1, body, jnp.zeros((GATHER_TM, 1), f32))
        xe_ref[j * GATHER_TM:(j + 1) * GATHER_TM, :] = acc_ref[...].astype(bf16)
        gcol_ref[j * GATHER_TM:(j + 1) * GATHER_TM, :] = jnp.broadcast_to(g, (GATHER_TM, LANE))


def _gather(lo, hi, h2, dest_e, gate_e, cap):
    n = h2.shape[0]
    nkb = n // GATHER_KB
    tab = pl.BlockSpec((None, nkb, GATHER_KB), lambda e, *_: (e, 0, 0))
    return pl.pallas_call(
        functools.partial(_gather_kernel, cap // GATHER_TM),
        grid_spec=pltpu.PrefetchScalarGridSpec(
            num_scalar_prefetch=2, grid=(NE,),
            in_specs=[_const((n, D)), tab, tab],
            out_specs=(pl.BlockSpec((cap, D), lambda e, *_: (e, 0)), pl.BlockSpec((cap, LANE), lambda e, *_: (e, 0))),
            scratch_shapes=[pltpu.VMEM((GATHER_TM, D), f32)]),
        out_shape=(jax.ShapeDtypeStruct((NE * cap, D), bf16), jax.ShapeDtypeStruct((NE * cap, LANE), f32)),
        compiler_params=_cp(40), name="moe_gather",
    )(lo, hi, h2, dest_e, gate_e)


FFN_RB = 256


def _ffn_kernel(xp_ref, gp_ref, xs_ref, gs_ref, wg_ref, wu_ref, wd_ref, yp_ref, ys_ref, wgb, wub, wdb):
    wgb[...] = wg_ref[...].astype(bf16)
    wub[...] = wu_ref[...].astype(bf16)
    wdb[...] = wd_ref[...].astype(bf16)
    for x_ref, g_ref, y_ref, cap in ((xp_ref, gp_ref, yp_ref, CAP_P), (xs_ref, gs_ref, ys_ref, CAP_S)):
        for r in range(cap // FFN_RB):
            sl = slice(r * FFN_RB, (r + 1) * FFN_RB)
            x = x_ref[sl, :]
            hid = (_silu(_dot(x, wgb[...])) * _dot(x, wub[...])).astype(bf16)
            gate = jnp.concatenate([g_ref[sl, :]] * (D // LANE), axis=1)
            y_ref[sl, :] = (_dot(hid, wdb[...]) * gate).astype(bf16)


def _ffn(xp, gp, xs, gs, wg, wu, wd):
    wspec = pl.BlockSpec((None, D, FF), lambda e: (e, 0, 0))
    return pl.pallas_call(
        _ffn_kernel, grid=(NE,),
        in_specs=[pl.BlockSpec((CAP_P, D), lambda e: (e, 0)), pl.BlockSpec((CAP_P, LANE), lambda e: (e, 0)),
                  pl.BlockSpec((CAP_S, D), lambda e: (e, 0)), pl.BlockSpec((CAP_S, LANE), lambda e: (e, 0)),
                  wspec, wspec, pl.BlockSpec((None, FF, D), lambda e: (e, 0, 0))],
        out_specs=(pl.BlockSpec((CAP_P, D), lambda e: (e, 0)), pl.BlockSpec((CAP_S, D), lambda e: (e, 0))),
        out_shape=(jax.ShapeDtypeStruct((NE * CAP_P, D), bf16), jax.ShapeDtypeStruct((NE * CAP_S, D), bf16)),
        scratch_shapes=[pltpu.VMEM((D, FF), bf16), pltpu.VMEM((D, FF), bf16), pltpu.VMEM((FF, D), bf16)],
        compiler_params=_cp(56), name="moe_ffn",
    )(xp, gp, xs, gs, wg, wu, wd)


def _combine_kernel(cap, ws_ref, nq_ref, yg_ref, dab_ref, x1_ref, mod_ref, lnp_ref, x2_ref, win_ref):
    tb = pl.program_id(0)
    kw = NE * COMB_W
    lane = lax.broadcasted_iota(i32, (1, kw), 1)
    lane_e, lane_w = lane // COMB_W, lane % COMB_W
    er = lax.broadcasted_iota(i32, (2 * NE, kw), 0)
    ec = lax.broadcasted_iota(i32, (2 * NE, kw), 1) // COMB_W
    spread = jnp.where(er == ec, 32.0, jnp.where(er - NE == ec, 1.0, 0.0)).astype(bf16)
    destp = _dot(dab_ref[...], spread)

    def body(q, y):
        tgt = jnp.full((1, kw), -1, i32)
        for e in range(NE):
            want = ws_ref[e, tb] + q * COMB_W
            st = pl.multiple_of(jnp.minimum(want, cap - COMB_W), 16)
            win_ref[e * COMB_W:(e + 1) * COMB_W, :] = yg_ref[pl.ds(e * cap + st, COMB_W), :]
            row = st + lane_w
            tgt = jnp.where(lane_e == e, jnp.where(row >= want, row + 1, -1), tgt)
        onehot = jnp.where(destp == tgt.astype(f32), 1.0, 0.0).astype(bf16)
        return y + _dot(onehot, win_ref[...])

    y = lax.fori_loop(0, nq_ref[tb], body, jnp.zeros((COMB_T, D), f32))
    m, lnp = mod_ref[...], lnp_ref[...]
    x2_ref[...] = _ln_rows(ALPHA * x1_ref[...] + m[5:6] * y, lnp[2:3], lnp[3:4])


def _combine(wstart, nq, yg, destab, x1, mod, lnp, cap, seg_of):
    n = x1.shape[0]
    return pl.pallas_call(
        functools.partial(_combine_kernel, cap),
        grid_spec=pltpu.PrefetchScalarGridSpec(
            num_scalar_prefetch=2, grid=(n // COMB_T,),
            in_specs=[_const(yg.shape), _rows(COMB_T, 2 * NE), _rows(COMB_T, D), _mod_spec(seg_of), _const((8, D))],
            out_specs=_rows(COMB_T, D),
            scratch_shapes=[pltpu.VMEM((NE * COMB_W, D), bf16)]),
        out_shape=jax.ShapeDtypeStruct((n, D), f32),
        compiler_params=_cp(52), name="moe_combine",
    )(wstart, nq, yg, destab, x1, mod, lnp)


def _moe(x1, h2, aff3, mod, lnp, wg, wu, wd):
    groups = []
    for lo_tok, n, cap in ((0, NP_TOK, CAP_P), (NP_TOK, NS_TOK, CAP_S)):
        a3 = aff3[lo_tok // LANE:(lo_tok + n) // LANE]
        dest3, gate3, off3 = _route(a3, cap)
        dest_e, gate_e, lo, hi, wstart, nq, destab = _route_tables(dest3, gate3, off3, n, cap)
        xe, gcol = _gather(lo, hi, h2[lo_tok:lo_tok + n], dest_e, gate_e, cap)
        groups.append((xe, gcol, wstart, nq, destab))
    yp, ys = _ffn(groups[0][0], groups[0][1], groups[1][0], groups[1][1], wg, wu, wd)
    x2p = _combine(groups[0][2], groups[0][3], yp, groups[0][4], x1[:NP_TOK], mod, lnp, CAP_P, _seg_p)
    x2s = _combine(groups[1][2], groups[1][3], ys, groups[1][4], x1[NP_TOK:], mod, lnp, CAP_S, _seg_s)
    return jnp.concatenate([x2p, x2s], axis=0)


def _mla_weights(wq_a, q_norm, wq_b, wkv_a, kv_norm, wkv_b):
    wqb = wq_b.reshape(MLA_Q_LORA, MLA_HEADS, MLA_NOPE + MLA_ROPE)
    wqb = jnp.concatenate([wqb[:, :, :MLA_NOPE].reshape(MLA_Q_LORA, -1), wqb[:, :, MLA_NOPE:].reshape(MLA_Q_LORA, -1)], axis=1)
    wkva = jnp.concatenate([wkv_a, wkv_a[:, MLA_KV_LORA:]], axis=1)
    wkvb = wkv_b.reshape(MLA_KV_LORA, MLA_HEADS, MLA_NOPE + MLA_V)
    wkvb = jnp.concatenate([wkvb[:, :, :MLA_NOPE].reshape(MLA_KV_LORA, -1), wkvb[:, :, MLA_NOPE:].reshape(MLA_KV_LORA, -1)], axis=1)
    return (wq_a.astype(bf16), q_norm.reshape(1, -1), wqb.astype(bf16), wkva.astype(bf16), kv_norm.reshape(1, -1),
            wkvb.astype(bf16))


def _dup_heads(w):
    w = w.reshape(w.shape[0], SWA_KV, 1, SWA_HD)
    return jnp.broadcast_to(w, (w.shape[0], SWA_KV, 2, SWA_HD)).reshape(w.shape[0], 2 * SWA_KV * SWA_HD)


def kernel(x_prompt, x_sample, c, cache_mla_ckv, cache_mla_kpe, cache_swa_k, cache_swa_v, c_ctx, w_mod, b_mod, ln_gain, ln_bias, router_w, moe_w_gate, moe_w_up, moe_w_down, mla_wq_a, mla_q_norm, mla_wq_b, mla_wkv_a, mla_kv_norm, mla_wkv_b, mla_wo, gm_w_in, gm_b_in, gm_v_norm_g, gm_v_norm_b, gm_w_s, gm_b_s, gm_w_out, gm_b_out, cv_w_pw1, cv_b_pw1, cv_w_dw, cv_b_dw, cv_norm_g, cv_norm_b, cv_w_pw2, cv_b_pw2, swa_wq, swa_wk, swa_wv, swa_sink, swa_wo):
    x = jnp.concatenate([x_prompt.reshape(NP_TOK, D), x_sample.reshape(NS_TOK, D)], axis=0)
    cond8 = jnp.concatenate([c_ctx[None, :], c, jnp.zeros((5, D), f32)], axis=0)
    mods = _modulation(cond8, w_mod, b_mod)
    lnps = jnp.concatenate([jnp.stack([ln_gain[:, 0], ln_bias[:, 0], ln_gain[:, 1], ln_bias[:, 1]], axis=1),
                            jnp.zeros((DEPTH, 4, D), f32)], axis=1)
    rwts = jnp.transpose(router_w, (0, 2, 1))
    tabs = _rope_tables(DEC_SEQ)

    def moe(i, x1, h2, aff3):
        return _moe(x1, h2, aff3, mods[i], lnps[i], moe_w_gate[i], moe_w_up[i], moe_w_down[i])

    wm = _mla_weights(mla_wq_a[0], mla_q_norm[0], mla_wq_b[0], mla_wkv_a[0], mla_kv_norm[0], mla_wkv_b[0])
    qc_p, kc_p, v_p, ckv_p, kpe_p = _mla_pre(x[:NP_TOK], mods[0], wm, None, _seg_p)
    ql_s, qc_s, kc_s, v_s = _mla_pre(x[NP_TOK:], mods[0], wm, tabs, _seg_s)
    cache_kpe = cache_mla_kpe[:, 0].reshape(DEC_BATCH * PAST, MLA_ROPE)
    kc_c, v_c = _mla_ctx(cache_mla_ckv[:, 0].reshape(DEC_BATCH * PAST, MLA_KV_LORA),
                         jnp.concatenate([cache_kpe, cache_kpe], axis=1), wm[5])
    o = jnp.concatenate([_mla_attn_p(qc_p, kc_p, v_p), _mla_attn_s(ql_s, qc_s, kc_s, v_s, kc_c, v_c)], axis=0)
    x1, h2, aff3 = _proj_close(o, x, mods[0], lnps[0], mla_wo[0].astype(bf16), rwts[0])
    x = moe(0, x1, h2, aff3)

    wgm = (gm_w_in[0].astype(bf16), gm_b_in[0].reshape(1, -1), gm_v_norm_g[0].reshape(1, -1),
           gm_v_norm_b[0].reshape(1, -1), gm_w_s[0].astype(bf16), gm_b_s[0].T, gm_w_out[0].astype(bf16),
           gm_b_out[0].reshape(1, -1))
    x1, h2, aff3 = _gmlp_layer(x, mods[1], lnps[1], wgm, rwts[1])
    x = moe(1, x1, h2, aff3)

    a = _conv_glu(x, mods[2], cv_w_pw1[0].astype(bf16), cv_b_pw1[0].reshape(1, -1))
    wcv = (cv_w_dw[0], cv_b_dw[0].reshape(1, -1), cv_norm_g[0].reshape(1, -1), cv_norm_b[0].reshape(1, -1),
           cv_w_pw2[0].astype(bf16), cv_b_pw2[0].reshape(1, -1))
    cp = _conv_close(a[:NP_TOK], x[:NP_TOK], mods[2], lnps[2], wcv, rwts[2], SEQ, _seg_p)
    cs = _conv_close(a[NP_TOK:], x[NP_TOK:], mods[2], lnps[2], wcv, rwts[2], DEC_SEQ, _seg_s)
    x1, h2, aff3 = (jnp.concatenate([u, v], axis=0) for u, v in zip(cp, cs))
    x = moe(2, x1, h2, aff3)

    wsw = (swa_wq[0].astype(bf16), _dup_heads(swa_wk[0]).astype(bf16), _dup_heads(swa_wv[0]).astype(bf16))
    q_p, kd_p, vd_p, k_p, v_p = _swa_pre(x[:NP_TOK], mods[3], wsw, None, _seg_p)
    qr_s, qw_s, kd_s, vd_s = _swa_pre(x[NP_TOK:], mods[3], wsw, tabs, _seg_s)

    def dup_cache(t):
        t = t[:, 0].reshape(DEC_BATCH * PAST, SWA_KV, 1, SWA_HD)
        return jnp.broadcast_to(t, (DEC_BATCH * PAST, SWA_KV, 2, SWA_HD)).reshape(DEC_BATCH * PAST, -1).astype(bf16)

    sink = swa_sink[0]
    o = jnp.concatenate([_swa_attn_p(sink, q_p, kd_p, vd_p),
                         _swa_attn_s(sink, qr_s, qw_s, kd_s, vd_s, dup_cache(cache_swa_k), dup_cache(cache_swa_v))],
                        axis=0)
    x1, h2, aff3 = _proj_close(o, x, mods[3], lnps[3], swa_wo[0].astype(bf16), rwts[3])
    x = moe(3, x1, h2, aff3)

    return (x[:NP_TOK].reshape(BATCH, SEQ, D), x[NP_TOK:].reshape(DEC_BATCH, DEC_SEQ, D),
            ckv_p.reshape(BATCH, 1, SEQ, MLA_KV_LORA), kpe_p.reshape(BATCH, 1, SEQ, MLA_ROPE),
            k_p.reshape(BATCH, 1, SEQ, SWA_KV, SWA_HD), v_p.reshape(BATCH, 1, SEQ, SWA_KV, SWA_HD))
```

```python
import functools
import math

import jax
import jax.numpy as jnp
from jax import lax
from jax.experimental import pallas as pl
from jax.experimental.pallas import tpu as pltpu

f32 = jnp.float32
bf16 = jnp.bfloat16
i32 = jnp.int32

D = 1024
BATCH, SEQ = 32, 256
DEC_BATCH, DEC_SEQ = 2, 2048
PAST = 256
DEPTH = 4
GRID_W = 64
ALPHA = (2 * DEPTH) ** 0.25
LN_EPS = 1e-5
RMS_EPS = 1e-6
ROPE_BASE = 10000.0
NEG_INF = -1e30
MLA_HEADS, MLA_NOPE, MLA_ROPE, MLA_V = 8, 128, 64, 128
MLA_Q_LORA, MLA_KV_LORA = 384, 256
MLA_SCALE = (MLA_NOPE + MLA_ROPE) ** -0.5
GM_CHUNK, GM_HALF, GM_GROUPS = 128, 2048, 4
CONV_W = 31
SWA_HEADS, SWA_KV, SWA_HD, SWA_WIN = 16, 4, 64, 128
SWA_SCALE = SWA_HD ** -0.5
NE = 16
FF = 1024

NP_TOK = BATCH * SEQ
NS_TOK = DEC_BATCH * DEC_SEQ

LANE = 128
TM = 512
SUB = 256
RT = 256
WIN = 64
ROWS_PER_PASS = 8192
MIB = 2 ** 20


class Group:
    def __init__(self, n_tok, seq, seg0):
        self.n = n_tok
        self.seq = seq
        self.cap = 2 * n_tok // NE
        self.seg0 = seg0

    def seg_of(self, i):
        return self.seg0 + (i * TM) // self.seq if self.seq > TM else self.seg0


CTX = Group(NP_TOK, SEQ, 0)
LAT = Group(NS_TOK, DEC_SEQ, 1)


def _cp(vmem_mb, n_axes=1):
    return pltpu.CompilerParams(dimension_semantics=("arbitrary",) * n_axes,
                                vmem_limit_bytes=int(vmem_mb * MIB))


def _const(shape):
    nd = len(shape)
    return pl.BlockSpec(shape, lambda *_: (0,) * nd, pipeline_mode=pl.Buffered(1))


def _rows(tm, c):
    return pl.BlockSpec((tm, c), lambda i, *_: (i, 0))


def _mod_spec(grp):
    return pl.BlockSpec((None, 8, D), lambda i, *_: (grp.seg_of(i), 0, 0))


def _dot(a, b):
    return jnp.dot(a, b, preferred_element_type=f32)


def _dot_nt(a, b):
    return lax.dot_general(a, b, (((1,), (1,)), ((), ())), preferred_element_type=f32)


def _silu(x):
    return x * jax.nn.sigmoid(x)


def _ln_rows(x, g, b):
    mu = jnp.mean(x, axis=-1, keepdims=True)
    xc = x - mu
    var = jnp.mean(xc * xc, axis=-1, keepdims=True)
    return xc * lax.rsqrt(var + LN_EPS) * g + b


def _rms_rows(x, g):
    return x * lax.rsqrt(jnp.mean(x * x, axis=-1, keepdims=True) + RMS_EPS) * g


def _split_bf16(x):
    hi = x.astype(bf16)
    lo = (x - hi.astype(f32)).astype(bf16)
    return hi, lo


def _subs():
    return [slice(s * SUB, (s + 1) * SUB) for s in range(TM // SUB)]


def _mod_kernel(c_ref, w_ref, b_ref, o_ref):
    x = _silu(c_ref[...])
    xh, xl = _split_bf16(x)
    wh, wl = _split_bf16(w_ref[...])
    o_ref[...] = _dot(xh, wh) + _dot(xh, wl) + _dot(xl, wh) + b_ref[...]


def _modulation(cond8, w_mod, b_mod):
    nk = 6
    out = pl.pallas_call(
        _mod_kernel,
        grid=(DEPTH, nk),
        in_specs=[pl.BlockSpec((8, D), lambda l, k: (0, 0)),
                  pl.BlockSpec((None, D, D), lambda l, k: (l, 0, k)),
                  pl.BlockSpec((None, None, 1, D), lambda l, k: (l, k, 0, 0))],
        out_specs=pl.BlockSpec((None, None, 8, D), lambda l, k: (l, k, 0, 0)),
        out_shape=jax.ShapeDtypeStruct((DEPTH, nk, 8, D), f32),
        compiler_params=_cp(32, 2),
        name="modulation",
    )(cond8, w_mod, b_mod.reshape(DEPTH, nk, 1, D))
    out = jnp.transpose(out, (0, 2, 1, 3))
    return jnp.pad(out, ((0, 0), (0, 0), (0, 2), (0, 0)))


def _router_aff_t(h2, rwt_ref):
    hh, hl = _split_bf16(h2)
    wh, wl = _split_bf16(rwt_ref[...])
    lg = _dot_nt(wh, hh) + _dot_nt(wh, hl) + _dot_nt(wl, hh)
    m = jnp.max(lg, axis=0, keepdims=True)
    e = jnp.exp(lg - m)
    return e / jnp.sum(e, axis=0, keepdims=True)


def _close(d, x, m, lnp, rwt_ref, x1_ref, h2_ref, aff_ref, sl):
    x1 = _ln_rows(ALPHA * x + m[2:3] * d, lnp[0:1], lnp[1:2])
    x1_ref[sl, :] = x1
    h2 = x1 * (1.0 + m[4:5]) + m[3:4]
    h2_ref[sl, :] = h2.astype(bf16)
    aff_ref[sl.start // RT] = _router_aff_t(h2, rwt_ref)


def _close_outs(n):
    shapes = (jax.ShapeDtypeStruct((n, D), f32), jax.ShapeDtypeStruct((n, D), bf16),
              jax.ShapeDtypeStruct((n // RT, NE, RT), f32))
    specs = (_rows(TM, D), _rows(TM, D),
             pl.BlockSpec((TM // RT, NE, RT), lambda i, *_: (i, 0, 0)))
    return shapes, specs


def _proj_close_kernel(o_ref, x_ref, mod_ref, lnp_ref, wo_ref, rwt_ref, x1_ref, h2_ref, aff_ref):
    m, lnp = mod_ref[...], lnp_ref[...]
    for sl in _subs():
        d = _dot(o_ref[sl, :], wo_ref[...])
        _close(d, x_ref[sl, :], m, lnp, rwt_ref, x1_ref, h2_ref, aff_ref, sl)


def _proj_close(grp, o, x, mod, lnp, wo, rwt):
    shapes, specs = _close_outs(grp.n)
    return pl.pallas_call(
        _proj_close_kernel,
        grid=(grp.n // TM,),
        in_specs=[_rows(TM, o.shape[1]), _rows(TM, D), _mod_spec(grp), _const((8, D)),
                  _const(wo.shape), _const((NE, D))],
        out_specs=specs, out_shape=shapes,
        compiler_params=_cp(40), name="proj_close",
    )(o, x, mod, lnp, wo, rwt)


def _rope(x, cos, sin):
    w = x.shape[1]
    reps = w // LANE
    c = jnp.concatenate([cos] * reps, axis=1) if reps > 1 else cos
    s = jnp.concatenate([sin] * reps, axis=1) if reps > 1 else sin
    lane = lax.broadcasted_iota(i32, x.shape, 1)
    up = pltpu.roll(x, w - 16, 1)
    dn = pltpu.roll(x, 16, 1)
    partner = jnp.where((lane % 32) < 16, up, dn)
    return x * c + partner * s


def _rope_tables(length):
    t = jnp.arange(length)
    rows, cols = (t // GRID_W).astype(f32), (t % GRID_W).astype(f32)
    inv = ROPE_BASE ** (-jnp.arange(16, dtype=f32) / 16)
    ar, ac = rows[:, None] * inv[None, :], cols[:, None] * inv[None, :]
    cos = jnp.concatenate([jnp.cos(ar), jnp.cos(ar), jnp.cos(ac), jnp.cos(ac)], axis=1)
    sin = jnp.concatenate([-jnp.sin(ar), jnp.sin(ar), -jnp.sin(ac), jnp.sin(ac)], axis=1)
    return jnp.concatenate([cos, cos], axis=1), jnp.concatenate([sin, sin], axis=1)


def _tab_spec():
    per = DEC_SEQ // TM
    return pl.BlockSpec((TM, LANE), lambda i, *_: (i % per, 0))


def _mla_pre_kernel(rope, x_ref, mod_ref, wqa_ref, qn_ref, wqb_ref, wkva_ref, kvn_ref, wkvb_ref, *rest):
    if rope:
        cos_ref, sin_ref, qlat_ref, qctx_ref, kcat_ref, v_ref = rest
    else:
        qctx_ref, kcat_ref, v_ref, ckv_ref, kpe_ref = rest
    m = mod_ref[...]
    for sl in _subs():
        h = (x_ref[sl, :] * (1.0 + m[1:2]) + m[0:1]).astype(bf16)
        qa = _rms_rows(_dot(h, wqa_ref[...]), qn_ref[...])
        q = _dot(qa.astype(bf16), wqb_ref[...])
        kv = _dot(h, wkva_ref[...])
        ckv = _rms_rows(kv[:, :MLA_KV_LORA], kvn_ref[...])
        kpe2 = kv[:, MLA_KV_LORA:MLA_KV_LORA + LANE]
        kvb = _dot(ckv.astype(bf16), wkvb_ref[...])
        q_pe = q[:, MLA_HEADS * MLA_NOPE:]
        if rope:
            cos, sin = cos_ref[sl, :], sin_ref[sl, :]
            q_pe_rot = _rope(q_pe, cos, sin).astype(bf16)
            kpe2 = _rope(kpe2, cos, sin)
        else:
            ckv_ref[sl, :] = ckv
            kpe_ref[sl, :] = kv[:, MLA_KV_LORA:MLA_KV_LORA + MLA_ROPE]
        q_pe = q_pe.astype(bf16)
        qn = q[:, :MLA_HEADS * MLA_NOPE].astype(bf16)
        kn = kvb[:, :MLA_HEADS * MLA_NOPE].astype(bf16)
        v_ref[sl, :] = kvb[:, MLA_HEADS * MLA_NOPE:].astype(bf16)
        lane = lax.broadcasted_iota(i32, kpe2.shape, 1)
        kpe_lo = jnp.where(lane < MLA_ROPE, kpe2, 0.0).astype(bf16)
        kpe_hi = jnp.where(lane >= MLA_ROPE, kpe2, 0.0).astype(bf16)
        for hh in range(MLA_HEADS):
            a, b = hh * 256, hh * 256 + LANE
            pr = (hh // 2) * LANE
            qctx_ref[sl, a:b] = qn[:, hh * LANE:(hh + 1) * LANE]
            qctx_ref[sl, b:b + LANE] = q_pe[:, pr:pr + LANE]
            if rope:
                qlat_ref[sl, a:b] = qn[:, hh * LANE:(hh + 1) * LANE]
                qlat_ref[sl, b:b + LANE] = q_pe_rot[:, pr:pr + LANE]
            kcat_ref[sl, a:b] = kn[:, hh * LANE:(hh + 1) * LANE]
            kcat_ref[sl, b:b + LANE] = kpe_lo if hh % 2 == 0 else kpe_hi


def _mla_pre(grp, x, mod, w, rope_tabs):
    n = grp.n
    ins = [x, mod] + list(w)
    specs = [_rows(TM, D), _mod_spec(grp)] + [_const(a.shape) for a in w]
    wide = jax.ShapeDtypeStruct((n, 2 * D), bf16)
    if rope_tabs is not None:
        ins += list(rope_tabs)
        specs += [_tab_spec(), _tab_spec()]
        shapes = (wide, wide, wide, jax.ShapeDtypeStruct((n, D), bf16))
        ospecs = (_rows(TM, 2 * D), _rows(TM, 2 * D), _rows(TM, 2 * D), _rows(TM, D))
    else:
        shapes = (wide, wide, jax.ShapeDtypeStruct((n, D), bf16),
                  jax.ShapeDtypeStruct((n, MLA_KV_LORA), f32), jax.ShapeDtypeStruct((n, MLA_ROPE), f32))
        ospecs = (_rows(TM, 2 * D), _rows(TM, 2 * D), _rows(TM, D), _rows(TM, MLA_KV_LORA), _rows(TM, MLA_ROPE))
    return pl.pallas_call(
        functools.partial(_mla_pre_kernel, rope_tabs is not None),
        grid=(n // TM,), in_specs=specs, out_specs=ospecs, out_shape=shapes,
        compiler_params=_cp(48), name="mla_pre",
    )(*ins)


def _mla_ctx_kernel(ckv_ref, kpe2_ref, wkvb_ref, kcat_ref, v_ref):
    kvb = _dot(ckv_ref[...].astype(bf16), wkvb_ref[...])
    kn = kvb[:, :MLA_HEADS * MLA_NOPE].astype(bf16)
    v_ref[...] = kvb[:, MLA_HEADS * MLA_NOPE:].astype(bf16)
    kpe2 = kpe2_ref[...]
    lane = lax.broadcasted_iota(i32, kpe2.shape, 1)
    kpe_lo = jnp.where(lane < MLA_ROPE, kpe2, 0.0).astype(bf16)
    kpe_hi = jnp.where(lane >= MLA_ROPE, kpe2, 0.0).astype(bf16)
    for hh in range(MLA_HEADS):
        a, b = hh * 256, hh * 256 + LANE
        kcat_ref[:, a:b] = kn[:, hh * LANE:(hh + 1) * LANE]
        kcat_ref[:, b:b + LANE] = kpe_lo if hh % 2 == 0 else kpe_hi


def _mla_ctx(ckv, kpe2, wkvb):
    n = ckv.shape[0]
    return pl.pallas_call(
        _mla_ctx_kernel, grid=(n // PAST,),
        in_specs=[_rows(PAST, MLA_KV_LORA), _rows(PAST, LANE), _const(wkvb.shape)],
        out_specs=(_rows(PAST, 2 * D), _rows(PAST, D)),
        out_shape=(jax.ShapeDtypeStruct((n, 2 * D), bf16), jax.ShapeDtypeStruct((n, D), bf16)),
        compiler_params=_cp(24), name="mla_ctx",
    )(ckv, kpe2, wkvb)


def _mla_attn_p_kernel(q_ref, k_ref, v_ref, o_ref):
    for hh in range(MLA_HEADS):
        s = _dot_nt(q_ref[:, hh * 256:(hh + 1) * 256], k_ref[:, hh * 256:(hh + 1) * 256]) * MLA_SCALE
        e = jnp.exp(s - jnp.max(s, axis=-1, keepdims=True))
        l = jnp.sum(e, axis=-1, keepdims=True)
        o = _dot(e.astype(bf16), v_ref[:, hh * LANE:(hh + 1) * LANE]) / l
        o_ref[:, hh * LANE:(hh + 1) * LANE] = o.astype(bf16)


def _mla_attn_p(q, k, v):
    n = q.shape[0]
    return pl.pallas_call(
        _mla_attn_p_kernel, grid=(n // SEQ,),
        in_specs=[_rows(SEQ, 2 * D), _rows(SEQ, 2 * D), _rows(SEQ, D)],
        out_specs=_rows(SEQ, D), out_shape=jax.ShapeDtypeStruct((n, D), bf16),
        compiler_params=_cp(24), name="mla_attn_ctx",
    )(q, k, v)


MLA_QT = 256


def _mla_attn_s_kernel(ql_ref, qc_ref, kl_ref, vl_ref, kc_ref, vc_ref, o_ref):
    for hh in range(MLA_HEADS):
        a, b = hh * 256, (hh + 1) * 256
        s1 = _dot_nt(ql_ref[:, a:b], kl_ref[:, a:b]) * MLA_SCALE
        s2 = _dot_nt(qc_ref[:, a:b], kc_ref[:, a:b]) * MLA_SCALE
        m = jnp.maximum(jnp.max(s1, axis=-1, keepdims=True), jnp.max(s2, axis=-1, keepdims=True))
        e1, e2 = jnp.exp(s1 - m), jnp.exp(s2 - m)
        l = jnp.sum(e1, axis=-1, keepdims=True) + jnp.sum(e2, axis=-1, keepdims=True)
        o = (_dot(e1.astype(bf16), vl_ref[:, hh * LANE:(hh + 1) * LANE])
             + _dot(e2.astype(bf16), vc_ref[:, hh * LANE:(hh + 1) * LANE])) / l
        o_ref[:, hh * LANE:(hh + 1) * LANE] = o.astype(bf16)


def _mla_attn_s(ql, qc, kl, vl, kc, vc):
    nq = DEC_SEQ // MLA_QT
    qs = pl.BlockSpec((MLA_QT, 2 * D), lambda b, i: (b * nq + i, 0))
    return pl.pallas_call(
        _mla_attn_s_kernel, grid=(DEC_BATCH, nq),
        in_specs=[qs, qs,
                  pl.BlockSpec((DEC_SEQ, 2 * D), lambda b, i: (b, 0)),
                  pl.BlockSpec((DEC_SEQ, D), lambda b, i: (b, 0)),
                  pl.BlockSpec((PAST, 2 * D), lambda b, i: (b, 0)),
                  pl.BlockSpec((PAST, D), lambda b, i: (b, 0))],
        out_specs=pl.BlockSpec((MLA_QT, D), lambda b, i: (b * nq + i, 0)),
        out_shape=jax.ShapeDtypeStruct((NS_TOK, D), bf16),
        compiler_params=_cp(48, 2), name="mla_attn_lat",
    )(ql, qc, kl, vl, kc, vc)


def _gelu_tanh(x):
    return 0.5 * x * (1.0 + jnp.tanh(math.sqrt(2.0 / math.pi) * (x + 0.044715 * (x * x * x))))


def _gmlp_kernel(x_ref, mod_ref, lnp_ref, win_ref, bin_ref, vg_ref, vb_ref, ws_ref, bs_ref, wout_ref, bout_ref,
                 rwt_ref, x1_ref, h2_ref, aff_ref, gated_ref):
    m, lnp = mod_ref[...], lnp_ref[...]
    gw = GM_HALF // GM_GROUPS
    for sl in _subs():
        x = x_ref[sl, :]
        h = (x * (1.0 + m[1:2]) + m[0:1]).astype(bf16)
        z = _gelu_tanh(_dot(h, win_ref[...]) + bin_ref[...])
        u = z[:, :GM_HALF]
        v = _ln_rows(z[:, GM_HALF:], vg_ref[...], vb_ref[...]).astype(bf16)
        for c in range(SUB // GM_CHUNK):
            r0, r1 = c * GM_CHUNK, (c + 1) * GM_CHUNK
            for g in range(GM_GROUPS):
                sv = _dot(ws_ref[g], v[r0:r1, g * gw:(g + 1) * gw]) + bs_ref[:, g:g + 1]
                gated_ref[sl.start + r0:sl.start + r1, g * gw:(g + 1) * gw] = (
                    u[r0:r1, g * gw:(g + 1) * gw] * sv).astype(bf16)
        d = _dot(gated_ref[sl, :], wout_ref[...]) + bout_ref[...]
        _close(d, x, m, lnp, rwt_ref, x1_ref, h2_ref, aff_ref, sl)


def _gmlp_layer(grp, x, mod, lnp, w, rwt):
    shapes, specs = _close_outs(grp.n)
    return pl.pallas_call(
        _gmlp_kernel, grid=(grp.n // TM,),
        in_specs=[_rows(TM, D), _mod_spec(grp), _const((8, D))] + [_const(a.shape) for a in w]
        + [_const((NE, D))],
        out_specs=specs, out_shape=shapes,
        scratch_shapes=[pltpu.VMEM((TM, GM_HALF), bf16)],
        compiler_params=_cp(56), name="gmlp_layer",
    )(x, mod, lnp, *w, rwt)


HALO = 16
CONV_RB = 32


def _conv_glu_kernel(x_ref, mod_ref, w_ref, b_ref, a_ref):
    m = mod_ref[...]
    for sl in _subs():
        h = (x_ref[sl, :] * (1.0 + m[1:2]) + m[0:1]).astype(bf16)
        a = _dot(h, w_ref[...]) + b_ref[...]
        a_ref[sl, :] = a[:, :D] * jax.nn.sigmoid(a[:, D:])


def _conv_glu(grp, x, mod, w, b):
    return pl.pallas_call(
        _conv_glu_kernel, grid=(grp.n // TM,),
        in_specs=[_rows(TM, D), _mod_spec(grp), _const(w.shape), _const(b.shape)],
        out_specs=_rows(TM, D), out_shape=jax.ShapeDtypeStruct((grp.n, D), f32),
        compiler_params=_cp(40), name="conv_glu",
    )(x, mod, w, b)


def _conv_close_kernel(seq_subs, ap_ref, a_ref, an_ref, x_ref, mod_ref, lnp_ref, wdw_ref, bdw_ref, ng_ref, nb_ref,
                       w2_ref, b2_ref, rwt_ref, x1_ref, h2_ref, aff_ref, pad_ref, act_ref):
    i = pl.program_id(0)
    nsub = TM // SUB
    bdw, ng, nb = bdw_ref[...], ng_ref[...], nb_ref[...]
    m, lnp = mod_ref[...], lnp_ref[...]
    for s, sl in enumerate(_subs()):
        gsub = i * nsub + s
        prev = a_ref[sl.start - HALO:sl.start, :] if s > 0 else ap_ref[...]
        nxt = a_ref[sl.stop:sl.stop + HALO, :] if s < nsub - 1 else an_ref[...]
        pad = pad_ref.at[s]
        pad[0:HALO, :] = jnp.where((gsub % seq_subs) != 0, prev, 0.0)
        pad[HALO:HALO + SUB, :] = a_ref[sl, :]
        pad[HALO + SUB:, :] = jnp.where((gsub % seq_subs) != seq_subs - 1, nxt, 0.0)
        for r in range(SUB // CONV_RB):
            base = r * CONV_RB + HALO - CONV_W // 2
            acc = jnp.zeros((CONV_RB, D), f32)
            for k in range(CONV_W):
                acc = acc + wdw_ref[k:k + 1, :] * pad[base + k:base + k + CONV_RB, :]
            y = _silu(_ln_rows(acc + bdw, ng, nb))
            act_ref[sl.start + r * CONV_RB:sl.start + (r + 1) * CONV_RB, :] = y.astype(bf16)
        d = _dot(act_ref[sl, :], w2_ref[...]) + b2_ref[...]
        _close(d, x_ref[sl, :], m, lnp, rwt_ref, x1_ref, h2_ref, aff_ref, sl)


def _conv_close(grp, a, x, mod, lnp, w, rwt):
    n = grp.n
    hb = TM // HALO
    last = n // HALO - 1
    shapes, specs = _close_outs(n)
    return pl.pallas_call(
        functools.partial(_conv_close_kernel, grp.seq // SUB), grid=(n // TM,),
        in_specs=[pl.BlockSpec((HALO, D), lambda i: (jnp.maximum(i * hb - 1, 0), 0)),
                  _rows(TM, D),
                  pl.BlockSpec((HALO, D), lambda i: (jnp.minimum((i + 1) * hb, last), 0)),
                  _rows(TM, D), _mod_spec(grp), _const((8, D))]
        + [_const(t.shape) for t in w] + [_const((NE, D))],
        out_specs=specs, out_shape=shapes,
        scratch_shapes=[pltpu.VMEM((TM // SUB, SUB + 2 * HALO, D), f32), pltpu.VMEM((TM, D), bf16)],
        compiler_params=_cp(40), name="conv_close",
    )(a, a, a, x, mod, lnp, *w, rwt)


def _swa_pre_kernel(rope, x_ref, mod_ref, wq_ref, wk_ref, wv_ref, *rest):
    if rope:
        cos_ref, sin_ref, qrot_ref, qraw_ref, kd_ref, vd_ref = rest
    else:
        qraw_ref, kd_ref, vd_ref, k_ref, v_ref = rest
    m = mod_ref[...]
    for sl in _subs():
        h = (x_ref[sl, :] * (1.0 + m[1:2]) + m[0:1]).astype(bf16)
        q = _dot(h, wq_ref[...])
        kd = _dot(h, wk_ref[...])
        vd = _dot(h, wv_ref[...])
        qraw_ref[sl, :] = q.astype(bf16)
        vd_ref[sl, :] = vd.astype(bf16)
        if rope:
            cos, sin = cos_ref[sl, :], sin_ref[sl, :]
            qrot_ref[sl, :] = _rope(q, cos, sin).astype(bf16)
            kd_ref[sl, :] = _rope(kd, cos, sin).astype(bf16)
        else:
            kd_ref[sl, :] = kd.astype(bf16)
            lane = lax.broadcasted_iota(i32, (SUB, LANE), 1)
            for j in range(SWA_KV // 2):
                lo, hi = 2 * j * LANE, (2 * j + 1) * LANE
                k_ref[sl, j * LANE:(j + 1) * LANE] = jnp.where(lane < SWA_HD, kd[:, lo:lo + LANE], kd[:, hi:hi + LANE])
                v_ref[sl, j * LANE:(j + 1) * LANE] = jnp.where(lane < SWA_HD, vd[:, lo:lo + LANE], vd[:, hi:hi + LANE])


def _swa_pre(grp, x, mod, w, rope_tabs):
    n = grp.n
    ins = [x, mod] + list(w)
    specs = [_rows(TM, D), _mod_spec(grp)] + [_const(a.shape) for a in w]
    kw = 2 * SWA_KV * SWA_HD
    qs, ks = jax.ShapeDtypeStruct((n, D), bf16), jax.ShapeDtypeStruct((n, kw), bf16)
    if rope_tabs is not None:
        ins += list(rope_tabs)
        specs += [_tab_spec(), _tab_spec()]
        shapes = (qs, qs, ks, ks)
        ospecs = (_rows(TM, D), _rows(TM, D), _rows(TM, kw), _rows(TM, kw))
    else:
        nat = jax.ShapeDtypeStruct((n, SWA_KV * SWA_HD), f32)
        shapes = (qs, ks, ks, nat, nat)
        ospecs = (_rows(TM, D), _rows(TM, kw), _rows(TM, kw), _rows(TM, SWA_KV * SWA_HD), _rows(TM, SWA_KV * SWA_HD))
    return pl.pallas_call(
        functools.partial(_swa_pre_kernel, rope_tabs is not None),
        grid=(n // TM,), in_specs=specs, out_specs=ospecs, out_shape=shapes,
        compiler_params=_cp(40), name="swa_pre",
    )(*ins)


def _half_mask(x, parity):
    lane = lax.broadcasted_iota(i32, x.shape, 1)
    keep = (lane < SWA_HD) if parity == 0 else (lane >= SWA_HD)
    return jnp.where(keep, x, jnp.zeros_like(x))


def _swa_attn_p_kernel(sink_ref, q_ref, kd_ref, vd_ref, o_ref):
    for pair in range(SWA_HEADS // 2):
        g = (2 * pair) // (SWA_HEADS // SWA_KV)
        qp = q_ref[:, pair * LANE:(pair + 1) * LANE]
        kd = kd_ref[:, g * LANE:(g + 1) * LANE]
        vd = vd_ref[:, g * LANE:(g + 1) * LANE]
        acc = jnp.zeros((SEQ, LANE), f32)
        for par in range(2):
            s = _dot_nt(qp, _half_mask(kd, par)) * SWA_SCALE
            sk = sink_ref[2 * pair + par]
            m = jnp.maximum(jnp.max(s, axis=-1, keepdims=True), sk)
            e = jnp.exp(s - m)
            l = jnp.sum(e, axis=-1, keepdims=True) + jnp.exp(sk - m)
            acc = acc + _dot(e.astype(bf16), _half_mask(vd, par)) / l
        o_ref[:, pair * LANE:(pair + 1) * LANE] = acc.astype(bf16)


def _swa_attn_p(sink, q, kd, vd):
    n = q.shape[0]
    kw = kd.shape[1]
    return pl.pallas_call(
        _swa_attn_p_kernel,
        grid_spec=pltpu.PrefetchScalarGridSpec(
            num_scalar_prefetch=1, grid=(n // SEQ,),
            in_specs=[_rows(SEQ, D), _rows(SEQ, kw), _rows(SEQ, kw)],
            out_specs=_rows(SEQ, D)),
        out_shape=jax.ShapeDtypeStruct((n, D), bf16),
        compiler_params=_cp(24), name="swa_attn_ctx",
    )(sink, q, kd, vd)


SWA_QB = 128


def _swa_attn_s_kernel(sink_ref, qr_ref, qw_ref, kd_ref, vd_ref, kc_ref, vc_ref, o_ref):
    nblk = pl.program_id(1)
    span = 3 * SWA_QB
    start = pl.multiple_of(jnp.clip((nblk - 1) * SWA_QB, 0, DEC_SEQ - span), SWA_QB)
    qpos = nblk * SWA_QB + lax.broadcasted_iota(i32, (SWA_QB, span), 0)
    kpos = start + lax.broadcasted_iota(i32, (SWA_QB, span), 1)
    band = jnp.abs(kpos - qpos) <= SWA_WIN
    for pair in range(SWA_HEADS // 2):
        g = (2 * pair) // (SWA_HEADS // SWA_KV)
        qr = qr_ref[:, pair * LANE:(pair + 1) * LANE]
        qw = qw_ref[:, pair * LANE:(pair + 1) * LANE]
        kd = kd_ref[pl.ds(start, span), g * LANE:(g + 1) * LANE]
        vd = vd_ref[pl.ds(start, span), g * LANE:(g + 1) * LANE]
        kc = kc_ref[:, g * LANE:(g + 1) * LANE]
        vc = vc_ref[:, g * LANE:(g + 1) * LANE]
        acc = jnp.zeros((SWA_QB, LANE), f32)
        for par in range(2):
            s1 = jnp.where(band, _dot_nt(qr, _half_mask(kd, par)) * SWA_SCALE, NEG_INF)
            s2 = _dot_nt(qw, _half_mask(kc, par)) * SWA_SCALE
            sk = sink_ref[2 * pair + par]
            m = jnp.maximum(jnp.maximum(jnp.max(s1, axis=-1, keepdims=True),
                                        jnp.max(s2, axis=-1, keepdims=True)), sk)
            e1, e2 = jnp.exp(s1 - m), jnp.exp(s2 - m)
            l = jnp.sum(e1, axis=-1, keepdims=True) + jnp.sum(e2, axis=-1, keepdims=True) + jnp.exp(sk - m)
            acc = acc + (_dot(e1.astype(bf16), _half_mask(vd, par))
                         + _dot(e2.astype(bf16), _half_mask(vc, par))) / l
        o_ref[:, pair * LANE:(pair + 1) * LANE] = acc.astype(bf16)


def _swa_attn_s(sink, qr, qw, kd, vd, kc, vc):
    nq = DEC_SEQ // SWA_QB
    kw = kd.shape[1]
    qs = pl.BlockSpec((SWA_QB, D), lambda b, i, *_: (b * nq + i, 0))
    full = pl.BlockSpec((DEC_SEQ, kw), lambda b, i, *_: (b, 0))
    ctx = pl.BlockSpec((PAST, kw), lambda b, i, *_: (b, 0))
    return pl.pallas_call(
        _swa_attn_s_kernel,
        grid_spec=pltpu.PrefetchScalarGridSpec(
            num_scalar_prefetch=1, grid=(DEC_BATCH, nq),
            in_specs=[qs, qs, full, full, ctx, ctx],
            out_specs=pl.BlockSpec((SWA_QB, D), lambda b, i, *_: (b * nq + i, 0))),
        out_shape=jax.ShapeDtypeStruct((NS_TOK, D), bf16),
        compiler_params=_cp(32, 2), name="swa_attn_lat",
    )(sink, qr, qw, kd, vd, kc, vc)


def _excl_prefix(mask_f, nb, tri, blk):
    m2 = mask_f.reshape(nb * NE, RT)
    within = _dot(m2.astype(bf16), tri)
    tot = jnp.sum(m2, axis=1, keepdims=True)
    totb = jnp.broadcast_to(tot, (nb * NE, LANE)).astype(bf16)
    offs = _dot(blk, totb)
    return (within + offs[:, 0:1]).reshape(nb, NE, RT), offs.reshape(nb, NE, LANE)


def _route_kernel(cap, nb, aff_ref, dest_ref, gate_ref, off_ref):
    a = aff_ref[...]

    def as_f32(bits):
        return pltpu.bitcast(bits, f32)[None]

    def count_ge(th):
        c = jnp.sum(jnp.where(a >= as_f32(th), 1.0, 0.0), axis=0)
        return jnp.sum(c, axis=1, keepdims=True)

    def body(_, c):
        lo, hi = c
        mid = lo + ((hi - lo + 1) >> 1)
        ok = count_ge(mid) >= cap
        return jnp.where(ok, mid, lo), jnp.where(ok, hi, mid - 1)

    lo0 = jnp.zeros((NE, 1), i32)
    hi0 = jnp.full((NE, 1), 0x7F800000, i32)
    thr, _ = lax.fori_loop(0, 31, body, (lo0, hi0))

    r = lax.broadcasted_iota(i32, (RT, RT), 0)
    c = lax.broadcasted_iota(i32, (RT, RT), 1)
    tri = jnp.where(r < c, 1.0, 0.0).astype(bf16)
    rr = lax.broadcasted_iota(i32, (nb * NE, nb * NE), 0)
    cc = lax.broadcasted_iota(i32, (nb * NE, nb * NE), 1)
    blk = jnp.where(((rr % NE) == (cc % NE)) & (cc < rr), 1.0, 0.0).astype(bf16)

    gt = a > as_f32(thr)
    eq = a == as_f32(thr)
    n_gt = jnp.sum(jnp.sum(jnp.where(gt, 1.0, 0.0), axis=0), axis=1, keepdims=True)
    need = cap - n_gt
    tie_rank, _ = _excl_prefix(jnp.where(eq, 1.0, 0.0), nb, tri, blk)
    sel = gt | (eq & (tie_rank < need[None]))
    pos, offs = _excl_prefix(jnp.where(sel, 1.0, 0.0), nb, tri, blk)
    dest_ref[...] = jnp.where(sel, pos, -1.0).astype(i32)
    gate_ref[...] = jnp.where(sel, a, 0.0)
    off_ref[...] = offs.astype(i32)


def _route(grp, aff3):
    nb = grp.n // RT
    full = pl.BlockSpec((nb, NE, RT), lambda: (0, 0, 0))
    offspec = pl.BlockSpec((nb, NE, LANE), lambda: (0, 0, 0))
    return pl.pallas_call(
        functools.partial(_route_kernel, grp.cap, nb),
        in_specs=[full], out_specs=(full, full, offspec),
        out_shape=(jax.ShapeDtypeStruct((nb, NE, RT), i32), jax.ShapeDtypeStruct((nb, NE, RT), f32),
                   jax.ShapeDtypeStruct((nb, NE, LANE), i32)),
        compiler_params=pltpu.CompilerParams(vmem_limit_bytes=40 * MIB), name="route",
    )(aff3)


def _route_tables(grp, dest3, off3):
    ct = off3[:, :, 0].T
    ct_end = jnp.concatenate([ct[:, 1:], jnp.full((NE, 1), grp.cap, i32)], axis=1)
    wstart = (ct // 16) * 16
    nq = jnp.max((ct_end - wstart + WIN - 1) // WIN, axis=0)
    destp = jnp.transpose(dest3, (0, 2, 1)).reshape(grp.n, NE) + 1
    destab = jnp.concatenate([destp // 32, destp % 32], axis=1).astype(bf16)
    return wstart.astype(i32), nq.astype(i32), destab


def _dispatch_kernel(cap, eg, ws_ref, nq_ref, h2_ref, dest_ref, gate_ref, xe_ref, gcol_ref, hot_ref):
    g, tb = pl.program_id(0), pl.program_id(1)

    @pl.when(tb == 0)
    def _():
        xe_ref[...] = jnp.zeros_like(xe_ref)
        gcol_ref[...] = jnp.zeros_like(gcol_ref)

    wrow = lax.broadcasted_iota(i32, (WIN, RT), 0)

    def body(q, carry):
        starts, gsums = [], []
        for el in range(eg):
            e = g * eg + el
            want = ws_ref[e, tb] + q * WIN
            st = pl.multiple_of(jnp.minimum(want, cap - WIN), 16)
            row = st + wrow
            hit = dest_ref[pl.ds(e, 1), :] == jnp.where(row >= want, row, -7)
            hot_ref[el * WIN:(el + 1) * WIN, :] = jnp.where(hit, 1.0, 0.0).astype(bf16)
            gsums.append(jnp.sum(jnp.where(hit, gate_ref[pl.ds(e, 1), :], 0.0), axis=1, keepdims=True))
            starts.append(st)
        part = _dot(hot_ref[...], h2_ref[...])
        for el in range(eg):
            dst = pl.ds(el * cap + starts[el], WIN)
            xe_ref[dst, :] = xe_ref[dst, :] + part[el * WIN:(el + 1) * WIN, :].astype(bf16)
            gcol_ref[dst, :] = gcol_ref[dst, :] + jnp.broadcast_to(gsums[el], (WIN, LANE))
        return carry

    lax.fori_loop(0, nq_ref[tb], body, 0)


def _dispatch(grp, wstart, nq, h2, dest3, gate3):
    eg = ROWS_PER_PASS // grp.cap
    nb = grp.n // RT
    tab = pl.BlockSpec((None, NE, RT), lambda g, t, *_: (t, 0, 0))
    return pl.pallas_call(
        functools.partial(_dispatch_kernel, grp.cap, eg),
        grid_spec=pltpu.PrefetchScalarGridSpec(
            num_scalar_prefetch=2, grid=(NE // eg, nb),
            in_specs=[pl.BlockSpec((RT, D), lambda g, t, *_: (t, 0)), tab, tab],
            out_specs=(pl.BlockSpec((ROWS_PER_PASS, D), lambda g, t, *_: (g, 0)),
                       pl.BlockSpec((ROWS_PER_PASS, LANE), lambda g, t, *_: (g, 0))),
            scratch_shapes=[pltpu.VMEM((eg * WIN, RT), bf16)]),
        out_shape=(jax.ShapeDtypeStruct((NE * grp.cap, D), bf16), jax.ShapeDtypeStruct((NE * grp.cap, LANE), f32)),
        compiler_params=_cp(56, 2), name="moe_dispatch",
    )(wstart, nq, h2, dest3, gate3)


FFN_RB = 256


def _ffn_kernel(xp_ref, gp_ref, xs_ref, gs_ref, wg_ref, wu_ref, wd_ref, yp_ref, ys_ref, wgb, wub, wdb):
    wgb[...] = wg_ref[...].astype(bf16)
    wub[...] = wu_ref[...].astype(bf16)
    wdb[...] = wd_ref[...].astype(bf16)
    for x_ref, g_ref, y_ref, cap in ((xp_ref, gp_ref, yp_ref, CTX.cap), (xs_ref, gs_ref, ys_ref, LAT.cap)):
        for r in range(cap // FFN_RB):
            sl = slice(r * FFN_RB, (r + 1) * FFN_RB)
            x = x_ref[sl, :]
            hid = (_silu(_dot(x, wgb[...])) * _dot(x, wub[...])).astype(bf16)
            gate = jnp.concatenate([g_ref[sl, :]] * (D // LANE), axis=1)
            y_ref[sl, :] = (_dot(hid, wdb[...]) * gate).astype(bf16)


def _ffn(layer, xp, gp, xs, gs, wg, wu, wd):
    wspec = pl.BlockSpec((None, None, D, FF), lambda e: (layer, e, 0, 0))
    cp, cs = CTX.cap, LAT.cap
    return pl.pallas_call(
        _ffn_kernel, grid=(NE,),
        in_specs=[pl.BlockSpec((cp, D), lambda e: (e, 0)), pl.BlockSpec((cp, LANE), lambda e: (e, 0)),
                  pl.BlockSpec((cs, D), lambda e: (e, 0)), pl.BlockSpec((cs, LANE), lambda e: (e, 0)),
                  wspec, wspec, pl.BlockSpec((None, None, FF, D), lambda e: (layer, e, 0, 0))],
        out_specs=(pl.BlockSpec((cp, D), lambda e: (e, 0)), pl.BlockSpec((cs, D), lambda e: (e, 0))),
        out_shape=(jax.ShapeDtypeStruct((NE * cp, D), bf16), jax.ShapeDtypeStruct((NE * cs, D), bf16)),
        scratch_shapes=[pltpu.VMEM((D, FF), bf16), pltpu.VMEM((D, FF), bf16), pltpu.VMEM((FF, D), bf16)],
        compiler_params=_cp(56), name="moe_ffn",
    )(xp, gp, xs, gs, wg, wu, wd)


def _combine_kernel(cap, ws_ref, nq_ref, yg_ref, dab_ref, x1_ref, mod_ref, lnp_ref, x2_ref, win_ref):
    tb = pl.program_id(0)
    kw = NE * WIN
    lane = lax.broadcasted_iota(i32, (1, kw), 1)
    lane_e, lane_w = lane // WIN, lane % WIN
    er = lax.broadcasted_iota(i32, (2 * NE, kw), 0)
    ec = lax.broadcasted_iota(i32, (2 * NE, kw), 1) // WIN
    spread = jnp.where(er == ec, 32.0, jnp.where(er - NE == ec, 1.0, 0.0)).astype(bf16)
    destp = _dot(dab_ref[...], spread)

    def body(q, y):
        tgt = jnp.full((1, kw), -1, i32)
        for e in range(NE):
            want = ws_ref[e, tb] + q * WIN
            st = pl.multiple_of(jnp.minimum(want, cap - WIN), 16)
            win_ref[e * WIN:(e + 1) * WIN, :] = yg_ref[pl.ds(e * cap + st, WIN), :]
            row = st + lane_w
            tgt = jnp.where(lane_e == e, jnp.where(row >= want, row + 1, -1), tgt)
        onehot = jnp.where(destp == tgt.astype(f32), 1.0, 0.0).astype(bf16)
        return y + _dot(onehot, win_ref[...])

    y = lax.fori_loop(0, nq_ref[tb], body, jnp.zeros((RT, D), f32))
    m, lnp = mod_ref[...], lnp_ref[...]
    x2_ref[...] = _ln_rows(ALPHA * x1_ref[...] + m[5:6] * y, lnp[2:3], lnp[3:4])


def _combine(grp, wstart, nq, yg, destab, x1, mod, lnp):
    seg = pl.BlockSpec((None, 8, D), lambda i, *_: (grp.seg0 + (i * RT) // grp.seq if grp.seq > RT else grp.seg0, 0, 0))
    return pl.pallas_call(
        functools.partial(_combine_kernel, grp.cap),
        grid_spec=pltpu.PrefetchScalarGridSpec(
            num_scalar_prefetch=2, grid=(grp.n // RT,),
            in_specs=[_const(yg.shape), _rows(RT, 2 * NE), _rows(RT, D), seg, _const((8, D))],
            out_specs=_rows(RT, D),
            scratch_shapes=[pltpu.VMEM((NE * WIN, D), bf16)]),
        out_shape=jax.ShapeDtypeStruct((grp.n, D), f32),
        compiler_params=_cp(52), name="moe_combine",
    )(wstart, nq, yg, destab, x1, mod, lnp)


def _moe(layer, closed, mod, lnp, wg, wu, wd):
    disp = []
    for grp, (x1, h2, aff3) in zip((CTX, LAT), closed):
        dest3, gate3, off3 = _route(grp, aff3)
        wstart, nq, destab = _route_tables(grp, dest3, off3)
        xe, gcol = _dispatch(grp, wstart, nq, h2, dest3, gate3)
        disp.append((xe, gcol, wstart, nq, destab, x1))
    ys = _ffn(layer, disp[0][0], disp[0][1], disp[1][0], disp[1][1], wg, wu, wd)
    return tuple(_combine(grp, d[2], d[3], y, d[4], d[5], mod, lnp)
                 for grp, d, y in zip((CTX, LAT), disp, ys))


def _mla_weights(wq_a, q_norm, wq_b, wkv_a, kv_norm, wkv_b):
    wqb = wq_b.reshape(MLA_Q_LORA, MLA_HEADS, MLA_NOPE + MLA_ROPE)
    wqb = jnp.concatenate([wqb[:, :, :MLA_NOPE].reshape(MLA_Q_LORA, -1), wqb[:, :, MLA_NOPE:].reshape(MLA_Q_LORA, -1)], axis=1)
    wkva = jnp.concatenate([wkv_a, wkv_a[:, MLA_KV_LORA:]], axis=1)
    wkvb = wkv_b.reshape(MLA_KV_LORA, MLA_HEADS, MLA_NOPE + MLA_V)
    wkvb = jnp.concatenate([wkvb[:, :, :MLA_NOPE].reshape(MLA_KV_LORA, -1), wkvb[:, :, MLA_NOPE:].reshape(MLA_KV_LORA, -1)], axis=1)
    return (wq_a.astype(bf16), q_norm.reshape(1, -1), wqb.astype(bf16), wkva.astype(bf16), kv_norm.reshape(1, -1),
            wkvb.astype(bf16))


def _dup_heads(w):
    w = w.reshape(w.shape[0], SWA_KV, 1, SWA_HD)
    return jnp.broadcast_to(w, (w.shape[0], SWA_KV, 2, SWA_HD)).reshape(w.shape[0], 2 * SWA_KV * SWA_HD)


def kernel(x_prompt, x_sample, c, cache_mla_ckv, cache_mla_kpe, cache_swa_k, cache_swa_v, c_ctx, w_mod, b_mod, ln_gain, ln_bias, router_w, moe_w_gate, moe_w_up, moe_w_down, mla_wq_a, mla_q_norm, mla_wq_b, mla_wkv_a, mla_kv_norm, mla_wkv_b, mla_wo, gm_w_in, gm_b_in, gm_v_norm_g, gm_v_norm_b, gm_w_s, gm_b_s, gm_w_out, gm_b_out, cv_w_pw1, cv_b_pw1, cv_w_dw, cv_b_dw, cv_norm_g, cv_norm_b, cv_w_pw2, cv_b_pw2, swa_wq, swa_wk, swa_wv, swa_sink, swa_wo):
    groups = (CTX, LAT)
    xs = (x_prompt.reshape(NP_TOK, D), x_sample.reshape(NS_TOK, D))
    cond8 = jnp.concatenate([c_ctx[None, :], c, jnp.zeros((5, D), f32)], axis=0)
    mods = _modulation(cond8, w_mod, b_mod)
    lnps = jnp.concatenate([jnp.stack([ln_gain[:, 0], ln_bias[:, 0], ln_gain[:, 1], ln_bias[:, 1]], axis=1),
                            jnp.zeros((DEPTH, 4, D), f32)], axis=1)
    rwts = jnp.transpose(router_w, (0, 2, 1))
    tabs = _rope_tables(DEC_SEQ)

    def moe(i, closed):
        return _moe(i, closed, mods[i], lnps[i], moe_w_gate, moe_w_up, moe_w_down)

    wm = _mla_weights(mla_wq_a[0], mla_q_norm[0], mla_wq_b[0], mla_wkv_a[0], mla_kv_norm[0], mla_wkv_b[0])
    qc_p, kc_p, v_p, ckv_p, kpe_p = _mla_pre(CTX, xs[0], mods[0], wm, None)
    ql_s, qc_s, kc_s, v_s = _mla_pre(LAT, xs[1], mods[0], wm, tabs)
    cache_kpe = cache_mla_kpe[:, 0].reshape(DEC_BATCH * PAST, MLA_ROPE)
    kc_c, v_c = _mla_ctx(cache_mla_ckv[:, 0].reshape(DEC_BATCH * PAST, MLA_KV_LORA),
                         jnp.concatenate([cache_kpe, cache_kpe], axis=1), wm[5])
    os_ = (_mla_attn_p(qc_p, kc_p, v_p), _mla_attn_s(ql_s, qc_s, kc_s, v_s, kc_c, v_c))
    wo = mla_wo[0].astype(bf16)
    xs = moe(0, [_proj_close(g, o, x, mods[0], lnps[0], wo, rwts[0]) for g, o, x in zip(groups, os_, xs)])

    wgm = (gm_w_in[0].astype(bf16), gm_b_in[0].reshape(1, -1), gm_v_norm_g[0].reshape(1, -1),
           gm_v_norm_b[0].reshape(1, -1), gm_w_s[0].astype(bf16), gm_b_s[0].T, gm_w_out[0].astype(bf16),
           gm_b_out[0].reshape(1, -1))
    xs = moe(1, [_gmlp_layer(g, x, mods[1], lnps[1], wgm, rwts[1]) for g, x in zip(groups, xs)])

    w1, b1 = cv_w_pw1[0].astype(bf16), cv_b_pw1[0].reshape(1, -1)
    wcv = (cv_w_dw[0], cv_b_dw[0].reshape(1, -1), cv_norm_g[0].reshape(1, -1), cv_norm_b[0].reshape(1, -1),
           cv_w_pw2[0].astype(bf16), cv_b_pw2[0].reshape(1, -1))
    xs = moe(2, [_conv_close(g, _conv_glu(g, x, mods[2], w1, b1), x, mods[2], lnps[2], wcv, rwts[2])
                 for g, x in zip(groups, xs)])

    wsw = (swa_wq[0].astype(bf16), _dup_heads(swa_wk[0]).astype(bf16), _dup_heads(swa_wv[0]).astype(bf16))
    q_p, kd_p, vd_p, k_p, v_p = _swa_pre(CTX, xs[0], mods[3], wsw, None)
    qr_s, qw_s, kd_s, vd_s = _swa_pre(LAT, xs[1], mods[3], wsw, tabs)

    def dup_cache(t):
        return _dup_heads(t[:, 0].reshape(DEC_BATCH * PAST, SWA_KV * SWA_HD)).astype(bf16)

    sink = swa_sink[0]
    os_ = (_swa_attn_p(sink, q_p, kd_p, vd_p),
           _swa_attn_s(sink, qr_s, qw_s, kd_s, vd_s, dup_cache(cache_swa_k), dup_cache(cache_swa_v)))
    wo = swa_wo[0].astype(bf16)
    xs = moe(3, [_proj_close(g, o, x, mods[3], lnps[3], wo, rwts[3]) for g, o, x in zip(groups, os_, xs)])

    return (xs[0].reshape(BATCH, SEQ, D), xs[1].reshape(DEC_BATCH, DEC_SEQ, D),
            ckv_p.reshape(BATCH, 1, SEQ, MLA_KV_LORA), kpe_p.reshape(BATCH, 1, SEQ, MLA_ROPE),
            k_p.reshape(BATCH, 1, SEQ, SWA_KV, SWA_HD), v_p.reshape(BATCH, 1, SEQ, SWA_KV, SWA_HD))
```

```python
import functools
import math

import jax
import jax.numpy as jnp
from jax import lax
from jax.experimental import pallas as pl
from jax.experimental.pallas import tpu as pltpu

f32 = jnp.float32
bf16 = jnp.bfloat16
i32 = jnp.int32

D = 1024
BATCH, SEQ = 32, 256
DEC_BATCH, DEC_SEQ = 2, 2048
PAST = 256
DEPTH = 4
GRID_W = 64
ALPHA = (2 * DEPTH) ** 0.25
LN_EPS = 1e-5
RMS_EPS = 1e-6
ROPE_BASE = 10000.0
NEG_INF = -1e30
MLA_HEADS, MLA_NOPE, MLA_ROPE, MLA_V = 8, 128, 64, 128
MLA_Q_LORA, MLA_KV_LORA = 384, 256
MLA_SCALE = (MLA_NOPE + MLA_ROPE) ** -0.5
MLA_C2 = MLA_SCALE * math.log2(math.e)
GM_CHUNK, GM_HALF, GM_GROUPS = 128, 2048, 4
CONV_W = 31
SWA_HEADS, SWA_KV, SWA_HD, SWA_WIN = 16, 4, 64, 128
SWA_SCALE = SWA_HD ** -0.5
SWA_C2 = SWA_SCALE * math.log2(math.e)
NE = 16
FF = 1024

NP_TOK = BATCH * SEQ
NS_TOK = DEC_BATCH * DEC_SEQ

LANE = 128
TM = 512
SUB = 256
RT = 256
WIN = 64
ROWS_PER_PASS = 8192
MIB = 2 ** 20


class Group:
    def __init__(self, n_tok, seq, seg0):
        self.n = n_tok
        self.seq = seq
        self.cap = 2 * n_tok // NE
        self.seg0 = seg0

    def seg_of(self, i):
        return self.seg0 + (i * TM) // self.seq if self.seq > TM else self.seg0


CTX = Group(NP_TOK, SEQ, 0)
LAT = Group(NS_TOK, DEC_SEQ, 1)


def _cp(vmem_mb, n_axes=1):
    return pltpu.CompilerParams(dimension_semantics=("arbitrary",) * n_axes,
                                vmem_limit_bytes=int(vmem_mb * MIB))


def _const(shape):
    nd = len(shape)
    return pl.BlockSpec(shape, lambda *_: (0,) * nd, pipeline_mode=pl.Buffered(1))


def _rows(tm, c):
    return pl.BlockSpec((tm, c), lambda i, *_: (i, 0))


def _mod_spec(grp):
    return pl.BlockSpec((None, 8, D), lambda i, *_: (grp.seg_of(i), 0, 0))


def _dot(a, b):
    return jnp.dot(a, b, preferred_element_type=f32)


def _dot_nt(a, b):
    return lax.dot_general(a, b, (((1,), (1,)), ((), ())), preferred_element_type=f32)


def _silu(x):
    return x * jax.nn.sigmoid(x)


def _ln_rows(x, g, b):
    mu = jnp.mean(x, axis=-1, keepdims=True)
    xc = x - mu
    var = jnp.mean(xc * xc, axis=-1, keepdims=True)
    return xc * lax.rsqrt(var + LN_EPS) * g + b


def _rms_rows(x, g):
    return x * lax.rsqrt(jnp.mean(x * x, axis=-1, keepdims=True) + RMS_EPS) * g


def _split_bf16(x):
    hi = x.astype(bf16)
    lo = (x - hi.astype(f32)).astype(bf16)
    return hi, lo


def _subs():
    return [slice(s * SUB, (s + 1) * SUB) for s in range(TM // SUB)]


def _mod_kernel(c_ref, w_ref, b_ref, o_ref):
    x = _silu(c_ref[...])
    xh, xl = _split_bf16(x)
    wh, wl = _split_bf16(w_ref[...])
    o_ref[...] = _dot(xh, wh) + _dot(xh, wl) + _dot(xl, wh) + b_ref[...]


def _modulation(cond8, w_mod, b_mod):
    nk = 6
    out = pl.pallas_call(
        _mod_kernel,
        grid=(DEPTH, nk),
        in_specs=[pl.BlockSpec((8, D), lambda l, k: (0, 0)),
                  pl.BlockSpec((None, D, D), lambda l, k: (l, 0, k)),
                  pl.BlockSpec((None, None, 1, D), lambda l, k: (l, k, 0, 0))],
        out_specs=pl.BlockSpec((None, None, 8, D), lambda l, k: (l, k, 0, 0)),
        out_shape=jax.ShapeDtypeStruct((DEPTH, nk, 8, D), f32),
        compiler_params=_cp(32, 2),
        name="modulation",
    )(cond8, w_mod, b_mod.reshape(DEPTH, nk, 1, D))
    out = jnp.transpose(out, (0, 2, 1, 3))
    return jnp.pad(out, ((0, 0), (0, 0), (0, 2), (0, 0)))


def _router_aff_t(h2, rw_ref):
    hh, hl = _split_bf16(h2)
    wh, wl = _split_bf16(rw_ref[...])
    lg = _dot(hh, wh) + _dot(hl, wh) + _dot(hh, wl)
    lg = jnp.transpose(lg)[:NE, :]
    m = jnp.max(lg, axis=0, keepdims=True)
    e = jnp.exp(lg - m)
    return e / jnp.sum(e, axis=0, keepdims=True)


def _close(d, x, m, lnp, rwt_ref, x1_ref, h2_ref, aff_ref, sl):
    x1 = _ln_rows(ALPHA * x + m[2:3] * d, lnp[0:1], lnp[1:2])
    x1_ref[sl, :] = x1
    h2 = x1 * (1.0 + m[4:5]) + m[3:4]
    h2_ref[sl, :] = h2.astype(bf16)
    aff_ref[sl.start // RT] = _router_aff_t(h2, rwt_ref)


def _close_outs(n):
    shapes = (jax.ShapeDtypeStruct((n, D), f32), jax.ShapeDtypeStruct((n, D), bf16),
              jax.ShapeDtypeStruct((n // RT, NE, RT), f32))
    specs = (_rows(TM, D), _rows(TM, D),
             pl.BlockSpec((TM // RT, NE, RT), lambda i, *_: (i, 0, 0)))
    return shapes, specs


def _proj_close_kernel(o_ref, x_ref, mod_ref, lnp_ref, wo_ref, rwt_ref, x1_ref, h2_ref, aff_ref):
    m, lnp = mod_ref[...], lnp_ref[...]
    subs = _subs()
    nxt = _dot(o_ref[subs[0], :], wo_ref[...])
    for s, sl in enumerate(subs):
        d = nxt
        if s + 1 < len(subs):
            nxt = _dot(o_ref[subs[s + 1], :], wo_ref[...])
        _close(d, x_ref[sl, :], m, lnp, rwt_ref, x1_ref, h2_ref, aff_ref, sl)


def _proj_close(grp, o, x, mod, lnp, wo, rwt):
    shapes, specs = _close_outs(grp.n)
    return pl.pallas_call(
        _proj_close_kernel,
        grid=(grp.n // TM,),
        in_specs=[_rows(TM, o.shape[1]), _rows(TM, D), _mod_spec(grp), _const((8, D)),
                  _const(wo.shape), _const((D, LANE))],
        out_specs=specs, out_shape=shapes,
        compiler_params=_cp(40), name="proj_close",
    )(o, x, mod, lnp, wo, rwt)


def _rope(x, cos, sin):
    w = x.shape[1]
    reps = w // LANE
    c = jnp.concatenate([cos] * reps, axis=1) if reps > 1 else cos
    s = jnp.concatenate([sin] * reps, axis=1) if reps > 1 else sin
    lane = lax.broadcasted_iota(i32, x.shape, 1)
    up = pltpu.roll(x, w - 16, 1)
    dn = pltpu.roll(x, 16, 1)
    partner = jnp.where((lane % 32) < 16, up, dn)
    return x * c + partner * s


def _rope_tables(length):
    t = jnp.arange(length)
    rows, cols = (t // GRID_W).astype(f32), (t % GRID_W).astype(f32)
    inv = ROPE_BASE ** (-jnp.arange(16, dtype=f32) / 16)
    ar, ac = rows[:, None] * inv[None, :], cols[:, None] * inv[None, :]
    cos = jnp.concatenate([jnp.cos(ar), jnp.cos(ar), jnp.cos(ac), jnp.cos(ac)], axis=1)
    sin = jnp.concatenate([-jnp.sin(ar), jnp.sin(ar), -jnp.sin(ac), jnp.sin(ac)], axis=1)
    return jnp.concatenate([cos, cos], axis=1), jnp.concatenate([sin, sin], axis=1)


def _tab_spec():
    per = DEC_SEQ // TM
    return pl.BlockSpec((TM, LANE), lambda i, *_: (i % per, 0))


def _mla_pre_kernel(rope, x_ref, mod_ref, wqa_ref, qn_ref, wqb_ref, wkva_ref, kvn_ref, wkvb_ref, *rest):
    if rope:
        cos_ref, sin_ref, qlat_ref, qctx_ref, kcat_ref, v_ref = rest
    else:
        qctx_ref, kcat_ref, v_ref, ckv_ref, kpe_ref = rest
    m = mod_ref[...]

    def front(sl):
        h = (x_ref[sl, :] * (1.0 + m[1:2]) + m[0:1]).astype(bf16)
        return _dot(h, wqa_ref[...]), _dot(h, wkva_ref[...])

    subs = _subs()
    nxt = front(subs[0])
    for s, sl in enumerate(subs):
        qa, kv = nxt
        if s + 1 < len(subs):
            nxt = front(subs[s + 1])
        qa = _rms_rows(qa, qn_ref[...])
        q = _dot(qa.astype(bf16), wqb_ref[...])
        ckv = _rms_rows(kv[:, :MLA_KV_LORA], kvn_ref[...])
        kpe2 = kv[:, MLA_KV_LORA:MLA_KV_LORA + LANE]
        kvb = _dot(ckv.astype(bf16), wkvb_ref[...])
        q_pe = q[:, MLA_HEADS * MLA_NOPE:]
        if rope:
            cos, sin = cos_ref[sl, :], sin_ref[sl, :]
            q_pe_rot = _rope(q_pe, cos, sin).astype(bf16)
            kpe2 = _rope(kpe2, cos, sin)
        else:
            ckv_ref[sl, :] = ckv
            kpe_ref[sl, :] = kv[:, MLA_KV_LORA:MLA_KV_LORA + MLA_ROPE]
        q_pe = q_pe.astype(bf16)
        qn = q[:, :MLA_HEADS * MLA_NOPE].astype(bf16)
        kn = kvb[:, :MLA_HEADS * MLA_NOPE].astype(bf16)
        v_ref[sl, :] = kvb[:, MLA_HEADS * MLA_NOPE:].astype(bf16)
        lane = lax.broadcasted_iota(i32, kpe2.shape, 1)
        kpe_lo = jnp.where(lane < MLA_ROPE, kpe2, 0.0).astype(bf16)
        kpe_hi = jnp.where(lane >= MLA_ROPE, kpe2, 0.0).astype(bf16)
        for hh in range(MLA_HEADS):
            a, b = hh * 256, hh * 256 + LANE
            pr = (hh // 2) * LANE
            qctx_ref[sl, a:b] = qn[:, hh * LANE:(hh + 1) * LANE]
            qctx_ref[sl, b:b + LANE] = q_pe[:, pr:pr + LANE]
            if rope:
                qlat_ref[sl, a:b] = qn[:, hh * LANE:(hh + 1) * LANE]
                qlat_ref[sl, b:b + LANE] = q_pe_rot[:, pr:pr + LANE]
            kcat_ref[sl, a:b] = kn[:, hh * LANE:(hh + 1) * LANE]
            kcat_ref[sl, b:b + LANE] = kpe_lo if hh % 2 == 0 else kpe_hi


def _mla_pre(grp, x, mod, w, rope_tabs):
    n = grp.n
    ins = [x, mod] + list(w)
    specs = [_rows(TM, D), _mod_spec(grp)] + [_const(a.shape) for a in w]
    wide = jax.ShapeDtypeStruct((n, 2 * D), bf16)
    if rope_tabs is not None:
        ins += list(rope_tabs)
        specs += [_tab_spec(), _tab_spec()]
        shapes = (wide, wide, wide, jax.ShapeDtypeStruct((n, D), bf16))
        ospecs = (_rows(TM, 2 * D), _rows(TM, 2 * D), _rows(TM, 2 * D), _rows(TM, D))
    else:
        shapes = (wide, wide, jax.ShapeDtypeStruct((n, D), bf16),
                  jax.ShapeDtypeStruct((n, MLA_KV_LORA), f32), jax.ShapeDtypeStruct((n, MLA_ROPE), f32))
        ospecs = (_rows(TM, 2 * D), _rows(TM, 2 * D), _rows(TM, D), _rows(TM, MLA_KV_LORA), _rows(TM, MLA_ROPE))
    return pl.pallas_call(
        functools.partial(_mla_pre_kernel, rope_tabs is not None),
        grid=(n // TM,), in_specs=specs, out_specs=ospecs, out_shape=shapes,
        compiler_params=_cp(48), name="mla_pre",
    )(*ins)


def _mla_ctx_kernel(ckv_ref, kpe2_ref, wkvb_ref, kcat_ref, v_ref):
    kvb = _dot(ckv_ref[...].astype(bf16), wkvb_ref[...])
    kn = kvb[:, :MLA_HEADS * MLA_NOPE].astype(bf16)
    v_ref[...] = kvb[:, MLA_HEADS * MLA_NOPE:].astype(bf16)
    kpe2 = kpe2_ref[...]
    lane = lax.broadcasted_iota(i32, kpe2.shape, 1)
    kpe_lo = jnp.where(lane < MLA_ROPE, kpe2, 0.0).astype(bf16)
    kpe_hi = jnp.where(lane >= MLA_ROPE, kpe2, 0.0).astype(bf16)
    for hh in range(MLA_HEADS):
        a, b = hh * 256, hh * 256 + LANE
        kcat_ref[:, a:b] = kn[:, hh * LANE:(hh + 1) * LANE]
        kcat_ref[:, b:b + LANE] = kpe_lo if hh % 2 == 0 else kpe_hi


def _mla_ctx(ckv, kpe2, wkvb):
    n = ckv.shape[0]
    return pl.pallas_call(
        _mla_ctx_kernel, grid=(n // PAST,),
        in_specs=[_rows(PAST, MLA_KV_LORA), _rows(PAST, LANE), _const(wkvb.shape)],
        out_specs=(_rows(PAST, 2 * D), _rows(PAST, D)),
        out_shape=(jax.ShapeDtypeStruct((n, 2 * D), bf16), jax.ShapeDtypeStruct((n, D), bf16)),
        compiler_params=_cp(24), name="mla_ctx",
    )(ckv, kpe2, wkvb)


def _mla_attn_p_kernel(q_ref, k_ref, v_ref, o_ref):
    def scores(hh):
        return _dot_nt(q_ref[:, hh * 256:(hh + 1) * 256], k_ref[:, hh * 256:(hh + 1) * 256])

    nxt = scores(0)
    for hh in range(MLA_HEADS):
        s = nxt
        if hh + 1 < MLA_HEADS:
            nxt = scores(hh + 1)
        e = jnp.exp2((s - jnp.max(s, axis=-1, keepdims=True)) * MLA_C2)
        l = jnp.sum(e, axis=-1, keepdims=True)
        o = _dot(e.astype(bf16), v_ref[:, hh * LANE:(hh + 1) * LANE]) / l
        o_ref[:, hh * LANE:(hh + 1) * LANE] = o.astype(bf16)


def _mla_attn_p(q, k, v):
    n = q.shape[0]
    return pl.pallas_call(
        _mla_attn_p_kernel, grid=(n // SEQ,),
        in_specs=[_rows(SEQ, 2 * D), _rows(SEQ, 2 * D), _rows(SEQ, D)],
        out_specs=_rows(SEQ, D), out_shape=jax.ShapeDtypeStruct((n, D), bf16),
        compiler_params=_cp(24), name="mla_attn_ctx",
    )(q, k, v)


MLA_QT = 256


def _mla_attn_s_kernel(ql_ref, qc_ref, kl_ref, vl_ref, kc_ref, vc_ref, o_ref):
    def scores(hh):
        a, b = hh * 256, (hh + 1) * 256
        return (_dot_nt(ql_ref[:, a:b], kl_ref[:, a:b]), _dot_nt(qc_ref[:, a:b], kc_ref[:, a:b]))

    nxt = scores(0)
    for hh in range(MLA_HEADS):
        s1, s2 = nxt
        if hh + 1 < MLA_HEADS:
            nxt = scores(hh + 1)
        m = jnp.maximum(jnp.max(s1, axis=-1, keepdims=True), jnp.max(s2, axis=-1, keepdims=True))
        e1, e2 = jnp.exp2((s1 - m) * MLA_C2), jnp.exp2((s2 - m) * MLA_C2)
        l = jnp.sum(e1, axis=-1, keepdims=True) + jnp.sum(e2, axis=-1, keepdims=True)
        o = (_dot(e1.astype(bf16), vl_ref[:, hh * LANE:(hh + 1) * LANE])
             + _dot(e2.astype(bf16), vc_ref[:, hh * LANE:(hh + 1) * LANE])) / l
        o_ref[:, hh * LANE:(hh + 1) * LANE] = o.astype(bf16)


def _mla_attn_s(ql, qc, kl, vl, kc, vc):
    nq = DEC_SEQ // MLA_QT
    qs = pl.BlockSpec((MLA_QT, 2 * D), lambda b, i: (b * nq + i, 0))
    return pl.pallas_call(
        _mla_attn_s_kernel, grid=(DEC_BATCH, nq),
        in_specs=[qs, qs,
                  pl.BlockSpec((DEC_SEQ, 2 * D), lambda b, i: (b, 0)),
                  pl.BlockSpec((DEC_SEQ, D), lambda b, i: (b, 0)),
                  pl.BlockSpec((PAST, 2 * D), lambda b, i: (b, 0)),
                  pl.BlockSpec((PAST, D), lambda b, i: (b, 0))],
        out_specs=pl.BlockSpec((MLA_QT, D), lambda b, i: (b * nq + i, 0)),
        out_shape=jax.ShapeDtypeStruct((NS_TOK, D), bf16),
        compiler_params=_cp(48, 2), name="mla_attn_lat",
    )(ql, qc, kl, vl, kc, vc)


def _gelu_tanh(x):
    return 0.5 * x * (1.0 + jnp.tanh(math.sqrt(2.0 / math.pi) * (x + 0.044715 * (x * x * x))))


GM_CW = GM_HALF // GM_GROUPS


def _gmlp_kernel(x_ref, mod_ref, lnp_ref, win_ref, bin_ref, vg_ref, vb_ref, ws_ref, bs_ref, wout_ref, bout_ref,
                 rwt_ref, x1_ref, h2_ref, aff_ref, gated_ref, vz_ref):
    m, lnp = mod_ref[...], lnp_ref[...]
    for s, sl in enumerate(_subs()):
        x = x_ref[sl, :]
        h = (x * (1.0 + m[1:2]) + m[0:1]).astype(bf16)

        def mm(j):
            return _dot(h, win_ref[:, j * GM_CW:(j + 1) * GM_CW]) + bin_ref[:, j * GM_CW:(j + 1) * GM_CW]

        s1 = jnp.zeros((SUB, 1), f32)
        s2 = jnp.zeros((SUB, 1), f32)
        nxt = mm(GM_GROUPS)
        for g in range(GM_GROUPS):
            cur = nxt
            nxt = mm(GM_GROUPS + g + 1) if g + 1 < GM_GROUPS else mm(0)
            z = _gelu_tanh(cur)
            s1 = s1 + jnp.sum(z, axis=-1, keepdims=True)
            s2 = s2 + jnp.sum(z * z, axis=-1, keepdims=True)
            vz_ref[s, :, g * GM_CW:(g + 1) * GM_CW] = z
        mu = s1 * (1.0 / GM_HALF)
        rstd = lax.rsqrt(s2 * (1.0 / GM_HALF) - mu * mu + LN_EPS)
        for g in range(GM_GROUPS):
            cur = nxt
            if g + 1 < GM_GROUPS:
                nxt = mm(g + 1)
            cols = slice(g * GM_CW, (g + 1) * GM_CW)
            u = _gelu_tanh(cur)
            v = ((vz_ref[s, :, cols] - mu) * rstd * vg_ref[:, cols] + vb_ref[:, cols]).astype(bf16)
            for c in range(SUB // GM_CHUNK):
                r0, r1 = c * GM_CHUNK, (c + 1) * GM_CHUNK
                sv = _dot(ws_ref[g], v[r0:r1, :]) + bs_ref[:, g:g + 1]
                gated_ref[sl.start + r0:sl.start + r1, cols] = (u[r0:r1, :] * sv).astype(bf16)
        d = _dot(gated_ref[sl, :], wout_ref[...]) + bout_ref[...]
        _close(d, x, m, lnp, rwt_ref, x1_ref, h2_ref, aff_ref, sl)


def _gmlp_layer(grp, x, mod, lnp, w, rwt):
    shapes, specs = _close_outs(grp.n)
    return pl.pallas_call(
        _gmlp_kernel, grid=(grp.n // TM,),
        in_specs=[_rows(TM, D), _mod_spec(grp), _const((8, D))] + [_const(a.shape) for a in w]
        + [_const((D, LANE))],
        out_specs=specs, out_shape=shapes,
        scratch_shapes=[pltpu.VMEM((TM, GM_HALF), bf16), pltpu.VMEM((TM // SUB, SUB, GM_HALF), f32)],
        compiler_params=_cp(56), name="gmlp_layer",
    )(x, mod, lnp, *w, rwt)


HALO = 16
CONV_RB = 64
CONV_LW = 256


def _conv_glu_kernel(x_ref, mod_ref, w_ref, b_ref, a_ref):
    m = mod_ref[...]

    def front(sl):
        h = (x_ref[sl, :] * (1.0 + m[1:2]) + m[0:1]).astype(bf16)
        return _dot(h, w_ref[...])

    subs = _subs()
    nxt = front(subs[0])
    for s, sl in enumerate(subs):
        a = nxt + b_ref[...]
        if s + 1 < len(subs):
            nxt = front(subs[s + 1])
        a_ref[sl, :] = a[:, :D] * jax.nn.sigmoid(a[:, D:])


def _conv_glu(grp, x, mod, w, b):
    return pl.pallas_call(
        _conv_glu_kernel, grid=(grp.n // TM,),
        in_specs=[_rows(TM, D), _mod_spec(grp), _const(w.shape), _const(b.shape)],
        out_specs=_rows(TM, D), out_shape=jax.ShapeDtypeStruct((grp.n, D), f32),
        compiler_params=_cp(40), name="conv_glu",
    )(x, mod, w, b)


def _conv_close_kernel(seq_subs, ap_ref, a_ref, an_ref, x_ref, mod_ref, lnp_ref, wdw_ref, bdw_ref, ng_ref, nb_ref,
                       w2_ref, b2_ref, rwt_ref, x1_ref, h2_ref, aff_ref, pad_ref, act_ref, cout_ref, shift_ref):
    i = pl.program_id(0)
    nsub = TM // SUB
    bdw, ng, nb = bdw_ref[...], ng_ref[...], nb_ref[...]
    m, lnp = mod_ref[...], lnp_ref[...]
    for s, sl in enumerate(_subs()):
        gsub = i * nsub + s
        prev = a_ref[sl.start - HALO:sl.start, :] if s > 0 else ap_ref[...]
        nxt = a_ref[sl.stop:sl.stop + HALO, :] if s < nsub - 1 else an_ref[...]
        pad = pad_ref.at[s]
        pad[0:HALO, :] = jnp.where((gsub % seq_subs) != 0, prev, 0.0)
        pad[HALO:HALO + SUB, :] = a_ref[sl, :]
        pad[HALO + SUB:, :] = jnp.where((gsub % seq_subs) != seq_subs - 1, nxt, 0.0)
        span = SUB + 24
        for r in range(1, 8):
            shift_ref[s, r - 1, :, :] = pad[r:r + span, :]
        for rb in range(SUB // CONV_RB):
            r0 = rb * CONV_RB
            for lc in range(D // CONV_LW):
                lanes = slice(lc * CONV_LW, (lc + 1) * CONV_LW)
                acc = jnp.zeros((CONV_RB, CONV_LW), f32)
                for k in range(CONV_W):
                    mm, r = (k + 1) // 8, (k + 1) % 8
                    rows = slice(r0 + 8 * mm, r0 + 8 * mm + CONV_RB)
                    win = pad[rows, lanes] if r == 0 else shift_ref[s, r - 1, rows, lanes]
                    acc = acc + wdw_ref[k:k + 1, lanes] * win
                cout_ref[s, r0:r0 + CONV_RB, lanes] = acc
            y = _silu(_ln_rows(cout_ref[s, r0:r0 + CONV_RB, :] + bdw, ng, nb))
            act_ref[sl.start + r0:sl.start + r0 + CONV_RB, :] = y.astype(bf16)
        d = _dot(act_ref[sl, :], w2_ref[...]) + b2_ref[...]
        _close(d, x_ref[sl, :], m, lnp, rwt_ref, x1_ref, h2_ref, aff_ref, sl)


def _conv_close(grp, a, x, mod, lnp, w, rwt):
    n = grp.n
    hb = TM // HALO
    last = n // HALO - 1
    shapes, specs = _close_outs(n)
    return pl.pallas_call(
        functools.partial(_conv_close_kernel, grp.seq // SUB), grid=(n // TM,),
        in_specs=[pl.BlockSpec((HALO, D), lambda i: (jnp.maximum(i * hb - 1, 0), 0)),
                  _rows(TM, D),
                  pl.BlockSpec((HALO, D), lambda i: (jnp.minimum((i + 1) * hb, last), 0)),
                  _rows(TM, D), _mod_spec(grp), _const((8, D))]
        + [_const(t.shape) for t in w] + [_const((D, LANE))],
        out_specs=specs, out_shape=shapes,
        scratch_shapes=[pltpu.VMEM((TM // SUB, SUB + 2 * HALO, D), f32), pltpu.VMEM((TM, D), bf16),
                        pltpu.VMEM((TM // SUB, SUB, D), f32), pltpu.VMEM((TM // SUB, 7, SUB + 24, D), f32)],
        compiler_params=_cp(48), name="conv_close",
    )(a, a, a, x, mod, lnp, *w, rwt)


def _swa_pre_kernel(rope, x_ref, mod_ref, wq_ref, wk_ref, wv_ref, *rest):
    if rope:
        cos_ref, sin_ref, qrot_ref, qraw_ref, kd_ref, vd_ref = rest
    else:
        qraw_ref, kd_ref, vd_ref, k_ref, v_ref = rest
    m = mod_ref[...]

    def front(sl):
        h = (x_ref[sl, :] * (1.0 + m[1:2]) + m[0:1]).astype(bf16)
        return _dot(h, wq_ref[...]), _dot(h, wk_ref[...]), _dot(h, wv_ref[...])

    subs = _subs()
    nxt = front(subs[0])
    for s, sl in enumerate(subs):
        q, kd, vd = nxt
        if s + 1 < len(subs):
            nxt = front(subs[s + 1])
        qraw_ref[sl, :] = q.astype(bf16)
        vd_ref[sl, :] = vd.astype(bf16)
        if rope:
            cos, sin = cos_ref[sl, :], sin_ref[sl, :]
            qrot_ref[sl, :] = _rope(q, cos, sin).astype(bf16)
            kd_ref[sl, :] = _rope(kd, cos, sin).astype(bf16)
        else:
            kd_ref[sl, :] = kd.astype(bf16)
            lane = lax.broadcasted_iota(i32, (SUB, LANE), 1)
            for j in range(SWA_KV // 2):
                lo, hi = 2 * j * LANE, (2 * j + 1) * LANE
                k_ref[sl, j * LANE:(j + 1) * LANE] = jnp.where(lane < SWA_HD, kd[:, lo:lo + LANE], kd[:, hi:hi + LANE])
                v_ref[sl, j * LANE:(j + 1) * LANE] = jnp.where(lane < SWA_HD, vd[:, lo:lo + LANE], vd[:, hi:hi + LANE])


def _swa_pre(grp, x, mod, w, rope_tabs):
    n = grp.n
    ins = [x, mod] + list(w)
    specs = [_rows(TM, D), _mod_spec(grp)] + [_const(a.shape) for a in w]
    kw = 2 * SWA_KV * SWA_HD
    qs, ks = jax.ShapeDtypeStruct((n, D), bf16), jax.ShapeDtypeStruct((n, kw), bf16)
    if rope_tabs is not None:
        ins += list(rope_tabs)
        specs += [_tab_spec(), _tab_spec()]
        shapes = (qs, qs, ks, ks)
        ospecs = (_rows(TM, D), _rows(TM, D), _rows(TM, kw), _rows(TM, kw))
    else:
        nat = jax.ShapeDtypeStruct((n, SWA_KV * SWA_HD), f32)
        shapes = (qs, ks, ks, nat, nat)
        ospecs = (_rows(TM, D), _rows(TM, kw), _rows(TM, kw), _rows(TM, SWA_KV * SWA_HD), _rows(TM, SWA_KV * SWA_HD))
    return pl.pallas_call(
        functools.partial(_swa_pre_kernel, rope_tabs is not None),
        grid=(n // TM,), in_specs=specs, out_specs=ospecs, out_shape=shapes,
        compiler_params=_cp(40), name="swa_pre",
    )(*ins)


def _half_mask(x, parity):
    lane = lax.broadcasted_iota(i32, x.shape, 1)
    keep = (lane < SWA_HD) if parity == 0 else (lane >= SWA_HD)
    return jnp.where(keep, x, jnp.zeros_like(x))


def _swa_attn_p_kernel(sink_ref, q_ref, kd_ref, vd_ref, o_ref):
    def scores(hd):
        pair, par = hd // 2, hd % 2
        g = hd // (SWA_HEADS // SWA_KV)
        return _dot_nt(q_ref[:, pair * LANE:(pair + 1) * LANE], _half_mask(kd_ref[:, g * LANE:(g + 1) * LANE], par))

    nxt = scores(0)
    acc = None
    for hd in range(SWA_HEADS):
        pair, par = hd // 2, hd % 2
        g = hd // (SWA_HEADS // SWA_KV)
        s = nxt
        if hd + 1 < SWA_HEADS:
            nxt = scores(hd + 1)
        sk = sink_ref[hd] * (1.0 / SWA_SCALE)
        m = jnp.maximum(jnp.max(s, axis=-1, keepdims=True), sk)
        e = jnp.exp2((s - m) * SWA_C2)
        l = jnp.sum(e, axis=-1, keepdims=True) + jnp.exp2((sk - m) * SWA_C2)
        o = _dot(e.astype(bf16), _half_mask(vd_ref[:, g * LANE:(g + 1) * LANE], par)) / l
        acc = o if par == 0 else acc + o
        if par == 1:
            o_ref[:, pair * LANE:(pair + 1) * LANE] = acc.astype(bf16)


def _swa_attn_p(sink, q, kd, vd):
    n = q.shape[0]
    kw = kd.shape[1]
    return pl.pallas_call(
        _swa_attn_p_kernel,
        grid_spec=pltpu.PrefetchScalarGridSpec(
            num_scalar_prefetch=1, grid=(n // SEQ,),
            in_specs=[_rows(SEQ, D), _rows(SEQ, kw), _rows(SEQ, kw)],
            out_specs=_rows(SEQ, D)),
        out_shape=jax.ShapeDtypeStruct((n, D), bf16),
        compiler_params=_cp(24), name="swa_attn_ctx",
    )(sink, q, kd, vd)


SWA_QB = 128


def _swa_attn_s_kernel(sink_ref, qr_ref, qw_ref, kd_ref, vd_ref, kc_ref, vc_ref, o_ref):
    nblk = pl.program_id(1)
    span = 3 * SWA_QB
    start = pl.multiple_of(jnp.clip((nblk - 1) * SWA_QB, 0, DEC_SEQ - span), SWA_QB)
    qpos = nblk * SWA_QB + lax.broadcasted_iota(i32, (SWA_QB, span), 0)
    kpos = start + lax.broadcasted_iota(i32, (SWA_QB, span), 1)
    band = jnp.abs(kpos - qpos) <= SWA_WIN

    def scores(hd):
        pair, par = hd // 2, hd % 2
        g = hd // (SWA_HEADS // SWA_KV)
        cols = slice(g * LANE, (g + 1) * LANE)
        s1 = _dot_nt(qr_ref[:, pair * LANE:(pair + 1) * LANE], _half_mask(kd_ref[pl.ds(start, span), cols], par))
        s2 = _dot_nt(qw_ref[:, pair * LANE:(pair + 1) * LANE], _half_mask(kc_ref[:, cols], par))
        return s1, s2

    nxt = scores(0)
    acc = None
    for hd in range(SWA_HEADS):
        pair, par = hd // 2, hd % 2
        g = hd // (SWA_HEADS // SWA_KV)
        cols = slice(g * LANE, (g + 1) * LANE)
        s1, s2 = nxt
        if hd + 1 < SWA_HEADS:
            nxt = scores(hd + 1)
        s1 = jnp.where(band, s1, NEG_INF)
        sk = sink_ref[hd] * (1.0 / SWA_SCALE)
        m = jnp.maximum(jnp.maximum(jnp.max(s1, axis=-1, keepdims=True), jnp.max(s2, axis=-1, keepdims=True)), sk)
        e1, e2 = jnp.exp2((s1 - m) * SWA_C2), jnp.exp2((s2 - m) * SWA_C2)
        l = (jnp.sum(e1, axis=-1, keepdims=True) + jnp.sum(e2, axis=-1, keepdims=True)
             + jnp.exp2((sk - m) * SWA_C2))
        o = (_dot(e1.astype(bf16), _half_mask(vd_ref[pl.ds(start, span), cols], par))
             + _dot(e2.astype(bf16), _half_mask(vc_ref[:, cols], par))) / l
        acc = o if par == 0 else acc + o
        if par == 1:
            o_ref[:, pair * LANE:(pair + 1) * LANE] = acc.astype(bf16)


def _swa_attn_s(sink, qr, qw, kd, vd, kc, vc):
    nq = DEC_SEQ // SWA_QB
    kw = kd.shape[1]
    qs = pl.BlockSpec((SWA_QB, D), lambda b, i, *_: (b * nq + i, 0))
    full = pl.BlockSpec((DEC_SEQ, kw), lambda b, i, *_: (b, 0))
    ctx = pl.BlockSpec((PAST, kw), lambda b, i, *_: (b, 0))
    return pl.pallas_call(
        _swa_attn_s_kernel,
        grid_spec=pltpu.PrefetchScalarGridSpec(
            num_scalar_prefetch=1, grid=(DEC_BATCH, nq),
            in_specs=[qs, qs, full, full, ctx, ctx],
            out_specs=pl.BlockSpec((SWA_QB, D), lambda b, i, *_: (b * nq + i, 0))),
        out_shape=jax.ShapeDtypeStruct((NS_TOK, D), bf16),
        compiler_params=_cp(32, 2), name="swa_attn_lat",
    )(sink, qr, qw, kd, vd, kc, vc)


def _excl_prefix(mask_f, nb, tri, blk):
    m2 = mask_f.reshape(nb * NE, RT)
    within = _dot(m2.astype(bf16), tri)
    tot = jnp.sum(m2, axis=1, keepdims=True)
    totb = jnp.broadcast_to(tot, (nb * NE, LANE)).astype(bf16)
    offs = _dot(blk, totb)
    return (within + offs[:, 0:1]).reshape(nb, NE, RT), offs.reshape(nb, NE, LANE)


def _route_kernel(cap, nb, aff_ref, dest_ref, gate_ref, off_ref):
    a = aff_ref[...]

    def as_f32(bits):
        return pltpu.bitcast(bits, f32)[None]

    def count_ge(th):
        c = jnp.sum(jnp.where(a >= as_f32(th), 1.0, 0.0), axis=0)
        return jnp.sum(c, axis=1, keepdims=True)

    def body(_, c):
        lo, hi = c
        mid = lo + ((hi - lo + 1) >> 1)
        ok = count_ge(mid) >= cap
        return jnp.where(ok, mid, lo), jnp.where(ok, hi, mid - 1)

    lo0 = jnp.zeros((NE, 1), i32)
    hi0 = jnp.full((NE, 1), 0x7F800000, i32)
    thr, _ = lax.fori_loop(0, 31, body, (lo0, hi0))

    r = lax.broadcasted_iota(i32, (RT, RT), 0)
    c = lax.broadcasted_iota(i32, (RT, RT), 1)
    tri = jnp.where(r < c, 1.0, 0.0).astype(bf16)
    rr = lax.broadcasted_iota(i32, (nb * NE, nb * NE), 0)
    cc = lax.broadcasted_iota(i32, (nb * NE, nb * NE), 1)
    blk = jnp.where(((rr % NE) == (cc % NE)) & (cc < rr), 1.0, 0.0).astype(bf16)

    gt = a > as_f32(thr)
    eq = a == as_f32(thr)
    n_gt = jnp.sum(jnp.sum(jnp.where(gt, 1.0, 0.0), axis=0), axis=1, keepdims=True)
    need = cap - n_gt
    tie_rank, _ = _excl_prefix(jnp.where(eq, 1.0, 0.0), nb, tri, blk)
    sel = gt | (eq & (tie_rank < need[None]))
    pos, offs = _excl_prefix(jnp.where(sel, 1.0, 0.0), nb, tri, blk)
    dest_ref[...] = jnp.where(sel, pos, -1.0).astype(i32)
    gate_ref[...] = jnp.where(sel, a, 0.0)
    off_ref[...] = offs.astype(i32)


def _route(grp, aff3):
    nb = grp.n // RT
    full = pl.BlockSpec((nb, NE, RT), lambda: (0, 0, 0))
    offspec = pl.BlockSpec((nb, NE, LANE), lambda: (0, 0, 0))
    return pl.pallas_call(
        functools.partial(_route_kernel, grp.cap, nb),
        in_specs=[full], out_specs=(full, full, offspec),
        out_shape=(jax.ShapeDtypeStruct((nb, NE, RT), i32), jax.ShapeDtypeStruct((nb, NE, RT), f32),
                   jax.ShapeDtypeStruct((nb, NE, LANE), i32)),
        compiler_params=pltpu.CompilerParams(vmem_limit_bytes=40 * MIB), name="route",
    )(aff3)


def _route_tables(grp, dest3, off3):
    ct = off3[:, :, 0].T
    ct_end = jnp.concatenate([ct[:, 1:], jnp.full((NE, 1), grp.cap, i32)], axis=1)
    wstart = (ct // 16) * 16
    nq = jnp.max((ct_end - wstart + WIN - 1) // WIN, axis=0)
    destp = jnp.transpose(dest3, (0, 2, 1)).reshape(grp.n, NE) + 1
    destab = jnp.concatenate([destp // 32, destp % 32], axis=1).astype(bf16)
    return wstart.astype(i32), nq.astype(i32), destab


def _dispatch_kernel(cap, eg, ws_ref, nq_ref, h2_ref, dest_ref, gate_ref, xe_ref, gcol_ref, hot_ref):
    g, tb = pl.program_id(0), pl.program_id(1)

    @pl.when(tb == 0)
    def _():
        xe_ref[...] = jnp.zeros_like(xe_ref)
        gcol_ref[...] = jnp.zeros_like(gcol_ref)

    wrow = lax.broadcasted_iota(i32, (WIN, RT), 0)

    def body(q, carry):
        starts, gsums = [], []
        for el in range(eg):
            e = g * eg + el
            want = ws_ref[e, tb] + q * WIN
            st = pl.multiple_of(jnp.minimum(want, cap - WIN), 16)
            row = st + wrow
            hit = dest_ref[pl.ds(e, 1), :] == jnp.where(row >= want, row, -7)
            hot_ref[el * WIN:(el + 1) * WIN, :] = jnp.where(hit, 1.0, 0.0).astype(bf16)
            gsums.append(jnp.sum(jnp.where(hit, gate_ref[pl.ds(e, 1), :], 0.0), axis=1, keepdims=True))
            starts.append(st)
        part = _dot(hot_ref[...], h2_ref[...])
        for el in range(eg):
            dst = pl.ds(el * cap + starts[el], WIN)
            xe_ref[dst, :] = xe_ref[dst, :] + part[el * WIN:(el + 1) * WIN, :].astype(bf16)
            gcol_ref[dst, :] = gcol_ref[dst, :] + jnp.broadcast_to(gsums[el], (WIN, LANE))
        return carry

    lax.fori_loop(0, nq_ref[tb], body, 0)


def _dispatch(grp, wstart, nq, h2, dest3, gate3):
    eg = ROWS_PER_PASS // grp.cap
    nb = grp.n // RT
    tab = pl.BlockSpec((None, NE, RT), lambda g, t, *_: (t, 0, 0))
    return pl.pallas_call(
        functools.partial(_dispatch_kernel, grp.cap, eg),
        grid_spec=pltpu.PrefetchScalarGridSpec(
            num_scalar_prefetch=2, grid=(NE // eg, nb),
            in_specs=[pl.BlockSpec((RT, D), lambda g, t, *_: (t, 0)), tab, tab],
            out_specs=(pl.BlockSpec((ROWS_PER_PASS, D), lambda g, t, *_: (g, 0)),
                       pl.BlockSpec((ROWS_PER_PASS, LANE), lambda g, t, *_: (g, 0))),
            scratch_shapes=[pltpu.VMEM((eg * WIN, RT), bf16)]),
        out_shape=(jax.ShapeDtypeStruct((NE * grp.cap, D), bf16), jax.ShapeDtypeStruct((NE * grp.cap, LANE), f32)),
        compiler_params=_cp(56, 2), name="moe_dispatch",
    )(wstart, nq, h2, dest3, gate3)


FFN_RB = 256


def _ffn_kernel(xp_ref, gp_ref, xs_ref, gs_ref, wg_ref, wu_ref, wd_ref, yp_ref, ys_ref, wgb, wub, wdb):
    wgb[...] = wg_ref[...].astype(bf16)
    wub[...] = wu_ref[...].astype(bf16)
    wdb[...] = wd_ref[...].astype(bf16)
    for x_ref, g_ref, y_ref, cap in ((xp_ref, gp_ref, yp_ref, CTX.cap), (xs_ref, gs_ref, ys_ref, LAT.cap)):
        for r in range(cap // FFN_RB):
            sl = slice(r * FFN_RB, (r + 1) * FFN_RB)
            x = x_ref[sl, :]
            hid = (_silu(_dot(x, wgb[...])) * _dot(x, wub[...])).astype(bf16)
            gate = jnp.concatenate([g_ref[sl, :]] * (D // LANE), axis=1)
            y_ref[sl, :] = (_dot(hid, wdb[...]) * gate).astype(bf16)


def _ffn(layer, xp, gp, xs, gs, wg, wu, wd):
    wspec = pl.BlockSpec((None, None, D, FF), lambda e: (layer, e, 0, 0))
    cp, cs = CTX.cap, LAT.cap
    return pl.pallas_call(
        _ffn_kernel, grid=(NE,),
        in_specs=[pl.BlockSpec((cp, D), lambda e: (e, 0)), pl.BlockSpec((cp, LANE), lambda e: (e, 0)),
                  pl.BlockSpec((cs, D), lambda e: (e, 0)), pl.BlockSpec((cs, LANE), lambda e: (e, 0)),
                  wspec, wspec, pl.BlockSpec((None, None, FF, D), lambda e: (layer, e, 0, 0))],
        out_specs=(pl.BlockSpec((cp, D), lambda e: (e, 0)), pl.BlockSpec((cs, D), lambda e: (e, 0))),
        out_shape=(jax.ShapeDtypeStruct((NE * cp, D), bf16), jax.ShapeDtypeStruct((NE * cs, D), bf16)),
        scratch_shapes=[pltpu.VMEM((D, FF), bf16), pltpu.VMEM((D, FF), bf16), pltpu.VMEM((FF, D), bf16)],
        compiler_params=_cp(56), name="moe_ffn",
    )(xp, gp, xs, gs, wg, wu, wd)


def _combine_kernel(cap, ws_ref, nq_ref, yg_ref, dab_ref, x1_ref, mod_ref, lnp_ref, x2_ref, win_ref):
    tb = pl.program_id(0)
    kw = NE * WIN
    lane = lax.broadcasted_iota(i32, (1, kw), 1)
    lane_e, lane_w = lane // WIN, lane % WIN
    er = lax.broadcasted_iota(i32, (2 * NE, kw), 0)
    ec = lax.broadcasted_iota(i32, (2 * NE, kw), 1) // WIN
    spread = jnp.where(er == ec, 32.0, jnp.where(er - NE == ec, 1.0, 0.0)).astype(bf16)
    destp = _dot(dab_ref[...], spread)

    def body(q, y):
        tgt = jnp.full((1, kw), -1, i32)
        for e in range(NE):
            want = ws_ref[e, tb] + q * WIN
            st = pl.multiple_of(jnp.minimum(want, cap - WIN), 16)
            win_ref[e * WIN:(e + 1) * WIN, :] = yg_ref[pl.ds(e * cap + st, WIN), :]
            row = st + lane_w
            tgt = jnp.where(lane_e == e, jnp.where(row >= want, row + 1, -1), tgt)
        onehot = jnp.where(destp == tgt.astype(f32), 1.0, 0.0).astype(bf16)
        return y + _dot(onehot, win_ref[...])

    y = lax.fori_loop(0, nq_ref[tb], body, jnp.zeros((RT, D), f32))
    m, lnp = mod_ref[...], lnp_ref[...]
    x2_ref[...] = _ln_rows(ALPHA * x1_ref[...] + m[5:6] * y, lnp[2:3], lnp[3:4])


def _combine(grp, wstart, nq, yg, destab, x1, mod, lnp):
    seg = pl.BlockSpec((None, 8, D), lambda i, *_: (grp.seg0 + (i * RT) // grp.seq if grp.seq > RT else grp.seg0, 0, 0))
    return pl.pallas_call(
        functools.partial(_combine_kernel, grp.cap),
        grid_spec=pltpu.PrefetchScalarGridSpec(
            num_scalar_prefetch=2, grid=(grp.n // RT,),
            in_specs=[_const(yg.shape), _rows(RT, 2 * NE), _rows(RT, D), seg, _const((8, D))],
            out_specs=_rows(RT, D),
            scratch_shapes=[pltpu.VMEM((NE * WIN, D), bf16)]),
        out_shape=jax.ShapeDtypeStruct((grp.n, D), f32),
        compiler_params=_cp(52), name="moe_combine",
    )(wstart, nq, yg, destab, x1, mod, lnp)


def _moe(layer, closed, mod, lnp, wg, wu, wd):
    disp = []
    for grp, (x1, h2, aff3) in zip((CTX, LAT), closed):
        dest3, gate3, off3 = _route(grp, aff3)
        wstart, nq, destab = _route_tables(grp, dest3, off3)
        xe, gcol = _dispatch(grp, wstart, nq, h2, dest3, gate3)
        disp.append((xe, gcol, wstart, nq, destab, x1))
    ys = _ffn(layer, disp[0][0], disp[0][1], disp[1][0], disp[1][1], wg, wu, wd)
    return tuple(_combine(grp, d[2], d[3], y, d[4], d[5], mod, lnp)
                 for grp, d, y in zip((CTX, LAT), disp, ys))


def _mla_weights(wq_a, q_norm, wq_b, wkv_a, kv_norm, wkv_b):
    wqb = wq_b.reshape(MLA_Q_LORA, MLA_HEADS, MLA_NOPE + MLA_ROPE)
    wqb = jnp.concatenate([wqb[:, :, :MLA_NOPE].reshape(MLA_Q_LORA, -1), wqb[:, :, MLA_NOPE:].reshape(MLA_Q_LORA, -1)], axis=1)
    wkva = jnp.concatenate([wkv_a, wkv_a[:, MLA_KV_LORA:]], axis=1)
    wkvb = wkv_b.reshape(MLA_KV_LORA, MLA_HEADS, MLA_NOPE + MLA_V)
    wkvb = jnp.concatenate([wkvb[:, :, :MLA_NOPE].reshape(MLA_KV_LORA, -1), wkvb[:, :, MLA_NOPE:].reshape(MLA_KV_LORA, -1)], axis=1)
    return (wq_a.astype(bf16), q_norm.reshape(1, -1), wqb.astype(bf16), wkva.astype(bf16), kv_norm.reshape(1, -1),
            wkvb.astype(bf16))


def _dup_heads(w):
    w = w.reshape(w.shape[0], SWA_KV, 1, SWA_HD)
    return jnp.broadcast_to(w, (w.shape[0], SWA_KV, 2, SWA_HD)).reshape(w.shape[0], 2 * SWA_KV * SWA_HD)


def kernel(x_prompt, x_sample, c, cache_mla_ckv, cache_mla_kpe, cache_swa_k, cache_swa_v, c_ctx, w_mod, b_mod, ln_gain, ln_bias, router_w, moe_w_gate, moe_w_up, moe_w_down, mla_wq_a, mla_q_norm, mla_wq_b, mla_wkv_a, mla_kv_norm, mla_wkv_b, mla_wo, gm_w_in, gm_b_in, gm_v_norm_g, gm_v_norm_b, gm_w_s, gm_b_s, gm_w_out, gm_b_out, cv_w_pw1, cv_b_pw1, cv_w_dw, cv_b_dw, cv_norm_g, cv_norm_b, cv_w_pw2, cv_b_pw2, swa_wq, swa_wk, swa_wv, swa_sink, swa_wo):
    groups = (CTX, LAT)
    xs = (x_prompt.reshape(NP_TOK, D), x_sample.reshape(NS_TOK, D))
    cond8 = jnp.concatenate([c_ctx[None, :], c, jnp.zeros((5, D), f32)], axis=0)
    mods = _modulation(cond8, w_mod, b_mod)
    lnps = jnp.concatenate([jnp.stack([ln_gain[:, 0], ln_bias[:, 0], ln_gain[:, 1], ln_bias[:, 1]], axis=1),
                            jnp.zeros((DEPTH, 4, D), f32)], axis=1)
    rwts = jnp.pad(router_w, ((0, 0), (0, 0), (0, LANE - NE)))
    tabs = _rope_tables(DEC_SEQ)

    def moe(i, closed):
        return _moe(i, closed, mods[i], lnps[i], moe_w_gate, moe_w_up, moe_w_down)

    wm = _mla_weights(mla_wq_a[0], mla_q_norm[0], mla_wq_b[0], mla_wkv_a[0], mla_kv_norm[0], mla_wkv_b[0])
    qc_p, kc_p, v_p, ckv_p, kpe_p = _mla_pre(CTX, xs[0], mods[0], wm, None)
    ql_s, qc_s, kc_s, v_s = _mla_pre(LAT, xs[1], mods[0], wm, tabs)
    cache_kpe = cache_mla_kpe[:, 0].reshape(DEC_BATCH * PAST, MLA_ROPE)
    kc_c, v_c = _mla_ctx(cache_mla_ckv[:, 0].reshape(DEC_BATCH * PAST, MLA_KV_LORA),
                         jnp.concatenate([cache_kpe, cache_kpe], axis=1), wm[5])
    os_ = (_mla_attn_p(qc_p, kc_p, v_p), _mla_attn_s(ql_s, qc_s, kc_s, v_s, kc_c, v_c))
    wo = mla_wo[0].astype(bf16)
    xs = moe(0, [_proj_close(g, o, x, mods[0], lnps[0], wo, rwts[0]) for g, o, x in zip(groups, os_, xs)])

    wgm = (gm_w_in[0].astype(bf16), gm_b_in[0].reshape(1, -1), gm_v_norm_g[0].reshape(1, -1),
           gm_v_norm_b[0].reshape(1, -1), gm_w_s[0].astype(bf16), gm_b_s[0].T, gm_w_out[0].astype(bf16),
           gm_b_out[0].reshape(1, -1))
    xs = moe(1, [_gmlp_layer(g, x, mods[1], lnps[1], wgm, rwts[1]) for g, x in zip(groups, xs)])

    w1, b1 = cv_w_pw1[0].astype(bf16), cv_b_pw1[0].reshape(1, -1)
    wcv = (cv_w_dw[0], cv_b_dw[0].reshape(1, -1), cv_norm_g[0].reshape(1, -1), cv_norm_b[0].reshape(1, -1),
           cv_w_pw2[0].astype(bf16), cv_b_pw2[0].reshape(1, -1))
    xs = moe(2, [_conv_close(g, _conv_glu(g, x, mods[2], w1, b1), x, mods[2], lnps[2], wcv, rwts[2])
                 for g, x in zip(groups, xs)])

    wsw = (swa_wq[0].astype(bf16), _dup_heads(swa_wk[0]).astype(bf16), _dup_heads(swa_wv[0]).astype(bf16))
    q_p, kd_p, vd_p, k_p, v_p = _swa_pre(CTX, xs[0], mods[3], wsw, None)
    qr_s, qw_s, kd_s, vd_s = _swa_pre(LAT, xs[1], mods[3], wsw, tabs)

    def dup_cache(t):
        return _dup_heads(t[:, 0].reshape(DEC_BATCH * PAST, SWA_KV * SWA_HD)).astype(bf16)

    sink = swa_sink[0]
    os_ = (_swa_attn_p(sink, q_p, kd_p, vd_p),
           _swa_attn_s(sink, qr_s, qw_s, kd_s, vd_s, dup_cache(cache_swa_k), dup_cache(cache_swa_v)))
    wo = swa_wo[0].astype(bf16)
    xs = moe(3, [_proj_close(g, o, x, mods[3], lnps[3], wo, rwts[3]) for g, o, x in zip(groups, os_, xs)])

    return (xs[0].reshape(BATCH, SEQ, D), xs[1].reshape(DEC_BATCH, DEC_SEQ, D),
            ckv_p.reshape(BATCH, 1, SEQ, MLA_KV_LORA), kpe_p.reshape(BATCH, 1, SEQ, MLA_ROPE),
            k_p.reshape(BATCH, 1, SEQ, SWA_KV, SWA_HD), v_p.reshape(BATCH, 1, SEQ, SWA_KV, SWA_HD))
```

```python
import functools
import math

import jax
import jax.numpy as jnp
from jax import lax
from jax.experimental import pallas as pl
from jax.experimental.pallas import tpu as pltpu

f32 = jnp.float32
bf16 = jnp.bfloat16
i32 = jnp.int32

D = 1024
BATCH, SEQ = 32, 256
DEC_BATCH, DEC_SEQ = 2, 2048
PAST = 256
DEPTH = 4
GRID_W = 64
ALPHA = (2 * DEPTH) ** 0.25
LN_EPS = 1e-5
RMS_EPS = 1e-6
ROPE_BASE = 10000.0
NEG_INF = -1e30
MLA_HEADS, MLA_NOPE, MLA_ROPE, MLA_V = 8, 128, 64, 128
MLA_Q_LORA, MLA_KV_LORA = 384, 256
MLA_SCALE = (MLA_NOPE + MLA_ROPE) ** -0.5
MLA_C2 = MLA_SCALE * math.log2(math.e)
GM_CHUNK, GM_HALF, GM_GROUPS = 128, 2048, 4
CONV_W = 31
SWA_HEADS, SWA_KV, SWA_HD, SWA_WIN = 16, 4, 64, 128
SWA_SCALE = SWA_HD ** -0.5
SWA_C2 = SWA_SCALE * math.log2(math.e)
NE = 16
FF = 1024

NP_TOK = BATCH * SEQ
NS_TOK = DEC_BATCH * DEC_SEQ

LANE = 128
TM = 512
SUB = 256
RT = 256
WIN = 64
BLOCKS_PER_STEP = 2
ROWS_PER_PASS = 8192
MIB = 2 ** 20


class Group:
    def __init__(self, n_tok, seq, seg0):
        self.n = n_tok
        self.seq = seq
        self.cap = 2 * n_tok // NE
        self.seg0 = seg0

    def seg_of(self, i):
        return self.seg0 + (i * TM) // self.seq if self.seq > TM else self.seg0


CTX = Group(NP_TOK, SEQ, 0)
LAT = Group(NS_TOK, DEC_SEQ, 1)


def _cp(vmem_mb, n_axes=1):
    return pltpu.CompilerParams(dimension_semantics=("arbitrary",) * n_axes,
                                vmem_limit_bytes=int(vmem_mb * MIB))


def _const(shape):
    nd = len(shape)
    return pl.BlockSpec(shape, lambda *_: (0,) * nd, pipeline_mode=pl.Buffered(1))


def _rows(tm, c):
    return pl.BlockSpec((tm, c), lambda i, *_: (i, 0))


def _mod_spec(grp):
    return pl.BlockSpec((None, 8, D), lambda i, *_: (grp.seg_of(i), 0, 0))


def _dot(a, b):
    return jnp.dot(a, b, preferred_element_type=f32)


def _dot_nt(a, b):
    return lax.dot_general(a, b, (((1,), (1,)), ((), ())), preferred_element_type=f32)


def _silu(x):
    return x * jax.nn.sigmoid(x)


def _ln_rows(x, g, b):
    mu = jnp.mean(x, axis=-1, keepdims=True)
    xc = x - mu
    var = jnp.mean(xc * xc, axis=-1, keepdims=True)
    return xc * lax.rsqrt(var + LN_EPS) * g + b


def _rms_rows(x, g):
    return x * lax.rsqrt(jnp.mean(x * x, axis=-1, keepdims=True) + RMS_EPS) * g


def _split_bf16(x):
    hi = x.astype(bf16)
    lo = (x - hi.astype(f32)).astype(bf16)
    return hi, lo


def _subs():
    return [slice(s * SUB, (s + 1) * SUB) for s in range(TM // SUB)]


def _mod_kernel(c_ref, w_ref, b_ref, o_ref):
    x = _silu(c_ref[...])
    xh, xl = _split_bf16(x)
    wh, wl = _split_bf16(w_ref[...])
    o_ref[...] = _dot(xh, wh) + _dot(xh, wl) + _dot(xl, wh) + b_ref[...]


def _modulation(cond8, w_mod, b_mod):
    nk = 6
    out = pl.pallas_call(
        _mod_kernel,
        grid=(DEPTH, nk),
        in_specs=[pl.BlockSpec((8, D), lambda l, k: (0, 0)),
                  pl.BlockSpec((None, D, D), lambda l, k: (l, 0, k)),
                  pl.BlockSpec((None, None, 1, D), lambda l, k: (l, k, 0, 0))],
        out_specs=pl.BlockSpec((None, None, 8, D), lambda l, k: (l, k, 0, 0)),
        out_shape=jax.ShapeDtypeStruct((DEPTH, nk, 8, D), f32),
        compiler_params=_cp(32, 2),
        name="modulation",
    )(cond8, w_mod, b_mod.reshape(DEPTH, nk, 1, D))
    out = jnp.transpose(out, (0, 2, 1, 3))
    return jnp.pad(out, ((0, 0), (0, 0), (0, 2), (0, 0)))


def _router_aff_t(h2, rw_ref):
    hh, hl = _split_bf16(h2)
    wh, wl = _split_bf16(rw_ref[...])
    lg = _dot(hh, wh) + _dot(hl, wh) + _dot(hh, wl)
    lg = jnp.transpose(lg)[:NE, :]
    m = jnp.max(lg, axis=0, keepdims=True)
    e = jnp.exp(lg - m)
    return e / jnp.sum(e, axis=0, keepdims=True)


def _close(d, x, m, lnp, rwt_ref, x1_ref, h2_ref, aff_ref, sl):
    x1 = _ln_rows(ALPHA * x + m[2:3] * d, lnp[0:1], lnp[1:2])
    x1_ref[sl, :] = x1
    h2 = x1 * (1.0 + m[4:5]) + m[3:4]
    h2_ref[sl, :] = h2.astype(bf16)
    aff_ref[sl.start // RT] = _router_aff_t(h2, rwt_ref)


def _close_outs(n):
    shapes = (jax.ShapeDtypeStruct((n, D), f32), jax.ShapeDtypeStruct((n, D), bf16),
              jax.ShapeDtypeStruct((n // RT, NE, RT), f32))
    specs = (_rows(TM, D), _rows(TM, D),
             pl.BlockSpec((TM // RT, NE, RT), lambda i, *_: (i, 0, 0)))
    return shapes, specs


def _proj_close_kernel(o_ref, x_ref, mod_ref, lnp_ref, wo_ref, rwt_ref, x1_ref, h2_ref, aff_ref):
    m, lnp = mod_ref[...], lnp_ref[...]
    subs = _subs()
    nxt = _dot(o_ref[subs[0], :], wo_ref[...])
    for s, sl in enumerate(subs):
        d = nxt
        if s + 1 < len(subs):
            nxt = _dot(o_ref[subs[s + 1], :], wo_ref[...])
        _close(d, x_ref[sl, :], m, lnp, rwt_ref, x1_ref, h2_ref, aff_ref, sl)


def _proj_close(grp, o, x, mod, lnp, wo, rwt):
    shapes, specs = _close_outs(grp.n)
    return pl.pallas_call(
        _proj_close_kernel,
        grid=(grp.n // TM,),
        in_specs=[_rows(TM, o.shape[1]), _rows(TM, D), _mod_spec(grp), _const((8, D)),
                  _const(wo.shape), _const((D, LANE))],
        out_specs=specs, out_shape=shapes,
        compiler_params=_cp(40), name="proj_close",
    )(o, x, mod, lnp, wo, rwt)


def _rope(x, cos, sin):
    w = x.shape[1]
    reps = w // LANE
    c = jnp.concatenate([cos] * reps, axis=1) if reps > 1 else cos
    s = jnp.concatenate([sin] * reps, axis=1) if reps > 1 else sin
    lane = lax.broadcasted_iota(i32, x.shape, 1)
    up = pltpu.roll(x, w - 16, 1)
    dn = pltpu.roll(x, 16, 1)
    partner = jnp.where((lane % 32) < 16, up, dn)
    return x * c + partner * s


def _rope_tables(length):
    t = jnp.arange(length)
    rows, cols = (t // GRID_W).astype(f32), (t % GRID_W).astype(f32)
    inv = ROPE_BASE ** (-jnp.arange(16, dtype=f32) / 16)
    ar, ac = rows[:, None] * inv[None, :], cols[:, None] * inv[None, :]
    cos = jnp.concatenate([jnp.cos(ar), jnp.cos(ar), jnp.cos(ac), jnp.cos(ac)], axis=1)
    sin = jnp.concatenate([-jnp.sin(ar), jnp.sin(ar), -jnp.sin(ac), jnp.sin(ac)], axis=1)
    return jnp.concatenate([cos, cos], axis=1), jnp.concatenate([sin, sin], axis=1)


def _tab_spec():
    per = DEC_SEQ // TM
    return pl.BlockSpec((TM, LANE), lambda i, *_: (i % per, 0))


def _mla_pre_kernel(rope, x_ref, mod_ref, wqa_ref, qn_ref, wqb_ref, wkva_ref, kvn_ref, wkvb_ref, *rest):
    if rope:
        cos_ref, sin_ref, qlat_ref, qctx_ref, kcat_ref, v_ref = rest
    else:
        qctx_ref, kcat_ref, v_ref, ckv_ref, kpe_ref = rest
    m = mod_ref[...]

    def front(sl):
        h = (x_ref[sl, :] * (1.0 + m[1:2]) + m[0:1]).astype(bf16)
        return _dot(h, wqa_ref[...]), _dot(h, wkva_ref[...])

    subs = _subs()
    nxt = front(subs[0])
    for s, sl in enumerate(subs):
        qa, kv = nxt
        if s + 1 < len(subs):
            nxt = front(subs[s + 1])
        qa = _rms_rows(qa, qn_ref[...])
        q = _dot(qa.astype(bf16), wqb_ref[...])
        ckv = _rms_rows(kv[:, :MLA_KV_LORA], kvn_ref[...])
        kpe2 = kv[:, MLA_KV_LORA:MLA_KV_LORA + LANE]
        kvb = _dot(ckv.astype(bf16), wkvb_ref[...])
        q_pe = q[:, MLA_HEADS * MLA_NOPE:]
        if rope:
            cos, sin = cos_ref[sl, :], sin_ref[sl, :]
            q_pe_rot = _rope(q_pe, cos, sin).astype(bf16)
            kpe2 = _rope(kpe2, cos, sin)
        else:
            ckv_ref[sl, :] = ckv
            kpe_ref[sl, :] = kv[:, MLA_KV_LORA:MLA_KV_LORA + MLA_ROPE]
        q_pe = q_pe.astype(bf16)
        qn = q[:, :MLA_HEADS * MLA_NOPE].astype(bf16)
        kn = kvb[:, :MLA_HEADS * MLA_NOPE].astype(bf16)
        v_ref[sl, :] = kvb[:, MLA_HEADS * MLA_NOPE:].astype(bf16)
        lane = lax.broadcasted_iota(i32, kpe2.shape, 1)
        kpe_lo = jnp.where(lane < MLA_ROPE, kpe2, 0.0).astype(bf16)
        kpe_hi = jnp.where(lane >= MLA_ROPE, kpe2, 0.0).astype(bf16)
        for hh in range(MLA_HEADS):
            a, b = hh * 256, hh * 256 + LANE
            pr = (hh // 2) * LANE
            qctx_ref[sl, a:b] = qn[:, hh * LANE:(hh + 1) * LANE]
            qctx_ref[sl, b:b + LANE] = q_pe[:, pr:pr + LANE]
            if rope:
                qlat_ref[sl, a:b] = qn[:, hh * LANE:(hh + 1) * LANE]
                qlat_ref[sl, b:b + LANE] = q_pe_rot[:, pr:pr + LANE]
            kcat_ref[sl, a:b] = kn[:, hh * LANE:(hh + 1) * LANE]
            kcat_ref[sl, b:b + LANE] = kpe_lo if hh % 2 == 0 else kpe_hi


def _mla_pre(grp, x, mod, w, rope_tabs):
    n = grp.n
    ins = [x, mod] + list(w)
    specs = [_rows(TM, D), _mod_spec(grp)] + [_const(a.shape) for a in w]
    wide = jax.ShapeDtypeStruct((n, 2 * D), bf16)
    if rope_tabs is not None:
        ins += list(rope_tabs)
        specs += [_tab_spec(), _tab_spec()]
        shapes = (wide, wide, wide, jax.ShapeDtypeStruct((n, D), bf16))
        ospecs = (_rows(TM, 2 * D), _rows(TM, 2 * D), _rows(TM, 2 * D), _rows(TM, D))
    else:
        shapes = (wide, wide, jax.ShapeDtypeStruct((n, D), bf16),
                  jax.ShapeDtypeStruct((n, MLA_KV_LORA), f32), jax.ShapeDtypeStruct((n, MLA_ROPE), f32))
        ospecs = (_rows(TM, 2 * D), _rows(TM, 2 * D), _rows(TM, D), _rows(TM, MLA_KV_LORA), _rows(TM, MLA_ROPE))
    return pl.pallas_call(
        functools.partial(_mla_pre_kernel, rope_tabs is not None),
        grid=(n // TM,), in_specs=specs, out_specs=ospecs, out_shape=shapes,
        compiler_params=_cp(48), name="mla_pre",
    )(*ins)


def _mla_ctx_kernel(ckv_ref, kpe2_ref, wkvb_ref, kcat_ref, v_ref):
    kvb = _dot(ckv_ref[...].astype(bf16), wkvb_ref[...])
    kn = kvb[:, :MLA_HEADS * MLA_NOPE].astype(bf16)
    v_ref[...] = kvb[:, MLA_HEADS * MLA_NOPE:].astype(bf16)
    kpe2 = kpe2_ref[...]
    lane = lax.broadcasted_iota(i32, kpe2.shape, 1)
    kpe_lo = jnp.where(lane < MLA_ROPE, kpe2, 0.0).astype(bf16)
    kpe_hi = jnp.where(lane >= MLA_ROPE, kpe2, 0.0).astype(bf16)
    for hh in range(MLA_HEADS):
        a, b = hh * 256, hh * 256 + LANE
        kcat_ref[:, a:b] = kn[:, hh * LANE:(hh + 1) * LANE]
        kcat_ref[:, b:b + LANE] = kpe_lo if hh % 2 == 0 else kpe_hi


def _mla_ctx(ckv, kpe2, wkvb):
    n = ckv.shape[0]
    return pl.pallas_call(
        _mla_ctx_kernel, grid=(n // PAST,),
        in_specs=[_rows(PAST, MLA_KV_LORA), _rows(PAST, LANE), _const(wkvb.shape)],
        out_specs=(_rows(PAST, 2 * D), _rows(PAST, D)),
        out_shape=(jax.ShapeDtypeStruct((n, 2 * D), bf16), jax.ShapeDtypeStruct((n, D), bf16)),
        compiler_params=_cp(24), name="mla_ctx",
    )(ckv, kpe2, wkvb)


def _mla_attn_p_kernel(q_ref, k_ref, v_ref, o_ref):
    def scores(hh):
        return _dot_nt(q_ref[:, hh * 256:(hh + 1) * 256], k_ref[:, hh * 256:(hh + 1) * 256])

    nxt = scores(0)
    for hh in range(MLA_HEADS):
        s = nxt
        if hh + 1 < MLA_HEADS:
            nxt = scores(hh + 1)
        e = jnp.exp2((s - jnp.max(s, axis=-1, keepdims=True)) * MLA_C2)
        l = jnp.sum(e, axis=-1, keepdims=True)
        o = _dot(e.astype(bf16), v_ref[:, hh * LANE:(hh + 1) * LANE]) / l
        o_ref[:, hh * LANE:(hh + 1) * LANE] = o.astype(bf16)


def _mla_attn_p(q, k, v):
    n = q.shape[0]
    return pl.pallas_call(
        _mla_attn_p_kernel, grid=(n // SEQ,),
        in_specs=[_rows(SEQ, 2 * D), _rows(SEQ, 2 * D), _rows(SEQ, D)],
        out_specs=_rows(SEQ, D), out_shape=jax.ShapeDtypeStruct((n, D), bf16),
        compiler_params=_cp(24), name="mla_attn_ctx",
    )(q, k, v)


MLA_QT = 256


def _mla_attn_s_kernel(ql_ref, qc_ref, kl_ref, vl_ref, kc_ref, vc_ref, o_ref):
    def scores(hh):
        a, b = hh * 256, (hh + 1) * 256
        return (_dot_nt(ql_ref[:, a:b], kl_ref[:, a:b]), _dot_nt(qc_ref[:, a:b], kc_ref[:, a:b]))

    nxt = scores(0)
    for hh in range(MLA_HEADS):
        s1, s2 = nxt
        if hh + 1 < MLA_HEADS:
            nxt = scores(hh + 1)
        m = jnp.maximum(jnp.max(s1, axis=-1, keepdims=True), jnp.max(s2, axis=-1, keepdims=True))
        e1, e2 = jnp.exp2((s1 - m) * MLA_C2), jnp.exp2((s2 - m) * MLA_C2)
        l = jnp.sum(e1, axis=-1, keepdims=True) + jnp.sum(e2, axis=-1, keepdims=True)
        o = (_dot(e1.astype(bf16), vl_ref[:, hh * LANE:(hh + 1) * LANE])
             + _dot(e2.astype(bf16), vc_ref[:, hh * LANE:(hh + 1) * LANE])) / l
        o_ref[:, hh * LANE:(hh + 1) * LANE] = o.astype(bf16)


def _mla_attn_s(ql, qc, kl, vl, kc, vc):
    nq = DEC_SEQ // MLA_QT
    qs = pl.BlockSpec((MLA_QT, 2 * D), lambda b, i: (b * nq + i, 0))
    return pl.pallas_call(
        _mla_attn_s_kernel, grid=(DEC_BATCH, nq),
        in_specs=[qs, qs,
                  pl.BlockSpec((DEC_SEQ, 2 * D), lambda b, i: (b, 0)),
                  pl.BlockSpec((DEC_SEQ, D), lambda b, i: (b, 0)),
                  pl.BlockSpec((PAST, 2 * D), lambda b, i: (b, 0)),
                  pl.BlockSpec((PAST, D), lambda b, i: (b, 0))],
        out_specs=pl.BlockSpec((MLA_QT, D), lambda b, i: (b * nq + i, 0)),
        out_shape=jax.ShapeDtypeStruct((NS_TOK, D), bf16),
        compiler_params=_cp(48, 2), name="mla_attn_lat",
    )(ql, qc, kl, vl, kc, vc)


def _gelu_tanh(x):
    return 0.5 * x * (1.0 + jnp.tanh(math.sqrt(2.0 / math.pi) * (x + 0.044715 * (x * x * x))))


GM_CW = GM_HALF // GM_GROUPS


def _gmlp_kernel(x_ref, mod_ref, lnp_ref, win_ref, bin_ref, vg_ref, vb_ref, ws_ref, bs_ref, wout_ref, bout_ref,
                 rwt_ref, x1_ref, h2_ref, aff_ref, gated_ref, vz_ref):
    m, lnp = mod_ref[...], lnp_ref[...]
    for s, sl in enumerate(_subs()):
        x = x_ref[sl, :]
        h = (x * (1.0 + m[1:2]) + m[0:1]).astype(bf16)

        def mm(j):
            return _dot(h, win_ref[:, j * GM_CW:(j + 1) * GM_CW]) + bin_ref[:, j * GM_CW:(j + 1) * GM_CW]

        s1 = jnp.zeros((SUB, 1), f32)
        s2 = jnp.zeros((SUB, 1), f32)
        nxt = mm(GM_GROUPS)
        for g in range(GM_GROUPS):
            cur = nxt
            nxt = mm(GM_GROUPS + g + 1) if g + 1 < GM_GROUPS else mm(0)
            z = _gelu_tanh(cur)
            s1 = s1 + jnp.sum(z, axis=-1, keepdims=True)
            s2 = s2 + jnp.sum(z * z, axis=-1, keepdims=True)
            vz_ref[s, :, g * GM_CW:(g + 1) * GM_CW] = z
        mu = s1 * (1.0 / GM_HALF)
        rstd = lax.rsqrt(s2 * (1.0 / GM_HALF) - mu * mu + LN_EPS)
        for g in range(GM_GROUPS):
            cur = nxt
            if g + 1 < GM_GROUPS:
                nxt = mm(g + 1)
            cols = slice(g * GM_CW, (g + 1) * GM_CW)
            u = _gelu_tanh(cur)
            v = ((vz_ref[s, :, cols] - mu) * rstd * vg_ref[:, cols] + vb_ref[:, cols]).astype(bf16)
            for c in range(SUB // GM_CHUNK):
                r0, r1 = c * GM_CHUNK, (c + 1) * GM_CHUNK
                sv = _dot(ws_ref[g], v[r0:r1, :]) + bs_ref[:, g:g + 1]
                gated_ref[sl.start + r0:sl.start + r1, cols] = (u[r0:r1, :] * sv).astype(bf16)
        d = _dot(gated_ref[sl, :], wout_ref[...]) + bout_ref[...]
        _close(d, x, m, lnp, rwt_ref, x1_ref, h2_ref, aff_ref, sl)


def _gmlp_layer(grp, x, mod, lnp, w, rwt):
    shapes, specs = _close_outs(grp.n)
    return pl.pallas_call(
        _gmlp_kernel, grid=(grp.n // TM,),
        in_specs=[_rows(TM, D), _mod_spec(grp), _const((8, D))] + [_const(a.shape) for a in w]
        + [_const((D, LANE))],
        out_specs=specs, out_shape=shapes,
        scratch_shapes=[pltpu.VMEM((TM, GM_HALF), bf16), pltpu.VMEM((TM // SUB, SUB, GM_HALF), f32)],
        compiler_params=_cp(56), name="gmlp_layer",
    )(x, mod, lnp, *w, rwt)


HALO = 16
CONV_RB = 64
CONV_LW = 256


def _conv_glu_kernel(x_ref, mod_ref, w_ref, b_ref, a_ref):
    m = mod_ref[...]

    def front(sl):
        h = (x_ref[sl, :] * (1.0 + m[1:2]) + m[0:1]).astype(bf16)
        return _dot(h, w_ref[...])

    subs = _subs()
    nxt = front(subs[0])
    for s, sl in enumerate(subs):
        a = nxt + b_ref[...]
        if s + 1 < len(subs):
            nxt = front(subs[s + 1])
        a_ref[sl, :] = a[:, :D] * jax.nn.sigmoid(a[:, D:])


def _conv_glu(grp, x, mod, w, b):
    return pl.pallas_call(
        _conv_glu_kernel, grid=(grp.n // TM,),
        in_specs=[_rows(TM, D), _mod_spec(grp), _const(w.shape), _const(b.shape)],
        out_specs=_rows(TM, D), out_shape=jax.ShapeDtypeStruct((grp.n, D), f32),
        compiler_params=_cp(40), name="conv_glu",
    )(x, mod, w, b)


def _conv_close_kernel(seq_subs, ap_ref, a_ref, an_ref, x_ref, mod_ref, lnp_ref, wdw_ref, bdw_ref, ng_ref, nb_ref,
                       w2_ref, b2_ref, rwt_ref, x1_ref, h2_ref, aff_ref, pad_ref, act_ref, cout_ref, shift_ref):
    i = pl.program_id(0)
    nsub = TM // SUB
    bdw, ng, nb = bdw_ref[...], ng_ref[...], nb_ref[...]
    m, lnp = mod_ref[...], lnp_ref[...]
    for s, sl in enumerate(_subs()):
        gsub = i * nsub + s
        prev = a_ref[sl.start - HALO:sl.start, :] if s > 0 else ap_ref[...]
        nxt = a_ref[sl.stop:sl.stop + HALO, :] if s < nsub - 1 else an_ref[...]
        pad = pad_ref.at[s]
        pad[0:HALO, :] = jnp.where((gsub % seq_subs) != 0, prev, 0.0)
        pad[HALO:HALO + SUB, :] = a_ref[sl, :]
        pad[HALO + SUB:, :] = jnp.where((gsub % seq_subs) != seq_subs - 1, nxt, 0.0)
        span = SUB + 24
        for r in range(1, 8):
            shift_ref[s, r - 1, :, :] = pad[r:r + span, :]
        for rb in range(SUB // CONV_RB):
            r0 = rb * CONV_RB
            for lc in range(D // CONV_LW):
                lanes = slice(lc * CONV_LW, (lc + 1) * CONV_LW)
                acc = jnp.zeros((CONV_RB, CONV_LW), f32)
                for k in range(CONV_W):
                    mm, r = (k + 1) // 8, (k + 1) % 8
                    rows = slice(r0 + 8 * mm, r0 + 8 * mm + CONV_RB)
                    win = pad[rows, lanes] if r == 0 else shift_ref[s, r - 1, rows, lanes]
                    acc = acc + wdw_ref[k:k + 1, lanes] * win
                cout_ref[s, r0:r0 + CONV_RB, lanes] = acc
            y = _silu(_ln_rows(cout_ref[s, r0:r0 + CONV_RB, :] + bdw, ng, nb))
            act_ref[sl.start + r0:sl.start + r0 + CONV_RB, :] = y.astype(bf16)
        d = _dot(act_ref[sl, :], w2_ref[...]) + b2_ref[...]
        _close(d, x_ref[sl, :], m, lnp, rwt_ref, x1_ref, h2_ref, aff_ref, sl)


def _conv_close(grp, a, x, mod, lnp, w, rwt):
    n = grp.n
    hb = TM // HALO
    last = n // HALO - 1
    shapes, specs = _close_outs(n)
    return pl.pallas_call(
        functools.partial(_conv_close_kernel, grp.seq // SUB), grid=(n // TM,),
        in_specs=[pl.BlockSpec((HALO, D), lambda i: (jnp.maximum(i * hb - 1, 0), 0)),
                  _rows(TM, D),
                  pl.BlockSpec((HALO, D), lambda i: (jnp.minimum((i + 1) * hb, last), 0)),
                  _rows(TM, D), _mod_spec(grp), _const((8, D))]
        + [_const(t.shape) for t in w] + [_const((D, LANE))],
        out_specs=specs, out_shape=shapes,
        scratch_shapes=[pltpu.VMEM((TM // SUB, SUB + 2 * HALO, D), f32), pltpu.VMEM((TM, D), bf16),
                        pltpu.VMEM((TM // SUB, SUB, D), f32), pltpu.VMEM((TM // SUB, 7, SUB + 24, D), f32)],
        compiler_params=_cp(48), name="conv_close",
    )(a, a, a, x, mod, lnp, *w, rwt)


def _swa_pre_kernel(rope, x_ref, mod_ref, wq_ref, wk_ref, wv_ref, *rest):
    if rope:
        cos_ref, sin_ref, qrot_ref, qraw_ref, kd_ref, vd_ref = rest
    else:
        qraw_ref, kd_ref, vd_ref, k_ref, v_ref = rest
    m = mod_ref[...]

    def front(sl):
        h = (x_ref[sl, :] * (1.0 + m[1:2]) + m[0:1]).astype(bf16)
        return _dot(h, wq_ref[...]), _dot(h, wk_ref[...]), _dot(h, wv_ref[...])

    subs = _subs()
    nxt = front(subs[0])
    for s, sl in enumerate(subs):
        q, kd, vd = nxt
        if s + 1 < len(subs):
            nxt = front(subs[s + 1])
        qraw_ref[sl, :] = q.astype(bf16)
        vd_ref[sl, :] = vd.astype(bf16)
        if rope:
            cos, sin = cos_ref[sl, :], sin_ref[sl, :]
            qrot_ref[sl, :] = _rope(q, cos, sin).astype(bf16)
            kd_ref[sl, :] = _rope(kd, cos, sin).astype(bf16)
        else:
            kd_ref[sl, :] = kd.astype(bf16)
            lane = lax.broadcasted_iota(i32, (SUB, LANE), 1)
            for j in range(SWA_KV // 2):
                lo, hi = 2 * j * LANE, (2 * j + 1) * LANE
                k_ref[sl, j * LANE:(j + 1) * LANE] = jnp.where(lane < SWA_HD, kd[:, lo:lo + LANE], kd[:, hi:hi + LANE])
                v_ref[sl, j * LANE:(j + 1) * LANE] = jnp.where(lane < SWA_HD, vd[:, lo:lo + LANE], vd[:, hi:hi + LANE])


def _swa_pre(grp, x, mod, w, rope_tabs):
    n = grp.n
    ins = [x, mod] + list(w)
    specs = [_rows(TM, D), _mod_spec(grp)] + [_const(a.shape) for a in w]
    kw = 2 * SWA_KV * SWA_HD
    qs, ks = jax.ShapeDtypeStruct((n, D), bf16), jax.ShapeDtypeStruct((n, kw), bf16)
    if rope_tabs is not None:
        ins += list(rope_tabs)
        specs += [_tab_spec(), _tab_spec()]
        shapes = (qs, qs, ks, ks)
        ospecs = (_rows(TM, D), _rows(TM, D), _rows(TM, kw), _rows(TM, kw))
    else:
        nat = jax.ShapeDtypeStruct((n, SWA_KV * SWA_HD), f32)
        shapes = (qs, ks, ks, nat, nat)
        ospecs = (_rows(TM, D), _rows(TM, kw), _rows(TM, kw), _rows(TM, SWA_KV * SWA_HD), _rows(TM, SWA_KV * SWA_HD))
    return pl.pallas_call(
        functools.partial(_swa_pre_kernel, rope_tabs is not None),
        grid=(n // TM,), in_specs=specs, out_specs=ospecs, out_shape=shapes,
        compiler_params=_cp(40), name="swa_pre",
    )(*ins)


def _half_mask(x, parity):
    lane = lax.broadcasted_iota(i32, x.shape, 1)
    keep = (lane < SWA_HD) if parity == 0 else (lane >= SWA_HD)
    return jnp.where(keep, x, jnp.zeros_like(x))


def _swa_attn_p_kernel(sink_ref, q_ref, kd_ref, vd_ref, o_ref):
    def scores(hd):
        pair, par = hd // 2, hd % 2
        g = hd // (SWA_HEADS // SWA_KV)
        return _dot_nt(q_ref[:, pair * LANE:(pair + 1) * LANE], _half_mask(kd_ref[:, g * LANE:(g + 1) * LANE], par))

    nxt = scores(0)
    acc = None
    for hd in range(SWA_HEADS):
        pair, par = hd // 2, hd % 2
        g = hd // (SWA_HEADS // SWA_KV)
        s = nxt
        if hd + 1 < SWA_HEADS:
            nxt = scores(hd + 1)
        sk = sink_ref[hd] * (1.0 / SWA_SCALE)
        m = jnp.maximum(jnp.max(s, axis=-1, keepdims=True), sk)
        e = jnp.exp2((s - m) * SWA_C2)
        l = jnp.sum(e, axis=-1, keepdims=True) + jnp.exp2((sk - m) * SWA_C2)
        o = _dot(e.astype(bf16), _half_mask(vd_ref[:, g * LANE:(g + 1) * LANE], par)) / l
        acc = o if par == 0 else acc + o
        if par == 1:
            o_ref[:, pair * LANE:(pair + 1) * LANE] = acc.astype(bf16)


def _swa_attn_p(sink, q, kd, vd):
    n = q.shape[0]
    kw = kd.shape[1]
    return pl.pallas_call(
        _swa_attn_p_kernel,
        grid_spec=pltpu.PrefetchScalarGridSpec(
            num_scalar_prefetch=1, grid=(n // SEQ,),
            in_specs=[_rows(SEQ, D), _rows(SEQ, kw), _rows(SEQ, kw)],
            out_specs=_rows(SEQ, D)),
        out_shape=jax.ShapeDtypeStruct((n, D), bf16),
        compiler_params=_cp(24), name="swa_attn_ctx",
    )(sink, q, kd, vd)


SWA_QB = 128


def _swa_attn_s_kernel(sink_ref, qr_ref, qw_ref, kd_ref, vd_ref, kc_ref, vc_ref, o_ref):
    nblk = pl.program_id(1)
    span = 3 * SWA_QB
    start = pl.multiple_of(jnp.clip((nblk - 1) * SWA_QB, 0, DEC_SEQ - span), SWA_QB)
    qpos = nblk * SWA_QB + lax.broadcasted_iota(i32, (SWA_QB, span), 0)
    kpos = start + lax.broadcasted_iota(i32, (SWA_QB, span), 1)
    band = jnp.abs(kpos - qpos) <= SWA_WIN

    def scores(hd):
        pair, par = hd // 2, hd % 2
        g = hd // (SWA_HEADS // SWA_KV)
        cols = slice(g * LANE, (g + 1) * LANE)
        s1 = _dot_nt(qr_ref[:, pair * LANE:(pair + 1) * LANE], _half_mask(kd_ref[pl.ds(start, span), cols], par))
        s2 = _dot_nt(qw_ref[:, pair * LANE:(pair + 1) * LANE], _half_mask(kc_ref[:, cols], par))
        return s1, s2

    nxt = scores(0)
    acc = None
    for hd in range(SWA_HEADS):
        pair, par = hd // 2, hd % 2
        g = hd // (SWA_HEADS // SWA_KV)
        cols = slice(g * LANE, (g + 1) * LANE)
        s1, s2 = nxt
        if hd + 1 < SWA_HEADS:
            nxt = scores(hd + 1)
        s1 = jnp.where(band, s1, NEG_INF)
        sk = sink_ref[hd] * (1.0 / SWA_SCALE)
        m = jnp.maximum(jnp.maximum(jnp.max(s1, axis=-1, keepdims=True), jnp.max(s2, axis=-1, keepdims=True)), sk)
        e1, e2 = jnp.exp2((s1 - m) * SWA_C2), jnp.exp2((s2 - m) * SWA_C2)
        l = (jnp.sum(e1, axis=-1, keepdims=True) + jnp.sum(e2, axis=-1, keepdims=True)
             + jnp.exp2((sk - m) * SWA_C2))
        o = (_dot(e1.astype(bf16), _half_mask(vd_ref[pl.ds(start, span), cols], par))
             + _dot(e2.astype(bf16), _half_mask(vc_ref[:, cols], par))) / l
        acc = o if par == 0 else acc + o
        if par == 1:
            o_ref[:, pair * LANE:(pair + 1) * LANE] = acc.astype(bf16)


def _swa_attn_s(sink, qr, qw, kd, vd, kc, vc):
    nq = DEC_SEQ // SWA_QB
    kw = kd.shape[1]
    qs = pl.BlockSpec((SWA_QB, D), lambda b, i, *_: (b * nq + i, 0))
    full = pl.BlockSpec((DEC_SEQ, kw), lambda b, i, *_: (b, 0))
    ctx = pl.BlockSpec((PAST, kw), lambda b, i, *_: (b, 0))
    return pl.pallas_call(
        _swa_attn_s_kernel,
        grid_spec=pltpu.PrefetchScalarGridSpec(
            num_scalar_prefetch=1, grid=(DEC_BATCH, nq),
            in_specs=[qs, qs, full, full, ctx, ctx],
            out_specs=pl.BlockSpec((SWA_QB, D), lambda b, i, *_: (b * nq + i, 0))),
        out_shape=jax.ShapeDtypeStruct((NS_TOK, D), bf16),
        compiler_params=_cp(32, 2), name="swa_attn_lat",
    )(sink, qr, qw, kd, vd, kc, vc)


def _excl_prefix(mask_f, nb, tri, blk):
    m2 = mask_f.reshape(nb * NE, RT)
    within = _dot(m2.astype(bf16), tri)
    tot = jnp.sum(m2, axis=1, keepdims=True)
    totb = jnp.broadcast_to(tot, (nb * NE, LANE)).astype(bf16)
    offs = _dot(blk, totb)
    return (within + offs[:, 0:1]).reshape(nb, NE, RT), offs.reshape(nb, NE, LANE)


def _route_kernel(cap, nb, aff_ref, dest_ref, gate_ref, off_ref):
    a = aff_ref[...]

    def as_f32(bits):
        return pltpu.bitcast(bits, f32)[None]

    def count_ge(th):
        c = jnp.sum(jnp.where(a >= as_f32(th), 1.0, 0.0), axis=0)
        return jnp.sum(c, axis=1, keepdims=True)

    def body(_, c):
        lo, hi = c
        mid = lo + ((hi - lo + 1) >> 1)
        ok = count_ge(mid) >= cap
        return jnp.where(ok, mid, lo), jnp.where(ok, hi, mid - 1)

    lo0 = jnp.zeros((NE, 1), i32)
    hi0 = jnp.full((NE, 1), 0x7F800000, i32)
    thr, _ = lax.fori_loop(0, 31, body, (lo0, hi0))

    r = lax.broadcasted_iota(i32, (RT, RT), 0)
    c = lax.broadcasted_iota(i32, (RT, RT), 1)
    tri = jnp.where(r < c, 1.0, 0.0).astype(bf16)
    rr = lax.broadcasted_iota(i32, (nb * NE, nb * NE), 0)
    cc = lax.broadcasted_iota(i32, (nb * NE, nb * NE), 1)
    blk = jnp.where(((rr % NE) == (cc % NE)) & (cc < rr), 1.0, 0.0).astype(bf16)

    gt = a > as_f32(thr)
    eq = a == as_f32(thr)
    n_gt = jnp.sum(jnp.sum(jnp.where(gt, 1.0, 0.0), axis=0), axis=1, keepdims=True)
    need = cap - n_gt
    tie_rank, _ = _excl_prefix(jnp.where(eq, 1.0, 0.0), nb, tri, blk)
    sel = gt | (eq & (tie_rank < need[None]))
    pos, offs = _excl_prefix(jnp.where(sel, 1.0, 0.0), nb, tri, blk)
    dest_ref[...] = jnp.where(sel, pos, -1.0).astype(i32)
    gate_ref[...] = jnp.where(sel, a, 0.0)
    off_ref[...] = offs.astype(i32)


def _route(grp, aff3):
    nb = grp.n // RT
    full = pl.BlockSpec((nb, NE, RT), lambda: (0, 0, 0))
    offspec = pl.BlockSpec((nb, NE, LANE), lambda: (0, 0, 0))
    return pl.pallas_call(
        functools.partial(_route_kernel, grp.cap, nb),
        in_specs=[full], out_specs=(full, full, offspec),
        out_shape=(jax.ShapeDtypeStruct((nb, NE, RT), i32), jax.ShapeDtypeStruct((nb, NE, RT), f32),
                   jax.ShapeDtypeStruct((nb, NE, LANE), i32)),
        compiler_params=pltpu.CompilerParams(vmem_limit_bytes=40 * MIB), name="route",
    )(aff3)


def _route_tables(grp, dest3, off3):
    ct = off3[:, :, 0].T
    ct_end = jnp.concatenate([ct[:, 1:], jnp.full((NE, 1), grp.cap, i32)], axis=1)
    wstart = (ct // 16) * 16
    nq = jnp.max((ct_end - wstart + WIN - 1) // WIN, axis=0)
    destp = jnp.transpose(dest3, (0, 2, 1)).reshape(grp.n, NE) + 1
    destab = jnp.concatenate([destp // 32, destp % 32], axis=1).astype(bf16)
    return wstart.astype(i32), nq.astype(i32), destab


def _dispatch_kernel(cap, eg, ws_ref, nq_ref, h2_ref, dest_ref, gate_ref, xe_ref, gcol_ref, hot_ref):
    g, step = pl.program_id(0), pl.program_id(1)

    @pl.when(step == 0)
    def _():
        xe_ref[...] = jnp.zeros_like(xe_ref)
        gcol_ref[...] = jnp.zeros_like(gcol_ref)

    wrow = lax.broadcasted_iota(i32, (WIN, RT), 0)
    for b in range(BLOCKS_PER_STEP):
        tb = step * BLOCKS_PER_STEP + b
        hot = hot_ref.at[b]

        def body(q, carry, tb=tb, b=b, hot=hot):
            starts, gsums = [], []
            for el in range(eg):
                e = g * eg + el
                want = ws_ref[e, tb] + q * WIN
                st = pl.multiple_of(jnp.minimum(want, cap - WIN), 16)
                row = st + wrow
                hit = dest_ref[b, pl.ds(e, 1), :] == jnp.where(row >= want, row, -7)
                hot[el * WIN:(el + 1) * WIN, :] = jnp.where(hit, 1.0, 0.0).astype(bf16)
                gsums.append(jnp.sum(jnp.where(hit, gate_ref[b, pl.ds(e, 1), :], 0.0), axis=1, keepdims=True))
                starts.append(st)
            part = _dot(hot[...], h2_ref[b * RT:(b + 1) * RT, :])
            for el in range(eg):
                dst = pl.ds(el * cap + starts[el], WIN)
                xe_ref[dst, :] = xe_ref[dst, :] + part[el * WIN:(el + 1) * WIN, :].astype(bf16)
                gcol_ref[dst, :] = gcol_ref[dst, :] + jnp.broadcast_to(gsums[el], (WIN, LANE))
            return carry

        lax.fori_loop(0, nq_ref[tb], body, 0)


def _dispatch(grp, wstart, nq, h2, dest3, gate3):
    eg = ROWS_PER_PASS // grp.cap
    nb = grp.n // (RT * BLOCKS_PER_STEP)
    tab = pl.BlockSpec((BLOCKS_PER_STEP, NE, RT), lambda g, t, *_: (t, 0, 0))
    return pl.pallas_call(
        functools.partial(_dispatch_kernel, grp.cap, eg),
        grid_spec=pltpu.PrefetchScalarGridSpec(
            num_scalar_prefetch=2, grid=(NE // eg, nb),
            in_specs=[pl.BlockSpec((RT * BLOCKS_PER_STEP, D), lambda g, t, *_: (t, 0)), tab, tab],
            out_specs=(pl.BlockSpec((ROWS_PER_PASS, D), lambda g, t, *_: (g, 0)),
                       pl.BlockSpec((ROWS_PER_PASS, LANE), lambda g, t, *_: (g, 0))),
            scratch_shapes=[pltpu.VMEM((BLOCKS_PER_STEP, eg * WIN, RT), bf16)]),
        out_shape=(jax.ShapeDtypeStruct((NE * grp.cap, D), bf16), jax.ShapeDtypeStruct((NE * grp.cap, LANE), f32)),
        compiler_params=_cp(56, 2), name="moe_dispatch",
    )(wstart, nq, h2, dest3, gate3)


FFN_RB = 256


def _ffn_kernel(xp_ref, gp_ref, xs_ref, gs_ref, wg_ref, wu_ref, wd_ref, yp_ref, ys_ref, wgb, wub, wdb):
    wgb[...] = wg_ref[...].astype(bf16)
    wub[...] = wu_ref[...].astype(bf16)
    wdb[...] = wd_ref[...].astype(bf16)
    for x_ref, g_ref, y_ref, cap in ((xp_ref, gp_ref, yp_ref, CTX.cap), (xs_ref, gs_ref, ys_ref, LAT.cap)):
        for r in range(cap // FFN_RB):
            sl = slice(r * FFN_RB, (r + 1) * FFN_RB)
            x = x_ref[sl, :]
            hid = (_silu(_dot(x, wgb[...])) * _dot(x, wub[...])).astype(bf16)
            gate = jnp.concatenate([g_ref[sl, :]] * (D // LANE), axis=1)
            y_ref[sl, :] = (_dot(hid, wdb[...]) * gate).astype(bf16)


def _ffn(layer, xp, gp, xs, gs, wg, wu, wd):
    wspec = pl.BlockSpec((None, None, D, FF), lambda e: (layer, e, 0, 0))
    cp, cs = CTX.cap, LAT.cap
    return pl.pallas_call(
        _ffn_kernel, grid=(NE,),
        in_specs=[pl.BlockSpec((cp, D), lambda e: (e, 0)), pl.BlockSpec((cp, LANE), lambda e: (e, 0)),
                  pl.BlockSpec((cs, D), lambda e: (e, 0)), pl.BlockSpec((cs, LANE), lambda e: (e, 0)),
                  wspec, wspec, pl.BlockSpec((None, None, FF, D), lambda e: (layer, e, 0, 0))],
        out_specs=(pl.BlockSpec((cp, D), lambda e: (e, 0)), pl.BlockSpec((cs, D), lambda e: (e, 0))),
        out_shape=(jax.ShapeDtypeStruct((NE * cp, D), bf16), jax.ShapeDtypeStruct((NE * cs, D), bf16)),
        scratch_shapes=[pltpu.VMEM((D, FF), bf16), pltpu.VMEM((D, FF), bf16), pltpu.VMEM((FF, D), bf16)],
        compiler_params=_cp(56), name="moe_ffn",
    )(xp, gp, xs, gs, wg, wu, wd)


def _combine_kernel(cap, ws_ref, nq_ref, yg_ref, dab_ref, x1_ref, mod_ref, lnp_ref, x2_ref, win_ref, y_ref):
    step = pl.program_id(0)
    kw = NE * WIN
    lane = lax.broadcasted_iota(i32, (1, kw), 1)
    lane_e, lane_w = lane // WIN, lane % WIN
    er = lax.broadcasted_iota(i32, (2 * NE, kw), 0)
    ec = lax.broadcasted_iota(i32, (2 * NE, kw), 1) // WIN
    spread = jnp.where(er == ec, 32.0, jnp.where(er - NE == ec, 1.0, 0.0)).astype(bf16)

    def chunk(b, tb, q, destp):
        win = win_ref.at[b]
        tgt = jnp.full((1, kw), -1, i32)
        for e in range(NE):
            want = ws_ref[e, tb] + q * WIN
            st = pl.multiple_of(jnp.minimum(want, cap - WIN), 16)
            win[e * WIN:(e + 1) * WIN, :] = yg_ref[pl.ds(e * cap + st, WIN), :]
            row = st + lane_w
            tgt = jnp.where(lane_e == e, jnp.where(row >= want, row + 1, -1), tgt)
        onehot = jnp.where(destp == tgt.astype(f32), 1.0, 0.0).astype(bf16)
        return _dot(onehot, win[...])

    blocks = [(b, step * BLOCKS_PER_STEP + b, slice(b * RT, (b + 1) * RT)) for b in range(BLOCKS_PER_STEP)]
    destps = []
    for b, tb, rows in blocks:
        destps.append(_dot(dab_ref[rows, :], spread))
        y_ref[b] = chunk(b, tb, 0, destps[b])
    for b, tb, rows in blocks:
        @pl.when(nq_ref[tb] > 1)
        def _(b=b, tb=tb, rows=rows):
            destp = _dot(dab_ref[rows, :], spread)

            def body(q, carry):
                y_ref[b] = y_ref[b] + chunk(b, tb, q, destp)
                return carry

            lax.fori_loop(1, nq_ref[tb], body, 0)
    m, lnp = mod_ref[...], lnp_ref[...]
    for b, tb, rows in blocks:
        x2_ref[rows, :] = _ln_rows(ALPHA * x1_ref[rows, :] + m[5:6] * y_ref[b], lnp[2:3], lnp[3:4])


def _combine(grp, wstart, nq, yg, destab, x1, mod, lnp):
    rt = RT * BLOCKS_PER_STEP
    seg = pl.BlockSpec((None, 8, D), lambda i, *_: (grp.seg0 + (i * rt) // grp.seq if grp.seq > rt else grp.seg0, 0, 0))
    return pl.pallas_call(
        functools.partial(_combine_kernel, grp.cap),
        grid_spec=pltpu.PrefetchScalarGridSpec(
            num_scalar_prefetch=2, grid=(grp.n // rt,),
            in_specs=[_const(yg.shape), _rows(rt, 2 * NE), _rows(rt, D), seg, _const((8, D))],
            out_specs=_rows(rt, D),
            scratch_shapes=[pltpu.VMEM((BLOCKS_PER_STEP, NE * WIN, D), bf16), pltpu.VMEM((BLOCKS_PER_STEP, RT, D), f32)]),
        out_shape=jax.ShapeDtypeStruct((grp.n, D), f32),
        compiler_params=_cp(52), name="moe_combine",
    )(wstart, nq, yg, destab, x1, mod, lnp)


def _moe(layer, closed, mod, lnp, wg, wu, wd):
    disp = []
    for grp, (x1, h2, aff3) in zip((CTX, LAT), closed):
        dest3, gate3, off3 = _route(grp, aff3)
        wstart, nq, destab = _route_tables(grp, dest3, off3)
        xe, gcol = _dispatch(grp, wstart, nq, h2, dest3, gate3)
        disp.append((xe, gcol, wstart, nq, destab, x1))
    ys = _ffn(layer, disp[0][0], disp[0][1], disp[1][0], disp[1][1], wg, wu, wd)
    return tuple(_combine(grp, d[2], d[3], y, d[4], d[5], mod, lnp)
                 for grp, d, y in zip((CTX, LAT), disp, ys))


def _mla_weights(wq_a, q_norm, wq_b, wkv_a, kv_norm, wkv_b):
    wqb = wq_b.reshape(MLA_Q_LORA, MLA_HEADS, MLA_NOPE + MLA_ROPE)
    wqb = jnp.concatenate([wqb[:, :, :MLA_NOPE].reshape(MLA_Q_LORA, -1), wqb[:, :, MLA_NOPE:].reshape(MLA_Q_LORA, -1)], axis=1)
    wkva = jnp.concatenate([wkv_a, wkv_a[:, MLA_KV_LORA:]], axis=1)
    wkvb = wkv_b.reshape(MLA_KV_LORA, MLA_HEADS, MLA_NOPE + MLA_V)
    wkvb = jnp.concatenate([wkvb[:, :, :MLA_NOPE].reshape(MLA_KV_LORA, -1), wkvb[:, :, MLA_NOPE:].reshape(MLA_KV_LORA, -1)], axis=1)
    return (wq_a.astype(bf16), q_norm.reshape(1, -1), wqb.astype(bf16), wkva.astype(bf16), kv_norm.reshape(1, -1),
            wkvb.astype(bf16))


def _dup_heads(w):
    w = w.reshape(w.shape[0], SWA_KV, 1, SWA_HD)
    return jnp.broadcast_to(w, (w.shape[0], SWA_KV, 2, SWA_HD)).reshape(w.shape[0], 2 * SWA_KV * SWA_HD)


def kernel(x_prompt, x_sample, c, cache_mla_ckv, cache_mla_kpe, cache_swa_k, cache_swa_v, c_ctx, w_mod, b_mod, ln_gain, ln_bias, router_w, moe_w_gate, moe_w_up, moe_w_down, mla_wq_a, mla_q_norm, mla_wq_b, mla_wkv_a, mla_kv_norm, mla_wkv_b, mla_wo, gm_w_in, gm_b_in, gm_v_norm_g, gm_v_norm_b, gm_w_s, gm_b_s, gm_w_out, gm_b_out, cv_w_pw1, cv_b_pw1, cv_w_dw, cv_b_dw, cv_norm_g, cv_norm_b, cv_w_pw2, cv_b_pw2, swa_wq, swa_wk, swa_wv, swa_sink, swa_wo):
    groups = (CTX, LAT)
    xs = (x_prompt.reshape(NP_TOK, D), x_sample.reshape(NS_TOK, D))
    cond8 = jnp.concatenate([c_ctx[None, :], c, jnp.zeros((5, D), f32)], axis=0)
    mods = _modulation(cond8, w_mod, b_mod)
    lnps = jnp.concatenate([jnp.stack([ln_gain[:, 0], ln_bias[:, 0], ln_gain[:, 1], ln_bias[:, 1]], axis=1),
                            jnp.zeros((DEPTH, 4, D), f32)], axis=1)
    rwts = jnp.pad(router_w, ((0, 0), (0, 0), (0, LANE - NE)))
    tabs = _rope_tables(DEC_SEQ)

    def moe(i, closed):
        return _moe(i, closed, mods[i], lnps[i], moe_w_gate, moe_w_up, moe_w_down)

    wm = _mla_weights(mla_wq_a[0], mla_q_norm[0], mla_wq_b[0], mla_wkv_a[0], mla_kv_norm[0], mla_wkv_b[0])
    qc_p, kc_p, v_p, ckv_p, kpe_p = _mla_pre(CTX, xs[0], mods[0], wm, None)
    ql_s, qc_s, kc_s, v_s = _mla_pre(LAT, xs[1], mods[0], wm, tabs)
    cache_kpe = cache_mla_kpe[:, 0].reshape(DEC_BATCH * PAST, MLA_ROPE)
    kc_c, v_c = _mla_ctx(cache_mla_ckv[:, 0].reshape(DEC_BATCH * PAST, MLA_KV_LORA),
                         jnp.concatenate([cache_kpe, cache_kpe], axis=1), wm[5])
    os_ = (_mla_attn_p(qc_p, kc_p, v_p), _mla_attn_s(ql_s, qc_s, kc_s, v_s, kc_c, v_c))
    wo = mla_wo[0].astype(bf16)
    xs = moe(0, [_proj_close(g, o, x, mods[0], lnps[0], wo, rwts[0]) for g, o, x in zip(groups, os_, xs)])

    wgm = (gm_w_in[0].astype(bf16), gm_b_in[0].reshape(1, -1), gm_v_norm_g[0].reshape(1, -1),
           gm_v_norm_b[0].reshape(1, -1), gm_w_s[0].astype(bf16), gm_b_s[0].T, gm_w_out[0].astype(bf16),
           gm_b_out[0].reshape(1, -1))
    xs = moe(1, [_gmlp_layer(g, x, mods[1], lnps[1], wgm, rwts[1]) for g, x in zip(groups, xs)])

    w1, b1 = cv_w_pw1[0].astype(bf16), cv_b_pw1[0].reshape(1, -1)
    wcv = (cv_w_dw[0], cv_b_dw[0].reshape(1, -1), cv_norm_g[0].reshape(1, -1), cv_norm_b[0].reshape(1, -1),
           cv_w_pw2[0].astype(bf16), cv_b_pw2[0].reshape(1, -1))
    xs = moe(2, [_conv_close(g, _conv_glu(g, x, mods[2], w1, b1), x, mods[2], lnps[2], wcv, rwts[2])
                 for g, x in zip(groups, xs)])

    wsw = (swa_wq[0].astype(bf16), _dup_heads(swa_wk[0]).astype(bf16), _dup_heads(swa_wv[0]).astype(bf16))
    q_p, kd_p, vd_p, k_p, v_p = _swa_pre(CTX, xs[0], mods[3], wsw, None)
    qr_s, qw_s, kd_s, vd_s = _swa_pre(LAT, xs[1], mods[3], wsw, tabs)

    def dup_cache(t):
        return _dup_heads(t[:, 0].reshape(DEC_BATCH * PAST, SWA_KV * SWA_HD)).astype(bf16)

    sink = swa_sink[0]
    os_ = (_swa_attn_p(sink, q_p, kd_p, vd_p),
           _swa_attn_s(sink, qr_s, qw_s, kd_s, vd_s, dup_cache(cache_swa_k), dup_cache(cache_swa_v)))
    wo = swa_wo[0].astype(bf16)
    xs = moe(3, [_proj_close(g, o, x, mods[3], lnps[3], wo, rwts[3]) for g, o, x in zip(groups, os_, xs)])

    return (xs[0].reshape(BATCH, SEQ, D), xs[1].reshape(DEC_BATCH, DEC_SEQ, D),
            ckv_p.reshape(BATCH, 1, SEQ, MLA_KV_LORA), kpe_p.reshape(BATCH, 1, SEQ, MLA_ROPE),
            k_p.reshape(BATCH, 1, SEQ, SWA_KV, SWA_HD), v_p.reshape(BATCH, 1, SEQ, SWA_KV, SWA_HD))
```

```python
import functools
import math

import jax
import jax.numpy as jnp
from jax import lax
from jax.experimental import pallas as pl
from jax.experimental.pallas import tpu as pltpu

f32 = jnp.float32
bf16 = jnp.bfloat16
i32 = jnp.int32

D = 1024
BATCH, SEQ = 32, 256
DEC_BATCH, DEC_SEQ = 2, 2048
PAST = 256
DEPTH = 4
GRID_W = 64
ALPHA = (2 * DEPTH) ** 0.25
LN_EPS = 1e-5
RMS_EPS = 1e-6
ROPE_BASE = 10000.0
NEG_INF = -1e30
MLA_HEADS, MLA_NOPE, MLA_ROPE, MLA_V = 8, 128, 64, 128
MLA_Q_LORA, MLA_KV_LORA = 384, 256
MLA_SCALE = (MLA_NOPE + MLA_ROPE) ** -0.5
MLA_C2 = MLA_SCALE * math.log2(math.e)
GM_CHUNK, GM_HALF, GM_GROUPS = 128, 2048, 4
CONV_W = 31
SWA_HEADS, SWA_KV, SWA_HD, SWA_WIN = 16, 4, 64, 128
SWA_SCALE = SWA_HD ** -0.5
SWA_C2 = SWA_SCALE * math.log2(math.e)
NE = 16
FF = 1024

NP_TOK = BATCH * SEQ
NS_TOK = DEC_BATCH * DEC_SEQ

LANE = 128
TM = 512
SUB = 256
RT = 256
WIN = 64
BLOCKS_PER_STEP = 2
DISPATCH_BLOCKS = 4
ROWS_PER_PASS = 8192
MIB = 2 ** 20


class Group:
    def __init__(self, n_tok, seq, seg0):
        self.n = n_tok
        self.seq = seq
        self.cap = 2 * n_tok // NE
        self.seg0 = seg0

    def seg_of(self, i):
        return self.seg0 + (i * TM) // self.seq if self.seq > TM else self.seg0


CTX = Group(NP_TOK, SEQ, 0)
LAT = Group(NS_TOK, DEC_SEQ, 1)


def _cp(vmem_mb, n_axes=1):
    return pltpu.CompilerParams(dimension_semantics=("arbitrary",) * n_axes,
                                vmem_limit_bytes=int(vmem_mb * MIB))


def _const(shape):
    nd = len(shape)
    return pl.BlockSpec(shape, lambda *_: (0,) * nd, pipeline_mode=pl.Buffered(1))


def _rows(tm, c):
    return pl.BlockSpec((tm, c), lambda i, *_: (i, 0))


def _mod_spec(grp):
    return pl.BlockSpec((None, 8, D), lambda i, *_: (grp.seg_of(i), 0, 0))


def _dot(a, b):
    return jnp.dot(a, b, preferred_element_type=f32)


def _dot_nt(a, b):
    return lax.dot_general(a, b, (((1,), (1,)), ((), ())), preferred_element_type=f32)


def _silu(x):
    return x * jax.nn.sigmoid(x)


def _ln_rows(x, g, b):
    mu = jnp.mean(x, axis=-1, keepdims=True)
    xc = x - mu
    var = jnp.mean(xc * xc, axis=-1, keepdims=True)
    return xc * lax.rsqrt(var + LN_EPS) * g + b


def _rms_rows(x, g):
    return x * lax.rsqrt(jnp.mean(x * x, axis=-1, keepdims=True) + RMS_EPS) * g


def _split_bf16(x):
    hi = x.astype(bf16)
    lo = (x - hi.astype(f32)).astype(bf16)
    return hi, lo


def _subs():
    return [slice(s * SUB, (s + 1) * SUB) for s in range(TM // SUB)]


def _mod_kernel(c_ref, w_ref, b_ref, o_ref):
    x = _silu(c_ref[...])
    xh, xl = _split_bf16(x)
    wh, wl = _split_bf16(w_ref[...])
    o_ref[...] = _dot(xh, wh) + _dot(xh, wl) + _dot(xl, wh) + b_ref[...]


def _modulation(cond8, w_mod, b_mod):
    nk = 6
    out = pl.pallas_call(
        _mod_kernel,
        grid=(DEPTH, nk),
        in_specs=[pl.BlockSpec((8, D), lambda l, k: (0, 0)),
                  pl.BlockSpec((None, D, D), lambda l, k: (l, 0, k)),
                  pl.BlockSpec((None, None, 1, D), lambda l, k: (l, k, 0, 0))],
        out_specs=pl.BlockSpec((None, None, 8, D), lambda l, k: (l, k, 0, 0)),
        out_shape=jax.ShapeDtypeStruct((DEPTH, nk, 8, D), f32),
        compiler_params=_cp(32, 2),
        name="modulation",
    )(cond8, w_mod, b_mod.reshape(DEPTH, nk, 1, D))
    out = jnp.transpose(out, (0, 2, 1, 3))
    return jnp.pad(out, ((0, 0), (0, 0), (0, 2), (0, 0)))


def _router_aff_t(h2, rw_ref):
    hh, hl = _split_bf16(h2)
    wh, wl = _split_bf16(rw_ref[...])
    lg = _dot(hh, wh) + _dot(hl, wh) + _dot(hh, wl)
    lg = jnp.transpose(lg)[:NE, :]
    m = jnp.max(lg, axis=0, keepdims=True)
    e = jnp.exp(lg - m)
    return e / jnp.sum(e, axis=0, keepdims=True)


def _close(d, x, m, lnp, rwt_ref, x1_ref, h2_ref, aff_ref, sl):
    x1 = _ln_rows(ALPHA * x + m[2:3] * d, lnp[0:1], lnp[1:2])
    x1_ref[sl, :] = x1
    h2 = x1 * (1.0 + m[4:5]) + m[3:4]
    h2_ref[sl, :] = h2.astype(bf16)
    aff_ref[sl.start // RT] = _router_aff_t(h2, rwt_ref)


def _close_outs(n):
    shapes = (jax.ShapeDtypeStruct((n, D), f32), jax.ShapeDtypeStruct((n, D), bf16),
              jax.ShapeDtypeStruct((n // RT, NE, RT), f32))
    specs = (_rows(TM, D), _rows(TM, D),
             pl.BlockSpec((TM // RT, NE, RT), lambda i, *_: (i, 0, 0)))
    return shapes, specs


def _proj_close_kernel(o_ref, x_ref, mod_ref, lnp_ref, wo_ref, rwt_ref, x1_ref, h2_ref, aff_ref):
    m, lnp = mod_ref[...], lnp_ref[...]
    subs = _subs()
    nxt = _dot(o_ref[subs[0], :], wo_ref[...])
    for s, sl in enumerate(subs):
        d = nxt
        if s + 1 < len(subs):
            nxt = _dot(o_ref[subs[s + 1], :], wo_ref[...])
        _close(d, x_ref[sl, :], m, lnp, rwt_ref, x1_ref, h2_ref, aff_ref, sl)


def _proj_close(grp, o, x, mod, lnp, wo, rwt):
    shapes, specs = _close_outs(grp.n)
    return pl.pallas_call(
        _proj_close_kernel,
        grid=(grp.n // TM,),
        in_specs=[_rows(TM, o.shape[1]), _rows(TM, D), _mod_spec(grp), _const((8, D)),
                  _const(wo.shape), _const((D, LANE))],
        out_specs=specs, out_shape=shapes,
        compiler_params=_cp(40), name="proj_close",
    )(o, x, mod, lnp, wo, rwt)


def _rope(x, cos, sin):
    w = x.shape[1]
    reps = w // LANE
    c = jnp.concatenate([cos] * reps, axis=1) if reps > 1 else cos
    s = jnp.concatenate([sin] * reps, axis=1) if reps > 1 else sin
    lane = lax.broadcasted_iota(i32, x.shape, 1)
    up = pltpu.roll(x, w - 16, 1)
    dn = pltpu.roll(x, 16, 1)
    partner = jnp.where((lane % 32) < 16, up, dn)
    return x * c + partner * s


def _rope_tables(length):
    t = jnp.arange(length)
    rows, cols = (t // GRID_W).astype(f32), (t % GRID_W).astype(f32)
    inv = ROPE_BASE ** (-jnp.arange(16, dtype=f32) / 16)
    ar, ac = rows[:, None] * inv[None, :], cols[:, None] * inv[None, :]
    cos = jnp.concatenate([jnp.cos(ar), jnp.cos(ar), jnp.cos(ac), jnp.cos(ac)], axis=1)
    sin = jnp.concatenate([-jnp.sin(ar), jnp.sin(ar), -jnp.sin(ac), jnp.sin(ac)], axis=1)
    return jnp.concatenate([cos, cos], axis=1), jnp.concatenate([sin, sin], axis=1)


def _tab_spec():
    per = DEC_SEQ // TM
    return pl.BlockSpec((TM, LANE), lambda i, *_: (i % per, 0))


def _mla_pre_kernel(rope, x_ref, mod_ref, wqa_ref, qn_ref, wqb_ref, wkva_ref, kvn_ref, wkvb_ref, *rest):
    if rope:
        cos_ref, sin_ref, qlat_ref, qctx_ref, kcat_ref, v_ref = rest
    else:
        qctx_ref, kcat_ref, v_ref, ckv_ref, kpe_ref = rest
    m = mod_ref[...]

    def front(sl):
        h = (x_ref[sl, :] * (1.0 + m[1:2]) + m[0:1]).astype(bf16)
        return _dot(h, wqa_ref[...]), _dot(h, wkva_ref[...])

    subs = _subs()
    nxt = front(subs[0])
    for s, sl in enumerate(subs):
        qa, kv = nxt
        if s + 1 < len(subs):
            nxt = front(subs[s + 1])
        qa = _rms_rows(qa, qn_ref[...])
        q = _dot(qa.astype(bf16), wqb_ref[...])
        ckv = _rms_rows(kv[:, :MLA_KV_LORA], kvn_ref[...])
        kpe2 = kv[:, MLA_KV_LORA:MLA_KV_LORA + LANE]
        kvb = _dot(ckv.astype(bf16), wkvb_ref[...])
        q_pe = q[:, MLA_HEADS * MLA_NOPE:]
        if rope:
            cos, sin = cos_ref[sl, :], sin_ref[sl, :]
            q_pe_rot = _rope(q_pe, cos, sin).astype(bf16)
            kpe2 = _rope(kpe2, cos, sin)
        else:
            ckv_ref[sl, :] = ckv
            kpe_ref[sl, :] = kv[:, MLA_KV_LORA:MLA_KV_LORA + MLA_ROPE]
        q_pe = q_pe.astype(bf16)
        qn = q[:, :MLA_HEADS * MLA_NOPE].astype(bf16)
        kn = kvb[:, :MLA_HEADS * MLA_NOPE].astype(bf16)
        v_ref[sl, :] = kvb[:, MLA_HEADS * MLA_NOPE:].astype(bf16)
        lane = lax.broadcasted_iota(i32, kpe2.shape, 1)
        kpe_lo = jnp.where(lane < MLA_ROPE, kpe2, 0.0).astype(bf16)
        kpe_hi = jnp.where(lane >= MLA_ROPE, kpe2, 0.0).astype(bf16)
        for hh in range(MLA_HEADS):
            a, b = hh * 256, hh * 256 + LANE
            pr = (hh // 2) * LANE
            qctx_ref[sl, a:b] = qn[:, hh * LANE:(hh + 1) * LANE]
            qctx_ref[sl, b:b + LANE] = q_pe[:, pr:pr + LANE]
            if rope:
                qlat_ref[sl, a:b] = qn[:, hh * LANE:(hh + 1) * LANE]
                qlat_ref[sl, b:b + LANE] = q_pe_rot[:, pr:pr + LANE]
            kcat_ref[sl, a:b] = kn[:, hh * LANE:(hh + 1) * LANE]
            kcat_ref[sl, b:b + LANE] = kpe_lo if hh % 2 == 0 else kpe_hi


def _mla_pre(grp, x, mod, w, rope_tabs):
    n = grp.n
    ins = [x, mod] + list(w)
    specs = [_rows(TM, D), _mod_spec(grp)] + [_const(a.shape) for a in w]
    wide = jax.ShapeDtypeStruct((n, 2 * D), bf16)
    if rope_tabs is not None:
        ins += list(rope_tabs)
        specs += [_tab_spec(), _tab_spec()]
        shapes = (wide, wide, wide, jax.ShapeDtypeStruct((n, D), bf16))
        ospecs = (_rows(TM, 2 * D), _rows(TM, 2 * D), _rows(TM, 2 * D), _rows(TM, D))
    else:
        shapes = (wide, wide, jax.ShapeDtypeStruct((n, D), bf16),
                  jax.ShapeDtypeStruct((n, MLA_KV_LORA), f32), jax.ShapeDtypeStruct((n, MLA_ROPE), f32))
        ospecs = (_rows(TM, 2 * D), _rows(TM, 2 * D), _rows(TM, D), _rows(TM, MLA_KV_LORA), _rows(TM, MLA_ROPE))
    return pl.pallas_call(
        functools.partial(_mla_pre_kernel, rope_tabs is not None),
        grid=(n // TM,), in_specs=specs, out_specs=ospecs, out_shape=shapes,
        compiler_params=_cp(48), name="mla_pre",
    )(*ins)


def _mla_ctx_kernel(ckv_ref, kpe2_ref, wkvb_ref, kcat_ref, v_ref):
    kvb = _dot(ckv_ref[...].astype(bf16), wkvb_ref[...])
    kn = kvb[:, :MLA_HEADS * MLA_NOPE].astype(bf16)
    v_ref[...] = kvb[:, MLA_HEADS * MLA_NOPE:].astype(bf16)
    kpe2 = kpe2_ref[...]
    lane = lax.broadcasted_iota(i32, kpe2.shape, 1)
    kpe_lo = jnp.where(lane < MLA_ROPE, kpe2, 0.0).astype(bf16)
    kpe_hi = jnp.where(lane >= MLA_ROPE, kpe2, 0.0).astype(bf16)
    for hh in range(MLA_HEADS):
        a, b = hh * 256, hh * 256 + LANE
        kcat_ref[:, a:b] = kn[:, hh * LANE:(hh + 1) * LANE]
        kcat_ref[:, b:b + LANE] = kpe_lo if hh % 2 == 0 else kpe_hi


def _mla_ctx(ckv, kpe2, wkvb):
    n = ckv.shape[0]
    return pl.pallas_call(
        _mla_ctx_kernel, grid=(n // PAST,),
        in_specs=[_rows(PAST, MLA_KV_LORA), _rows(PAST, LANE), _const(wkvb.shape)],
        out_specs=(_rows(PAST, 2 * D), _rows(PAST, D)),
        out_shape=(jax.ShapeDtypeStruct((n, 2 * D), bf16), jax.ShapeDtypeStruct((n, D), bf16)),
        compiler_params=_cp(24), name="mla_ctx",
    )(ckv, kpe2, wkvb)


def _mla_attn_p_kernel(q_ref, k_ref, v_ref, o_ref):
    def scores(hh):
        return _dot_nt(q_ref[:, hh * 256:(hh + 1) * 256], k_ref[:, hh * 256:(hh + 1) * 256])

    nxt = scores(0)
    for hh in range(MLA_HEADS):
        s = nxt
        if hh + 1 < MLA_HEADS:
            nxt = scores(hh + 1)
        e = jnp.exp2((s - jnp.max(s, axis=-1, keepdims=True)) * MLA_C2)
        l = jnp.sum(e, axis=-1, keepdims=True)
        o = _dot(e.astype(bf16), v_ref[:, hh * LANE:(hh + 1) * LANE]) / l
        o_ref[:, hh * LANE:(hh + 1) * LANE] = o.astype(bf16)


def _mla_attn_p(q, k, v):
    n = q.shape[0]
    return pl.pallas_call(
        _mla_attn_p_kernel, grid=(n // SEQ,),
        in_specs=[_rows(SEQ, 2 * D), _rows(SEQ, 2 * D), _rows(SEQ, D)],
        out_specs=_rows(SEQ, D), out_shape=jax.ShapeDtypeStruct((n, D), bf16),
        compiler_params=_cp(24), name="mla_attn_ctx",
    )(q, k, v)


MLA_QT = 256


def _mla_attn_s_kernel(ql_ref, qc_ref, kl_ref, vl_ref, kc_ref, vc_ref, o_ref):
    def scores(hh):
        a, b = hh * 256, (hh + 1) * 256
        return (_dot_nt(ql_ref[:, a:b], kl_ref[:, a:b]), _dot_nt(qc_ref[:, a:b], kc_ref[:, a:b]))

    nxt = scores(0)
    for hh in range(MLA_HEADS):
        s1, s2 = nxt
        if hh + 1 < MLA_HEADS:
            nxt = scores(hh + 1)
        m = jnp.maximum(jnp.max(s1, axis=-1, keepdims=True), jnp.max(s2, axis=-1, keepdims=True))
        e1, e2 = jnp.exp2((s1 - m) * MLA_C2), jnp.exp2((s2 - m) * MLA_C2)
        l = jnp.sum(e1, axis=-1, keepdims=True) + jnp.sum(e2, axis=-1, keepdims=True)
        o = (_dot(e1.astype(bf16), vl_ref[:, hh * LANE:(hh + 1) * LANE])
             + _dot(e2.astype(bf16), vc_ref[:, hh * LANE:(hh + 1) * LANE])) / l
        o_ref[:, hh * LANE:(hh + 1) * LANE] = o.astype(bf16)


def _mla_attn_s(ql, qc, kl, vl, kc, vc):
    nq = DEC_SEQ // MLA_QT
    qs = pl.BlockSpec((MLA_QT, 2 * D), lambda b, i: (b * nq + i, 0))
    return pl.pallas_call(
        _mla_attn_s_kernel, grid=(DEC_BATCH, nq),
        in_specs=[qs, qs,
                  pl.BlockSpec((DEC_SEQ, 2 * D), lambda b, i: (b, 0)),
                  pl.BlockSpec((DEC_SEQ, D), lambda b, i: (b, 0)),
                  pl.BlockSpec((PAST, 2 * D), lambda b, i: (b, 0)),
                  pl.BlockSpec((PAST, D), lambda b, i: (b, 0))],
        out_specs=pl.BlockSpec((MLA_QT, D), lambda b, i: (b * nq + i, 0)),
        out_shape=jax.ShapeDtypeStruct((NS_TOK, D), bf16),
        compiler_params=_cp(48, 2), name="mla_attn_lat",
    )(ql, qc, kl, vl, kc, vc)


def _gelu_tanh(x):
    return 0.5 * x * (1.0 + jnp.tanh(math.sqrt(2.0 / math.pi) * (x + 0.044715 * (x * x * x))))


GM_CW = GM_HALF // GM_GROUPS


def _gmlp_kernel(x_ref, mod_ref, lnp_ref, win_ref, bin_ref, vg_ref, vb_ref, ws_ref, bs_ref, wout_ref, bout_ref,
                 rwt_ref, x1_ref, h2_ref, aff_ref, gated_ref, vz_ref):
    m, lnp = mod_ref[...], lnp_ref[...]
    for s, sl in enumerate(_subs()):
        x = x_ref[sl, :]
        h = (x * (1.0 + m[1:2]) + m[0:1]).astype(bf16)

        def mm(j):
            return _dot(h, win_ref[:, j * GM_CW:(j + 1) * GM_CW]) + bin_ref[:, j * GM_CW:(j + 1) * GM_CW]

        s1 = jnp.zeros((SUB, 1), f32)
        s2 = jnp.zeros((SUB, 1), f32)
        nxt = mm(GM_GROUPS)
        for g in range(GM_GROUPS):
            cur = nxt
            nxt = mm(GM_GROUPS + g + 1) if g + 1 < GM_GROUPS else mm(0)
            z = _gelu_tanh(cur)
            s1 = s1 + jnp.sum(z, axis=-1, keepdims=True)
            s2 = s2 + jnp.sum(z * z, axis=-1, keepdims=True)
            vz_ref[s, :, g * GM_CW:(g + 1) * GM_CW] = z
        mu = s1 * (1.0 / GM_HALF)
        rstd = lax.rsqrt(s2 * (1.0 / GM_HALF) - mu * mu + LN_EPS)
        for g in range(GM_GROUPS):
            cur = nxt
            if g + 1 < GM_GROUPS:
                nxt = mm(g + 1)
            cols = slice(g * GM_CW, (g + 1) * GM_CW)
            u = _gelu_tanh(cur)
            v = ((vz_ref[s, :, cols] - mu) * rstd * vg_ref[:, cols] + vb_ref[:, cols]).astype(bf16)
            for c in range(SUB // GM_CHUNK):
                r0, r1 = c * GM_CHUNK, (c + 1) * GM_CHUNK
                sv = _dot(ws_ref[g], v[r0:r1, :]) + bs_ref[:, g:g + 1]
                gated_ref[sl.start + r0:sl.start + r1, cols] = (u[r0:r1, :] * sv).astype(bf16)
        d = _dot(gated_ref[sl, :], wout_ref[...]) + bout_ref[...]
        _close(d, x, m, lnp, rwt_ref, x1_ref, h2_ref, aff_ref, sl)


def _gmlp_layer(grp, x, mod, lnp, w, rwt):
    shapes, specs = _close_outs(grp.n)
    return pl.pallas_call(
        _gmlp_kernel, grid=(grp.n // TM,),
        in_specs=[_rows(TM, D), _mod_spec(grp), _const((8, D))] + [_const(a.shape) for a in w]
        + [_const((D, LANE))],
        out_specs=specs, out_shape=shapes,
        scratch_shapes=[pltpu.VMEM((TM, GM_HALF), bf16), pltpu.VMEM((TM // SUB, SUB, GM_HALF), f32)],
        compiler_params=_cp(56), name="gmlp_layer",
    )(x, mod, lnp, *w, rwt)


HALO = 16
CONV_RB = 64
CONV_LW = 256


def _conv_glu_kernel(x_ref, mod_ref, w_ref, b_ref, a_ref):
    m = mod_ref[...]

    def front(sl):
        h = (x_ref[sl, :] * (1.0 + m[1:2]) + m[0:1]).astype(bf16)
        return _dot(h, w_ref[...])

    subs = _subs()
    nxt = front(subs[0])
    for s, sl in enumerate(subs):
        a = nxt + b_ref[...]
        if s + 1 < len(subs):
            nxt = front(subs[s + 1])
        a_ref[sl, :] = a[:, :D] * jax.nn.sigmoid(a[:, D:])


def _conv_glu(grp, x, mod, w, b):
    return pl.pallas_call(
        _conv_glu_kernel, grid=(grp.n // TM,),
        in_specs=[_rows(TM, D), _mod_spec(grp), _const(w.shape), _const(b.shape)],
        out_specs=_rows(TM, D), out_shape=jax.ShapeDtypeStruct((grp.n, D), f32),
        compiler_params=_cp(40), name="conv_glu",
    )(x, mod, w, b)


def _conv_close_kernel(seq_subs, ap_ref, a_ref, an_ref, x_ref, mod_ref, lnp_ref, wdw_ref, bdw_ref, ng_ref, nb_ref,
                       w2_ref, b2_ref, rwt_ref, x1_ref, h2_ref, aff_ref, pad_ref, act_ref, cout_ref, shift_ref):
    i = pl.program_id(0)
    nsub = TM // SUB
    bdw, ng, nb = bdw_ref[...], ng_ref[...], nb_ref[...]
    m, lnp = mod_ref[...], lnp_ref[...]
    for s, sl in enumerate(_subs()):
        gsub = i * nsub + s
        prev = a_ref[sl.start - HALO:sl.start, :] if s > 0 else ap_ref[...]
        nxt = a_ref[sl.stop:sl.stop + HALO, :] if s < nsub - 1 else an_ref[...]
        pad = pad_ref.at[s]
        pad[0:HALO, :] = jnp.where((gsub % seq_subs) != 0, prev, 0.0)
        pad[HALO:HALO + SUB, :] = a_ref[sl, :]
        pad[HALO + SUB:, :] = jnp.where((gsub % seq_subs) != seq_subs - 1, nxt, 0.0)
        span = SUB + 24
        for r in range(1, 8):
            shift_ref[s, r - 1, :, :] = pad[r:r + span, :]
        for rb in range(SUB // CONV_RB):
            r0 = rb * CONV_RB
            for lc in range(D // CONV_LW):
                lanes = slice(lc * CONV_LW, (lc + 1) * CONV_LW)
                acc = jnp.zeros((CONV_RB, CONV_LW), f32)
                for k in range(CONV_W):
                    mm, r = (k + 1) // 8, (k + 1) % 8
                    rows = slice(r0 + 8 * mm, r0 + 8 * mm + CONV_RB)
                    win = pad[rows, lanes] if r == 0 else shift_ref[s, r - 1, rows, lanes]
                    acc = acc + wdw_ref[k:k + 1, lanes] * win
                cout_ref[s, r0:r0 + CONV_RB, lanes] = acc
            y = _silu(_ln_rows(cout_ref[s, r0:r0 + CONV_RB, :] + bdw, ng, nb))
            act_ref[sl.start + r0:sl.start + r0 + CONV_RB, :] = y.astype(bf16)
        d = _dot(act_ref[sl, :], w2_ref[...]) + b2_ref[...]
        _close(d, x_ref[sl, :], m, lnp, rwt_ref, x1_ref, h2_ref, aff_ref, sl)


def _conv_close(grp, a, x, mod, lnp, w, rwt):
    n = grp.n
    hb = TM // HALO
    last = n // HALO - 1
    shapes, specs = _close_outs(n)
    return pl.pallas_call(
        functools.partial(_conv_close_kernel, grp.seq // SUB), grid=(n // TM,),
        in_specs=[pl.BlockSpec((HALO, D), lambda i: (jnp.maximum(i * hb - 1, 0), 0)),
                  _rows(TM, D),
                  pl.BlockSpec((HALO, D), lambda i: (jnp.minimum((i + 1) * hb, last), 0)),
                  _rows(TM, D), _mod_spec(grp), _const((8, D))]
        + [_const(t.shape) for t in w] + [_const((D, LANE))],
        out_specs=specs, out_shape=shapes,
        scratch_shapes=[pltpu.VMEM((TM // SUB, SUB + 2 * HALO, D), f32), pltpu.VMEM((TM, D), bf16),
                        pltpu.VMEM((TM // SUB, SUB, D), f32), pltpu.VMEM((TM // SUB, 7, SUB + 24, D), f32)],
        compiler_params=_cp(48), name="conv_close",
    )(a, a, a, x, mod, lnp, *w, rwt)


def _swa_pre_kernel(rope, x_ref, mod_ref, wq_ref, wk_ref, wv_ref, *rest):
    if rope:
        cos_ref, sin_ref, qrot_ref, qraw_ref, kd_ref, vd_ref = rest
    else:
        qraw_ref, kd_ref, vd_ref, k_ref, v_ref = rest
    m = mod_ref[...]

    def front(sl):
        h = (x_ref[sl, :] * (1.0 + m[1:2]) + m[0:1]).astype(bf16)
        return _dot(h, wq_ref[...]), _dot(h, wk_ref[...]), _dot(h, wv_ref[...])

    subs = _subs()
    nxt = front(subs[0])
    for s, sl in enumerate(subs):
        q, kd, vd = nxt
        if s + 1 < len(subs):
            nxt = front(subs[s + 1])
        qraw_ref[sl, :] = q.astype(bf16)
        vd_ref[sl, :] = vd.astype(bf16)
        if rope:
            cos, sin = cos_ref[sl, :], sin_ref[sl, :]
            qrot_ref[sl, :] = _rope(q, cos, sin).astype(bf16)
            kd_ref[sl, :] = _rope(kd, cos, sin).astype(bf16)
        else:
            kd_ref[sl, :] = kd.astype(bf16)
            lane = lax.broadcasted_iota(i32, (SUB, LANE), 1)
            for j in range(SWA_KV // 2):
                lo, hi = 2 * j * LANE, (2 * j + 1) * LANE
                k_ref[sl, j * LANE:(j + 1) * LANE] = jnp.where(lane < SWA_HD, kd[:, lo:lo + LANE], kd[:, hi:hi + LANE])
                v_ref[sl, j * LANE:(j + 1) * LANE] = jnp.where(lane < SWA_HD, vd[:, lo:lo + LANE], vd[:, hi:hi + LANE])


def _swa_pre(grp, x, mod, w, rope_tabs):
    n = grp.n
    ins = [x, mod] + list(w)
    specs = [_rows(TM, D), _mod_spec(grp)] + [_const(a.shape) for a in w]
    kw = 2 * SWA_KV * SWA_HD
    qs, ks = jax.ShapeDtypeStruct((n, D), bf16), jax.ShapeDtypeStruct((n, kw), bf16)
    if rope_tabs is not None:
        ins += list(rope_tabs)
        specs += [_tab_spec(), _tab_spec()]
        shapes = (qs, qs, ks, ks)
        ospecs = (_rows(TM, D), _rows(TM, D), _rows(TM, kw), _rows(TM, kw))
    else:
        nat = jax.ShapeDtypeStruct((n, SWA_KV * SWA_HD), f32)
        shapes = (qs, ks, ks, nat, nat)
        ospecs = (_rows(TM, D), _rows(TM, kw), _rows(TM, kw), _rows(TM, SWA_KV * SWA_HD), _rows(TM, SWA_KV * SWA_HD))
    return pl.pallas_call(
        functools.partial(_swa_pre_kernel, rope_tabs is not None),
        grid=(n // TM,), in_specs=specs, out_specs=ospecs, out_shape=shapes,
        compiler_params=_cp(40), name="swa_pre",
    )(*ins)


def _half_mask(x, parity):
    lane = lax.broadcasted_iota(i32, x.shape, 1)
    keep = (lane < SWA_HD) if parity == 0 else (lane >= SWA_HD)
    return jnp.where(keep, x, jnp.zeros_like(x))


def _swa_attn_p_kernel(sink_ref, q_ref, kd_ref, vd_ref, o_ref):
    def scores(hd):
        pair, par = hd // 2, hd % 2
        g = hd // (SWA_HEADS // SWA_KV)
        return _dot_nt(q_ref[:, pair * LANE:(pair + 1) * LANE], _half_mask(kd_ref[:, g * LANE:(g + 1) * LANE], par))

    nxt = scores(0)
    acc = None
    for hd in range(SWA_HEADS):
        pair, par = hd // 2, hd % 2
        g = hd // (SWA_HEADS // SWA_KV)
        s = nxt
        if hd + 1 < SWA_HEADS:
            nxt = scores(hd + 1)
        sk = sink_ref[hd] * (1.0 / SWA_SCALE)
        m = jnp.maximum(jnp.max(s, axis=-1, keepdims=True), sk)
        e = jnp.exp2((s - m) * SWA_C2)
        l = jnp.sum(e, axis=-1, keepdims=True) + jnp.exp2((sk - m) * SWA_C2)
        o = _dot(e.astype(bf16), _half_mask(vd_ref[:, g * LANE:(g + 1) * LANE], par)) / l
        acc = o if par == 0 else acc + o
        if par == 1:
            o_ref[:, pair * LANE:(pair + 1) * LANE] = acc.astype(bf16)


def _swa_attn_p(sink, q, kd, vd):
    n = q.shape[0]
    kw = kd.shape[1]
    return pl.pallas_call(
        _swa_attn_p_kernel,
        grid_spec=pltpu.PrefetchScalarGridSpec(
            num_scalar_prefetch=1, grid=(n // SEQ,),
            in_specs=[_rows(SEQ, D), _rows(SEQ, kw), _rows(SEQ, kw)],
            out_specs=_rows(SEQ, D)),
        out_shape=jax.ShapeDtypeStruct((n, D), bf16),
        compiler_params=_cp(24), name="swa_attn_ctx",
    )(sink, q, kd, vd)


SWA_QB = 128


def _swa_attn_s_kernel(sink_ref, qr_ref, qw_ref, kd_ref, vd_ref, kc_ref, vc_ref, o_ref):
    nblk = pl.program_id(1)
    span = 3 * SWA_QB
    start = pl.multiple_of(jnp.clip((nblk - 1) * SWA_QB, 0, DEC_SEQ - span), SWA_QB)
    qpos = nblk * SWA_QB + lax.broadcasted_iota(i32, (SWA_QB, span), 0)
    kpos = start + lax.broadcasted_iota(i32, (SWA_QB, span), 1)
    band = jnp.abs(kpos - qpos) <= SWA_WIN

    def scores(hd):
        pair, par = hd // 2, hd % 2
        g = hd // (SWA_HEADS // SWA_KV)
        cols = slice(g * LANE, (g + 1) * LANE)
        s1 = _dot_nt(qr_ref[:, pair * LANE:(pair + 1) * LANE], _half_mask(kd_ref[pl.ds(start, span), cols], par))
        s2 = _dot_nt(qw_ref[:, pair * LANE:(pair + 1) * LANE], _half_mask(kc_ref[:, cols], par))
        return s1, s2

    nxt = scores(0)
    acc = None
    for hd in range(SWA_HEADS):
        pair, par = hd // 2, hd % 2
        g = hd // (SWA_HEADS // SWA_KV)
        cols = slice(g * LANE, (g + 1) * LANE)
        s1, s2 = nxt
        if hd + 1 < SWA_HEADS:
            nxt = scores(hd + 1)
        s1 = jnp.where(band, s1, NEG_INF)
        sk = sink_ref[hd] * (1.0 / SWA_SCALE)
        m = jnp.maximum(jnp.maximum(jnp.max(s1, axis=-1, keepdims=True), jnp.max(s2, axis=-1, keepdims=True)), sk)
        e1, e2 = jnp.exp2((s1 - m) * SWA_C2), jnp.exp2((s2 - m) * SWA_C2)
        l = (jnp.sum(e1, axis=-1, keepdims=True) + jnp.sum(e2, axis=-1, keepdims=True)
             + jnp.exp2((sk - m) * SWA_C2))
        o = (_dot(e1.astype(bf16), _half_mask(vd_ref[pl.ds(start, span), cols], par))
             + _dot(e2.astype(bf16), _half_mask(vc_ref[:, cols], par))) / l
        acc = o if par == 0 else acc + o
        if par == 1:
            o_ref[:, pair * LANE:(pair + 1) * LANE] = acc.astype(bf16)


def _swa_attn_s(sink, qr, qw, kd, vd, kc, vc):
    nq = DEC_SEQ // SWA_QB
    kw = kd.shape[1]
    qs = pl.BlockSpec((SWA_QB, D), lambda b, i, *_: (b * nq + i, 0))
    full = pl.BlockSpec((DEC_SEQ, kw), lambda b, i, *_: (b, 0))
    ctx = pl.BlockSpec((PAST, kw), lambda b, i, *_: (b, 0))
    return pl.pallas_call(
        _swa_attn_s_kernel,
        grid_spec=pltpu.PrefetchScalarGridSpec(
            num_scalar_prefetch=1, grid=(DEC_BATCH, nq),
            in_specs=[qs, qs, full, full, ctx, ctx],
            out_specs=pl.BlockSpec((SWA_QB, D), lambda b, i, *_: (b * nq + i, 0))),
        out_shape=jax.ShapeDtypeStruct((NS_TOK, D), bf16),
        compiler_params=_cp(32, 2), name="swa_attn_lat",
    )(sink, qr, qw, kd, vd, kc, vc)


def _excl_prefix(mask_f, nb, tri, blk):
    m2 = mask_f.reshape(nb * NE, RT)
    within = _dot(m2.astype(bf16), tri)
    tot = jnp.sum(m2, axis=1, keepdims=True)
    totb = jnp.broadcast_to(tot, (nb * NE, LANE)).astype(bf16)
    offs = _dot(blk, totb)
    return (within + offs[:, 0:1]).reshape(nb, NE, RT), offs.reshape(nb, NE, LANE)


def _route_kernel(cap, nb, aff_ref, dest_ref, gate_ref, ws_ref, nq_ref, destt_ref):
    a = aff_ref[...]

    def as_f32(bits):
        return pltpu.bitcast(bits, f32)[None]

    def count_ge(th):
        c = jnp.sum(jnp.where(a >= as_f32(th), 1.0, 0.0), axis=0)
        return jnp.sum(c, axis=1, keepdims=True)

    def body(_, c):
        lo, hi = c
        mid = lo + ((hi - lo + 1) >> 1)
        ok = count_ge(mid) >= cap
        return jnp.where(ok, mid, lo), jnp.where(ok, hi, mid - 1)

    lo0 = jnp.zeros((NE, 1), i32)
    hi0 = jnp.full((NE, 1), 0x7F800000, i32)
    thr, _ = lax.fori_loop(0, 31, body, (lo0, hi0))

    r = lax.broadcasted_iota(i32, (RT, RT), 0)
    c = lax.broadcasted_iota(i32, (RT, RT), 1)
    tri = jnp.where(r < c, 1.0, 0.0).astype(bf16)
    rr = lax.broadcasted_iota(i32, (nb * NE, nb * NE), 0)
    cc = lax.broadcasted_iota(i32, (nb * NE, nb * NE), 1)
    blk = jnp.where(((rr % NE) == (cc % NE)) & (cc < rr), 1.0, 0.0).astype(bf16)

    gt = a > as_f32(thr)
    eq = a == as_f32(thr)
    n_gt = jnp.sum(jnp.sum(jnp.where(gt, 1.0, 0.0), axis=0), axis=1, keepdims=True)
    need = cap - n_gt
    tie_rank, _ = _excl_prefix(jnp.where(eq, 1.0, 0.0), nb, tri, blk)
    sel = gt | (eq & (tie_rank < need[None]))
    pos, offs = _excl_prefix(jnp.where(sel, 1.0, 0.0), nb, tri, blk)
    dest_ref[...] = jnp.where(sel, pos, -1.0).astype(i32)
    gate_ref[...] = jnp.where(sel, a, 0.0)
    ct_end = jnp.concatenate([offs[1:], jnp.full((1, NE, LANE), float(cap), f32)], axis=0)
    ws = jnp.floor(offs * (1.0 / 16)) * 16.0
    ws_ref[...] = ws.astype(i32)
    nchunk = jnp.floor((ct_end - ws + (WIN - 1)) * (1.0 / WIN))
    nq_ref[...] = jnp.broadcast_to(jnp.max(nchunk, axis=1, keepdims=True), (nb, NE, LANE)).astype(i32)
    destp = jnp.where(sel, pos + 1.0, 0.0)
    zpad = jnp.zeros((LANE - NE, RT), f32)
    for b in range(nb):
        destt_ref[b * RT:(b + 1) * RT, :] = jnp.transpose(jnp.concatenate([destp[b], zpad], axis=0))


def _route(grp, aff3):
    nb = grp.n // RT
    full = pl.BlockSpec((nb, NE, RT), lambda: (0, 0, 0))
    small = pl.BlockSpec((nb, NE, LANE), lambda: (0, 0, 0))
    dest3, gate3, ws3, nq3, destt = pl.pallas_call(
        functools.partial(_route_kernel, grp.cap, nb),
        in_specs=[full], out_specs=(full, full, small, small, pl.BlockSpec((grp.n, LANE), lambda: (0, 0))),
        out_shape=(jax.ShapeDtypeStruct((nb, NE, RT), i32), jax.ShapeDtypeStruct((nb, NE, RT), f32),
                   jax.ShapeDtypeStruct((nb, NE, LANE), i32), jax.ShapeDtypeStruct((nb, NE, LANE), i32),
                   jax.ShapeDtypeStruct((grp.n, LANE), f32)),
        compiler_params=pltpu.CompilerParams(vmem_limit_bytes=48 * MIB), name="route",
    )(aff3)
    return dest3, gate3, ws3[:, :, 0], nq3[:, 0, 0], destt


def _dispatch_kernel(cap, eg, ws_ref, nq_ref, h2_ref, dest_ref, gate_ref, xe_ref, gcol_ref, hot_ref):
    g, step = pl.program_id(0), pl.program_id(1)

    @pl.when(step == 0)
    def _():
        xe_ref[...] = jnp.zeros_like(xe_ref)
        gcol_ref[...] = jnp.zeros_like(gcol_ref)

    wrow = lax.broadcasted_iota(i32, (WIN, RT), 0)
    for b in range(DISPATCH_BLOCKS):
        tb = step * DISPATCH_BLOCKS + b
        hot = hot_ref.at[b]

        def body(q, carry, tb=tb, b=b, hot=hot):
            starts, gsums = [], []
            for el in range(eg):
                e = g * eg + el
                want = ws_ref[tb, e] + q * WIN
                st = pl.multiple_of(jnp.minimum(want, cap - WIN), 16)
                row = st + wrow
                hit = dest_ref[b, pl.ds(e, 1), :] == jnp.where(row >= want, row, -7)
                hot[el * WIN:(el + 1) * WIN, :] = jnp.where(hit, 1.0, 0.0).astype(bf16)
                gsums.append(jnp.sum(jnp.where(hit, gate_ref[b, pl.ds(e, 1), :], 0.0), axis=1, keepdims=True))
                starts.append(st)
            part = _dot(hot[...], h2_ref[b * RT:(b + 1) * RT, :])
            for el in range(eg):
                dst = pl.ds(el * cap + starts[el], WIN)
                xe_ref[dst, :] = xe_ref[dst, :] + part[el * WIN:(el + 1) * WIN, :].astype(bf16)
                gcol_ref[dst, :] = gcol_ref[dst, :] + jnp.broadcast_to(gsums[el], (WIN, LANE))
            return carry

        lax.fori_loop(0, nq_ref[tb], body, 0)


def _dispatch(grp, wstart, nq, h2, dest3, gate3):
    eg = ROWS_PER_PASS // grp.cap
    nb = grp.n // (RT * DISPATCH_BLOCKS)
    tab = pl.BlockSpec((DISPATCH_BLOCKS, NE, RT), lambda g, t, *_: (t, 0, 0))
    return pl.pallas_call(
        functools.partial(_dispatch_kernel, grp.cap, eg),
        grid_spec=pltpu.PrefetchScalarGridSpec(
            num_scalar_prefetch=2, grid=(NE // eg, nb),
            in_specs=[pl.BlockSpec((RT * DISPATCH_BLOCKS, D), lambda g, t, *_: (t, 0)), tab, tab],
            out_specs=(pl.BlockSpec((ROWS_PER_PASS, D), lambda g, t, *_: (g, 0)),
                       pl.BlockSpec((ROWS_PER_PASS, LANE), lambda g, t, *_: (g, 0))),
            scratch_shapes=[pltpu.VMEM((DISPATCH_BLOCKS, eg * WIN, RT), bf16)]),
        out_shape=(jax.ShapeDtypeStruct((NE * grp.cap, D), bf16), jax.ShapeDtypeStruct((NE * grp.cap, LANE), f32)),
        compiler_params=_cp(56, 2), name="moe_dispatch",
    )(wstart, nq, h2, dest3, gate3)


FFN_RB = 256


def _ffn_kernel(xp_ref, gp_ref, xs_ref, gs_ref, wg_ref, wu_ref, wd_ref, yp_ref, ys_ref, wgb, wub, wdb):
    wgb[...] = wg_ref[...].astype(bf16)
    wub[...] = wu_ref[...].astype(bf16)
    wdb[...] = wd_ref[...].astype(bf16)
    for x_ref, g_ref, y_ref, cap in ((xp_ref, gp_ref, yp_ref, CTX.cap), (xs_ref, gs_ref, ys_ref, LAT.cap)):
        for r in range(cap // FFN_RB):
            sl = slice(r * FFN_RB, (r + 1) * FFN_RB)
            x = x_ref[sl, :]
            hid = (_silu(_dot(x, wgb[...])) * _dot(x, wub[...])).astype(bf16)
            gate = jnp.concatenate([g_ref[sl, :]] * (D // LANE), axis=1)
            y_ref[sl, :] = (_dot(hid, wdb[...]) * gate).astype(bf16)


def _ffn(layer, xp, gp, xs, gs, wg, wu, wd):
    wspec = pl.BlockSpec((None, None, D, FF), lambda e: (layer, e, 0, 0))
    cp, cs = CTX.cap, LAT.cap
    return pl.pallas_call(
        _ffn_kernel, grid=(NE,),
        in_specs=[pl.BlockSpec((cp, D), lambda e: (e, 0)), pl.BlockSpec((cp, LANE), lambda e: (e, 0)),
                  pl.BlockSpec((cs, D), lambda e: (e, 0)), pl.BlockSpec((cs, LANE), lambda e: (e, 0)),
                  wspec, wspec, pl.BlockSpec((None, None, FF, D), lambda e: (layer, e, 0, 0))],
        out_specs=(pl.BlockSpec((cp, D), lambda e: (e, 0)), pl.BlockSpec((cs, D), lambda e: (e, 0))),
        out_shape=(jax.ShapeDtypeStruct((NE * cp, D), bf16), jax.ShapeDtypeStruct((NE * cs, D), bf16)),
        scratch_shapes=[pltpu.VMEM((D, FF), bf16), pltpu.VMEM((D, FF), bf16), pltpu.VMEM((FF, D), bf16)],
        compiler_params=_cp(56), name="moe_ffn",
    )(xp, gp, xs, gs, wg, wu, wd)


def _combine_kernel(cap, ws_ref, nq_ref, yg_ref, dt_ref, x1_ref, mod_ref, lnp_ref, x2_ref, win_ref, y_ref):
    step = pl.program_id(0)
    kw = NE * WIN
    lane = lax.broadcasted_iota(i32, (1, kw), 1)
    lane_e, lane_w = lane // WIN, lane % WIN
    er = lax.broadcasted_iota(i32, (2 * LANE, kw), 0)
    ec = lax.broadcasted_iota(i32, (2 * LANE, kw), 1) // WIN
    spread = jnp.where(er == ec, 32.0, jnp.where(er - LANE == ec, 1.0, 0.0)).astype(bf16)

    def spread_rows(rows):
        d = dt_ref[rows, :]
        hi = jnp.floor(d * (1.0 / 32))
        hilo = jnp.concatenate([hi, d - 32.0 * hi], axis=1).astype(bf16)
        return _dot(hilo, spread)

    def chunk(b, tb, q, destp):
        win = win_ref.at[b]
        tgt = jnp.full((1, kw), -1, i32)
        for e in range(NE):
            want = ws_ref[tb, e] + q * WIN
            st = pl.multiple_of(jnp.minimum(want, cap - WIN), 16)
            win[e * WIN:(e + 1) * WIN, :] = yg_ref[pl.ds(e * cap + st, WIN), :]
            row = st + lane_w
            tgt = jnp.where(lane_e == e, jnp.where(row >= want, row + 1, -1), tgt)
        onehot = jnp.where(destp == tgt.astype(f32), 1.0, 0.0).astype(bf16)
        return _dot(onehot, win[...])

    blocks = [(b, step * BLOCKS_PER_STEP + b, slice(b * RT, (b + 1) * RT)) for b in range(BLOCKS_PER_STEP)]
    destps = []
    for b, tb, rows in blocks:
        destps.append(spread_rows(rows))
        y_ref[b] = chunk(b, tb, 0, destps[b])
    for b, tb, rows in blocks:
        @pl.when(nq_ref[tb] > 1)
        def _(b=b, tb=tb, rows=rows):
            destp = spread_rows(rows)

            def body(q, carry):
                y_ref[b] = y_ref[b] + chunk(b, tb, q, destp)
                return carry

            lax.fori_loop(1, nq_ref[tb], body, 0)
    m, lnp = mod_ref[...], lnp_ref[...]
    for b, tb, rows in blocks:
        x2_ref[rows, :] = _ln_rows(ALPHA * x1_ref[rows, :] + m[5:6] * y_ref[b], lnp[2:3], lnp[3:4])


def _combine(grp, wstart, nq, yg, destab, x1, mod, lnp):
    rt = RT * BLOCKS_PER_STEP
    seg = pl.BlockSpec((None, 8, D), lambda i, *_: (grp.seg0 + (i * rt) // grp.seq if grp.seq > rt else grp.seg0, 0, 0))
    return pl.pallas_call(
        functools.partial(_combine_kernel, grp.cap),
        grid_spec=pltpu.PrefetchScalarGridSpec(
            num_scalar_prefetch=2, grid=(grp.n // rt,),
            in_specs=[_const(yg.shape), _rows(rt, LANE), _rows(rt, D), seg, _const((8, D))],
            out_specs=_rows(rt, D),
            scratch_shapes=[pltpu.VMEM((BLOCKS_PER_STEP, NE * WIN, D), bf16), pltpu.VMEM((BLOCKS_PER_STEP, RT, D), f32)]),
        out_shape=jax.ShapeDtypeStruct((grp.n, D), f32),
        compiler_params=_cp(52), name="moe_combine",
    )(wstart, nq, yg, destab, x1, mod, lnp)


def _moe(layer, closed, mod, lnp, wg, wu, wd):
    disp = []
    for grp, (x1, h2, aff3) in zip((CTX, LAT), closed):
        dest3, gate3, wstart, nq, destab = _route(grp, aff3)
        xe, gcol = _dispatch(grp, wstart, nq, h2, dest3, gate3)
        disp.append((xe, gcol, wstart, nq, destab, x1))
    ys = _ffn(layer, disp[0][0], disp[0][1], disp[1][0], disp[1][1], wg, wu, wd)
    return tuple(_combine(grp, d[2], d[3], y, d[4], d[5], mod, lnp)
                 for grp, d, y in zip((CTX, LAT), disp, ys))


def _mla_weights(wq_a, q_norm, wq_b, wkv_a, kv_norm, wkv_b):
    wqb = wq_b.reshape(MLA_Q_LORA, MLA_HEADS, MLA_NOPE + MLA_ROPE)
    wqb = jnp.concatenate([wqb[:, :, :MLA_NOPE].reshape(MLA_Q_LORA, -1), wqb[:, :, MLA_NOPE:].reshape(MLA_Q_LORA, -1)], axis=1)
    wkva = jnp.concatenate([wkv_a, wkv_a[:, MLA_KV_LORA:]], axis=1)
    wkvb = wkv_b.reshape(MLA_KV_LORA, MLA_HEADS, MLA_NOPE + MLA_V)
    wkvb = jnp.concatenate([wkvb[:, :, :MLA_NOPE].reshape(MLA_KV_LORA, -1), wkvb[:, :, MLA_NOPE:].reshape(MLA_KV_LORA, -1)], axis=1)
    return (wq_a.astype(bf16), q_norm.reshape(1, -1), wqb.astype(bf16), wkva.astype(bf16), kv_norm.reshape(1, -1),
            wkvb.astype(bf16))


def _dup_heads(w):
    w = w.reshape(w.shape[0], SWA_KV, 1, SWA_HD)
    return jnp.broadcast_to(w, (w.shape[0], SWA_KV, 2, SWA_HD)).reshape(w.shape[0], 2 * SWA_KV * SWA_HD)


def kernel(x_prompt, x_sample, c, cache_mla_ckv, cache_mla_kpe, cache_swa_k, cache_swa_v, c_ctx, w_mod, b_mod, ln_gain, ln_bias, router_w, moe_w_gate, moe_w_up, moe_w_down, mla_wq_a, mla_q_norm, mla_wq_b, mla_wkv_a, mla_kv_norm, mla_wkv_b, mla_wo, gm_w_in, gm_b_in, gm_v_norm_g, gm_v_norm_b, gm_w_s, gm_b_s, gm_w_out, gm_b_out, cv_w_pw1, cv_b_pw1, cv_w_dw, cv_b_dw, cv_norm_g, cv_norm_b, cv_w_pw2, cv_b_pw2, swa_wq, swa_wk, swa_wv, swa_sink, swa_wo):
    groups = (CTX, LAT)
    xs = (x_prompt.reshape(NP_TOK, D), x_sample.reshape(NS_TOK, D))
    cond8 = jnp.concatenate([c_ctx[None, :], c, jnp.zeros((5, D), f32)], axis=0)
    mods = _modulation(cond8, w_mod, b_mod)
    lnps = jnp.concatenate([jnp.stack([ln_gain[:, 0], ln_bias[:, 0], ln_gain[:, 1], ln_bias[:, 1]], axis=1),
                            jnp.zeros((DEPTH, 4, D), f32)], axis=1)
    rwts = jnp.pad(router_w, ((0, 0), (0, 0), (0, LANE - NE)))
    tabs = _rope_tables(DEC_SEQ)

    def moe(i, closed):
        return _moe(i, closed, mods[i], lnps[i], moe_w_gate, moe_w_up, moe_w_down)

    wm = _mla_weights(mla_wq_a[0], mla_q_norm[0], mla_wq_b[0], mla_wkv_a[0], mla_kv_norm[0], mla_wkv_b[0])
    qc_p, kc_p, v_p, ckv_p, kpe_p = _mla_pre(CTX, xs[0], mods[0], wm, None)
    ql_s, qc_s, kc_s, v_s = _mla_pre(LAT, xs[1], mods[0], wm, tabs)
    cache_kpe = cache_mla_kpe[:, 0].reshape(DEC_BATCH * PAST, MLA_ROPE)
    kc_c, v_c = _mla_ctx(cache_mla_ckv[:, 0].reshape(DEC_BATCH * PAST, MLA_KV_LORA),
                         jnp.concatenate([cache_kpe, cache_kpe], axis=1), wm[5])
    os_ = (_mla_attn_p(qc_p, kc_p, v_p), _mla_attn_s(ql_s, qc_s, kc_s, v_s, kc_c, v_c))
    wo = mla_wo[0].astype(bf16)
    xs = moe(0, [_proj_close(g, o, x, mods[0], lnps[0], wo, rwts[0]) for g, o, x in zip(groups, os_, xs)])

    wgm = (gm_w_in[0].astype(bf16), gm_b_in[0].reshape(1, -1), gm_v_norm_g[0].reshape(1, -1),
           gm_v_norm_b[0].reshape(1, -1), gm_w_s[0].astype(bf16), gm_b_s[0].T, gm_w_out[0].astype(bf16),
           gm_b_out[0].reshape(1, -1))
    xs = moe(1, [_gmlp_layer(g, x, mods[1], lnps[1], wgm, rwts[1]) for g, x in zip(groups, xs)])

    w1, b1 = cv_w_pw1[0].astype(bf16), cv_b_pw1[0].reshape(1, -1)
    wcv = (cv_w_dw[0], cv_b_dw[0].reshape(1, -1), cv_norm_g[0].reshape(1, -1), cv_norm_b[0].reshape(1, -1),
           cv_w_pw2[0].astype(bf16), cv_b_pw2[0].reshape(1, -1))
    xs = moe(2, [_conv_close(g, _conv_glu(g, x, mods[2], w1, b1), x, mods[2], lnps[2], wcv, rwts[2])
                 for g, x in zip(groups, xs)])

    wsw = (swa_wq[0].astype(bf16), _dup_heads(swa_wk[0]).astype(bf16), _dup_heads(swa_wv[0]).astype(bf16))
    q_p, kd_p, vd_p, k_p, v_p = _swa_pre(CTX, xs[0], mods[3], wsw, None)
    qr_s, qw_s, kd_s, vd_s = _swa_pre(LAT, xs[1], mods[3], wsw, tabs)

    def dup_cache(t):
        return _dup_heads(t[:, 0].reshape(DEC_BATCH * PAST, SWA_KV * SWA_HD)).astype(bf16)

    sink = swa_sink[0]
    os_ = (_swa_attn_p(sink, q_p, kd_p, vd_p),
           _swa_attn_s(sink, qr_s, qw_s, kd_s, vd_s, dup_cache(cache_swa_k), dup_cache(cache_swa_v)))
    wo = swa_wo[0].astype(bf16)
    xs = moe(3, [_proj_close(g, o, x, mods[3], lnps[3], wo, rwts[3]) for g, o, x in zip(groups, os_, xs)])

    return (xs[0].reshape(BATCH, SEQ, D), xs[1].reshape(DEC_BATCH, DEC_SEQ, D),
            ckv_p.reshape(BATCH, 1, SEQ, MLA_KV_LORA), kpe_p.reshape(BATCH, 1, SEQ, MLA_ROPE),
            k_p.reshape(BATCH, 1, SEQ, SWA_KV, SWA_HD), v_p.reshape(BATCH, 1, SEQ, SWA_KV, SWA_HD))
```

```python
import functools
import math

import jax
import jax.numpy as jnp
from jax import lax
from jax.experimental import pallas as pl
from jax.experimental.pallas import tpu as pltpu

f32 = jnp.float32
bf16 = jnp.bfloat16
i32 = jnp.int32

D = 1024
BATCH, SEQ = 32, 256
DEC_BATCH, DEC_SEQ = 2, 2048
PAST = 256
DEPTH = 4
GRID_W = 64
ALPHA = (2 * DEPTH) ** 0.25
LN_EPS = 1e-5
RMS_EPS = 1e-6
ROPE_BASE = 10000.0
NEG_INF = -1e30
MLA_HEADS, MLA_NOPE, MLA_ROPE, MLA_V = 8, 128, 64, 128
MLA_Q_LORA, MLA_KV_LORA = 384, 256
MLA_SCALE = (MLA_NOPE + MLA_ROPE) ** -0.5
MLA_C2 = MLA_SCALE * math.log2(math.e)
GM_CHUNK, GM_HALF, GM_GROUPS = 128, 2048, 4
CONV_W = 31
SWA_HEADS, SWA_KV, SWA_HD, SWA_WIN = 16, 4, 64, 128
SWA_SCALE = SWA_HD ** -0.5
SWA_C2 = SWA_SCALE * math.log2(math.e)
NE = 16
FF = 1024

NP_TOK = BATCH * SEQ
NS_TOK = DEC_BATCH * DEC_SEQ

LANE = 128
TM = 512
SUB = 256
RT = 256
WIN = 64
BLOCKS_PER_STEP = 2
DISPATCH_BLOCKS = 4
ROWS_PER_PASS = 8192
MIB = 2 ** 20


class Group:
    def __init__(self, n_tok, seq, seg0):
        self.n = n_tok
        self.seq = seq
        self.cap = 2 * n_tok // NE
        self.seg0 = seg0

    def seg_of(self, i):
        return self.seg0 + (i * TM) // self.seq if self.seq > TM else self.seg0


CTX = Group(NP_TOK, SEQ, 0)
LAT = Group(NS_TOK, DEC_SEQ, 1)


def _cp(vmem_mb, n_axes=1):
    return pltpu.CompilerParams(dimension_semantics=("arbitrary",) * n_axes,
                                vmem_limit_bytes=int(vmem_mb * MIB))


def _const(shape):
    nd = len(shape)
    return pl.BlockSpec(shape, lambda *_: (0,) * nd, pipeline_mode=pl.Buffered(1))


def _rows(tm, c):
    return pl.BlockSpec((tm, c), lambda i, *_: (i, 0))


def _mod_spec(grp):
    return pl.BlockSpec((None, 8, D), lambda i, *_: (grp.seg_of(i), 0, 0))


def _dot(a, b):
    return jnp.dot(a, b, preferred_element_type=f32)


def _dot_nt(a, b):
    return lax.dot_general(a, b, (((1,), (1,)), ((), ())), preferred_element_type=f32)


def _silu(x):
    return x * jax.nn.sigmoid(x)


def _ln_rows(x, g, b):
    mu = jnp.mean(x, axis=-1, keepdims=True)
    xc = x - mu
    var = jnp.mean(xc * xc, axis=-1, keepdims=True)
    return xc * lax.rsqrt(var + LN_EPS) * g + b


def _rms_rows(x, g):
    return x * lax.rsqrt(jnp.mean(x * x, axis=-1, keepdims=True) + RMS_EPS) * g


def _split_bf16(x):
    hi = x.astype(bf16)
    lo = (x - hi.astype(f32)).astype(bf16)
    return hi, lo


def _subs():
    return [slice(s * SUB, (s + 1) * SUB) for s in range(TM // SUB)]


def _mod_kernel(c_ref, w_ref, b_ref, o_ref):
    x = _silu(c_ref[...])
    xh, xl = _split_bf16(x)
    wh = w_ref[...].astype(bf16)
    o_ref[...] = _dot(xh, wh) + _dot(xl, wh) + b_ref[...]


def _modulation(cond8, w_mod, b_mod):
    nk = 6
    out = pl.pallas_call(
        _mod_kernel,
        grid=(DEPTH, nk),
        in_specs=[pl.BlockSpec((8, D), lambda l, k: (0, 0)),
                  pl.BlockSpec((None, D, D), lambda l, k: (l, 0, k)),
                  pl.BlockSpec((None, None, 1, D), lambda l, k: (l, k, 0, 0))],
        out_specs=pl.BlockSpec((None, None, 8, D), lambda l, k: (l, k, 0, 0)),
        out_shape=jax.ShapeDtypeStruct((DEPTH, nk, 8, D), f32),
        compiler_params=_cp(32, 2),
        name="modulation",
    )(cond8, w_mod, b_mod.reshape(DEPTH, nk, 1, D))
    out = jnp.transpose(out, (0, 2, 1, 3))
    return jnp.pad(out, ((0, 0), (0, 0), (0, 2), (0, 0)))


def _router_aff_t(h2, rw_ref):
    hh, hl = _split_bf16(h2)
    wh, wl = _split_bf16(rw_ref[...])
    lg = _dot(hh, wh) + _dot(hl, wh) + _dot(hh, wl)
    lg = jnp.transpose(lg)[:NE, :]
    m = jnp.max(lg, axis=0, keepdims=True)
    e = jnp.exp(lg - m)
    return e / jnp.sum(e, axis=0, keepdims=True)


def _close(d, x, m, lnp, rwt_ref, x1_ref, h2_ref, aff_ref, sl):
    x1 = _ln_rows(ALPHA * x + m[2:3] * d, lnp[0:1], lnp[1:2])
    x1_ref[sl, :] = x1
    h2 = x1 * (1.0 + m[4:5]) + m[3:4]
    h2_ref[sl, :] = h2.astype(bf16)
    aff_ref[sl.start // RT] = _router_aff_t(h2, rwt_ref)


def _close_outs(n):
    shapes = (jax.ShapeDtypeStruct((n, D), f32), jax.ShapeDtypeStruct((n, D), bf16),
              jax.ShapeDtypeStruct((n // RT, NE, RT), f32))
    specs = (_rows(TM, D), _rows(TM, D),
             pl.BlockSpec((TM // RT, NE, RT), lambda i, *_: (i, 0, 0)))
    return shapes, specs


def _proj_close_kernel(o_ref, x_ref, mod_ref, lnp_ref, wo_ref, rwt_ref, x1_ref, h2_ref, aff_ref):
    m, lnp = mod_ref[...], lnp_ref[...]
    subs = _subs()
    nxt = _dot(o_ref[subs[0], :], wo_ref[...])
    for s, sl in enumerate(subs):
        d = nxt
        if s + 1 < len(subs):
            nxt = _dot(o_ref[subs[s + 1], :], wo_ref[...])
        _close(d, x_ref[sl, :], m, lnp, rwt_ref, x1_ref, h2_ref, aff_ref, sl)


def _proj_close(grp, o, x, mod, lnp, wo, rwt):
    shapes, specs = _close_outs(grp.n)
    return pl.pallas_call(
        _proj_close_kernel,
        grid=(grp.n // TM,),
        in_specs=[_rows(TM, o.shape[1]), _rows(TM, D), _mod_spec(grp), _const((8, D)),
                  _const(wo.shape), _const((D, LANE))],
        out_specs=specs, out_shape=shapes,
        compiler_params=_cp(40), name="proj_close",
    )(o, x, mod, lnp, wo, rwt)


def _rope(x, cos, sin):
    w = x.shape[1]
    reps = w // LANE
    c = jnp.concatenate([cos] * reps, axis=1) if reps > 1 else cos
    s = jnp.concatenate([sin] * reps, axis=1) if reps > 1 else sin
    lane = lax.broadcasted_iota(i32, x.shape, 1)
    up = pltpu.roll(x, w - 16, 1)
    dn = pltpu.roll(x, 16, 1)
    partner = jnp.where((lane % 32) < 16, up, dn)
    return x * c + partner * s


def _rope_tables(length):
    t = jnp.arange(length)
    rows, cols = (t // GRID_W).astype(f32), (t % GRID_W).astype(f32)
    inv = ROPE_BASE ** (-jnp.arange(16, dtype=f32) / 16)
    ar, ac = rows[:, None] * inv[None, :], cols[:, None] * inv[None, :]
    cos = jnp.concatenate([jnp.cos(ar), jnp.cos(ar), jnp.cos(ac), jnp.cos(ac)], axis=1)
    sin = jnp.concatenate([-jnp.sin(ar), jnp.sin(ar), -jnp.sin(ac), jnp.sin(ac)], axis=1)
    return jnp.concatenate([cos, cos], axis=1), jnp.concatenate([sin, sin], axis=1)


def _tab_spec():
    per = DEC_SEQ // TM
    return pl.BlockSpec((TM, LANE), lambda i, *_: (i % per, 0))


def _mla_pre_kernel(rope, x_ref, mod_ref, wqa_ref, qn_ref, wqb_ref, wkva_ref, kvn_ref, wkvb_ref, *rest):
    if rope:
        cos_ref, sin_ref, qlat_ref, qctx_ref, kcat_ref, v_ref = rest
    else:
        qctx_ref, kcat_ref, v_ref, ckv_ref, kpe_ref = rest
    m = mod_ref[...]

    def front(sl):
        h = (x_ref[sl, :] * (1.0 + m[1:2]) + m[0:1]).astype(bf16)
        return _dot(h, wqa_ref[...]), _dot(h, wkva_ref[...])

    subs = _subs()
    nxt = front(subs[0])
    for s, sl in enumerate(subs):
        qa, kv = nxt
        if s + 1 < len(subs):
            nxt = front(subs[s + 1])
        qa = _rms_rows(qa, qn_ref[...])
        q = _dot(qa.astype(bf16), wqb_ref[...])
        ckv = _rms_rows(kv[:, :MLA_KV_LORA], kvn_ref[...])
        kpe2 = kv[:, MLA_KV_LORA:MLA_KV_LORA + LANE]
        kvb = _dot(ckv.astype(bf16), wkvb_ref[...])
        q_pe = q[:, MLA_HEADS * MLA_NOPE:]
        if rope:
            cos, sin = cos_ref[sl, :], sin_ref[sl, :]
            q_pe_rot = _rope(q_pe, cos, sin).astype(bf16)
            kpe2 = _rope(kpe2, cos, sin)
        else:
            ckv_ref[sl, :] = ckv
            kpe_ref[sl, :] = kv[:, MLA_KV_LORA:MLA_KV_LORA + MLA_ROPE]
        q_pe = q_pe.astype(bf16)
        qn = q[:, :MLA_HEADS * MLA_NOPE].astype(bf16)
        kn = kvb[:, :MLA_HEADS * MLA_NOPE].astype(bf16)
        v_ref[sl, :] = kvb[:, MLA_HEADS * MLA_NOPE:].astype(bf16)
        lane = lax.broadcasted_iota(i32, kpe2.shape, 1)
        kpe_lo = jnp.where(lane < MLA_ROPE, kpe2, 0.0).astype(bf16)
        kpe_hi = jnp.where(lane >= MLA_ROPE, kpe2, 0.0).astype(bf16)
        for hh in range(MLA_HEADS):
            a, b = hh * 256, hh * 256 + LANE
            pr = (hh // 2) * LANE
            qctx_ref[sl, a:b] = qn[:, hh * LANE:(hh + 1) * LANE]
            qctx_ref[sl, b:b + LANE] = q_pe[:, pr:pr + LANE]
            if rope:
                qlat_ref[sl, a:b] = qn[:, hh * LANE:(hh + 1) * LANE]
                qlat_ref[sl, b:b + LANE] = q_pe_rot[:, pr:pr + LANE]
            kcat_ref[sl, a:b] = kn[:, hh * LANE:(hh + 1) * LANE]
            kcat_ref[sl, b:b + LANE] = kpe_lo if hh % 2 == 0 else kpe_hi


def _mla_pre(grp, x, mod, w, rope_tabs):
    n = grp.n
    ins = [x, mod] + list(w)
    specs = [_rows(TM, D), _mod_spec(grp)] + [_const(a.shape) for a in w]
    wide = jax.ShapeDtypeStruct((n, 2 * D), bf16)
    if rope_tabs is not None:
        ins += list(rope_tabs)
        specs += [_tab_spec(), _tab_spec()]
        shapes = (wide, wide, wide, jax.ShapeDtypeStruct((n, D), bf16))
        ospecs = (_rows(TM, 2 * D), _rows(TM, 2 * D), _rows(TM, 2 * D), _rows(TM, D))
    else:
        shapes = (wide, wide, jax.ShapeDtypeStruct((n, D), bf16),
                  jax.ShapeDtypeStruct((n, MLA_KV_LORA), f32), jax.ShapeDtypeStruct((n, MLA_ROPE), f32))
        ospecs = (_rows(TM, 2 * D), _rows(TM, 2 * D), _rows(TM, D), _rows(TM, MLA_KV_LORA), _rows(TM, MLA_ROPE))
    return pl.pallas_call(
        functools.partial(_mla_pre_kernel, rope_tabs is not None),
        grid=(n // TM,), in_specs=specs, out_specs=ospecs, out_shape=shapes,
        compiler_params=_cp(48), name="mla_pre",
    )(*ins)


def _mla_ctx_kernel(ckv_ref, kpe2_ref, wkvb_ref, kcat_ref, v_ref):
    kvb = _dot(ckv_ref[...].astype(bf16), wkvb_ref[...])
    kn = kvb[:, :MLA_HEADS * MLA_NOPE].astype(bf16)
    v_ref[...] = kvb[:, MLA_HEADS * MLA_NOPE:].astype(bf16)
    kpe2 = kpe2_ref[...]
    lane = lax.broadcasted_iota(i32, kpe2.shape, 1)
    kpe_lo = jnp.where(lane < MLA_ROPE, kpe2, 0.0).astype(bf16)
    kpe_hi = jnp.where(lane >= MLA_ROPE, kpe2, 0.0).astype(bf16)
    for hh in range(MLA_HEADS):
        a, b = hh * 256, hh * 256 + LANE
        kcat_ref[:, a:b] = kn[:, hh * LANE:(hh + 1) * LANE]
        kcat_ref[:, b:b + LANE] = kpe_lo if hh % 2 == 0 else kpe_hi


def _mla_ctx(ckv, kpe2, wkvb):
    n = ckv.shape[0]
    return pl.pallas_call(
        _mla_ctx_kernel, grid=(n // PAST,),
        in_specs=[_rows(PAST, MLA_KV_LORA), _rows(PAST, LANE), _const(wkvb.shape)],
        out_specs=(_rows(PAST, 2 * D), _rows(PAST, D)),
        out_shape=(jax.ShapeDtypeStruct((n, 2 * D), bf16), jax.ShapeDtypeStruct((n, D), bf16)),
        compiler_params=_cp(24), name="mla_ctx",
    )(ckv, kpe2, wkvb)


CTX_SEQS = 2


def _mla_attn_p_kernel(q_ref, k_ref, v_ref, o_ref):
    items = [(slice(sq * SEQ, (sq + 1) * SEQ), hh) for sq in range(CTX_SEQS) for hh in range(MLA_HEADS)]

    def scores(item):
        rows, hh = item
        return _dot_nt(q_ref[rows, hh * 256:(hh + 1) * 256], k_ref[rows, hh * 256:(hh + 1) * 256])

    nxt = scores(items[0])
    for n, (rows, hh) in enumerate(items):
        s = nxt
        if n + 1 < len(items):
            nxt = scores(items[n + 1])
        e = jnp.exp2((s - jnp.max(s, axis=-1, keepdims=True)) * MLA_C2)
        l = jnp.sum(e, axis=-1, keepdims=True)
        o = _dot(e.astype(bf16), v_ref[rows, hh * LANE:(hh + 1) * LANE]) / l
        o_ref[rows, hh * LANE:(hh + 1) * LANE] = o.astype(bf16)


def _mla_attn_p(q, k, v):
    n = q.shape[0]
    rows = CTX_SEQS * SEQ
    return pl.pallas_call(
        _mla_attn_p_kernel, grid=(n // rows,),
        in_specs=[_rows(rows, 2 * D), _rows(rows, 2 * D), _rows(rows, D)],
        out_specs=_rows(rows, D), out_shape=jax.ShapeDtypeStruct((n, D), bf16),
        compiler_params=_cp(32), name="mla_attn_ctx",
    )(q, k, v)


MLA_QT = 256


def _mla_attn_s_kernel(ql_ref, qc_ref, kl_ref, vl_ref, kc_ref, vc_ref, o_ref):
    def scores(hh):
        a, b = hh * 256, (hh + 1) * 256
        return (_dot_nt(ql_ref[:, a:b], kl_ref[:, a:b]), _dot_nt(qc_ref[:, a:b], kc_ref[:, a:b]))

    nxt = scores(0)
    for hh in range(MLA_HEADS):
        s1, s2 = nxt
        if hh + 1 < MLA_HEADS:
            nxt = scores(hh + 1)
        m = jnp.maximum(jnp.max(s1, axis=-1, keepdims=True), jnp.max(s2, axis=-1, keepdims=True))
        e1, e2 = jnp.exp2((s1 - m) * MLA_C2), jnp.exp2((s2 - m) * MLA_C2)
        l = jnp.sum(e1, axis=-1, keepdims=True) + jnp.sum(e2, axis=-1, keepdims=True)
        o = (_dot(e1.astype(bf16), vl_ref[:, hh * LANE:(hh + 1) * LANE])
             + _dot(e2.astype(bf16), vc_ref[:, hh * LANE:(hh + 1) * LANE])) / l
        o_ref[:, hh * LANE:(hh + 1) * LANE] = o.astype(bf16)


def _mla_attn_s(ql, qc, kl, vl, kc, vc):
    nq = DEC_SEQ // MLA_QT
    qs = pl.BlockSpec((MLA_QT, 2 * D), lambda b, i: (b * nq + i, 0))
    return pl.pallas_call(
        _mla_attn_s_kernel, grid=(DEC_BATCH, nq),
        in_specs=[qs, qs,
                  pl.BlockSpec((DEC_SEQ, 2 * D), lambda b, i: (b, 0)),
                  pl.BlockSpec((DEC_SEQ, D), lambda b, i: (b, 0)),
                  pl.BlockSpec((PAST, 2 * D), lambda b, i: (b, 0)),
                  pl.BlockSpec((PAST, D), lambda b, i: (b, 0))],
        out_specs=pl.BlockSpec((MLA_QT, D), lambda b, i: (b * nq + i, 0)),
        out_shape=jax.ShapeDtypeStruct((NS_TOK, D), bf16),
        compiler_params=_cp(48, 2), name="mla_attn_lat",
    )(ql, qc, kl, vl, kc, vc)


def _gelu_tanh(x):
    return 0.5 * x * (1.0 + jnp.tanh(math.sqrt(2.0 / math.pi) * (x + 0.044715 * (x * x * x))))


GM_CW = GM_HALF // GM_GROUPS


def _gmlp_kernel(x_ref, mod_ref, lnp_ref, win_ref, bin_ref, vg_ref, vb_ref, ws_ref, bs_ref, wout_ref, bout_ref,
                 rwt_ref, x1_ref, h2_ref, aff_ref, gated_ref, vz_ref):
    m, lnp = mod_ref[...], lnp_ref[...]
    for s, sl in enumerate(_subs()):
        x = x_ref[sl, :]
        h = (x * (1.0 + m[1:2]) + m[0:1]).astype(bf16)

        def mm(j):
            return _dot(h, win_ref[:, j * GM_CW:(j + 1) * GM_CW]) + bin_ref[:, j * GM_CW:(j + 1) * GM_CW]

        s1 = jnp.zeros((SUB, 1), f32)
        s2 = jnp.zeros((SUB, 1), f32)
        nxt = mm(GM_GROUPS)
        for g in range(GM_GROUPS):
            cur = nxt
            nxt = mm(GM_GROUPS + g + 1) if g + 1 < GM_GROUPS else mm(0)
            z = _gelu_tanh(cur)
            s1 = s1 + jnp.sum(z, axis=-1, keepdims=True)
            s2 = s2 + jnp.sum(z * z, axis=-1, keepdims=True)
            vz_ref[s, :, g * GM_CW:(g + 1) * GM_CW] = z
        mu = s1 * (1.0 / GM_HALF)
        rstd = lax.rsqrt(s2 * (1.0 / GM_HALF) - mu * mu + LN_EPS)
        for g in range(GM_GROUPS):
            cur = nxt
            if g + 1 < GM_GROUPS:
                nxt = mm(g + 1)
            cols = slice(g * GM_CW, (g + 1) * GM_CW)
            u = _gelu_tanh(cur)
            v = ((vz_ref[s, :, cols] - mu) * rstd * vg_ref[:, cols] + vb_ref[:, cols]).astype(bf16)
            for c in range(SUB // GM_CHUNK):
                r0, r1 = c * GM_CHUNK, (c + 1) * GM_CHUNK
                sv = _dot(ws_ref[g], v[r0:r1, :]) + bs_ref[:, g:g + 1]
                gated_ref[sl.start + r0:sl.start + r1, cols] = (u[r0:r1, :] * sv).astype(bf16)
        d = _dot(gated_ref[sl, :], wout_ref[...]) + bout_ref[...]
        _close(d, x, m, lnp, rwt_ref, x1_ref, h2_ref, aff_ref, sl)


def _gmlp_layer(grp, x, mod, lnp, w, rwt):
    shapes, specs = _close_outs(grp.n)
    return pl.pallas_call(
        _gmlp_kernel, grid=(grp.n // TM,),
        in_specs=[_rows(TM, D), _mod_spec(grp), _const((8, D))] + [_const(a.shape) for a in w]
        + [_const((D, LANE))],
        out_specs=specs, out_shape=shapes,
        scratch_shapes=[pltpu.VMEM((TM, GM_HALF), bf16), pltpu.VMEM((TM // SUB, SUB, GM_HALF), f32)],
        compiler_params=_cp(56), name="gmlp_layer",
    )(x, mod, lnp, *w, rwt)


HALO = 16
CONV_RB = 64
CONV_LW = 256


def _conv_glu_kernel(x_ref, mod_ref, w_ref, b_ref, a_ref):
    m = mod_ref[...]

    def front(sl):
        h = (x_ref[sl, :] * (1.0 + m[1:2]) + m[0:1]).astype(bf16)
        return _dot(h, w_ref[...])

    subs = _subs()
    nxt = front(subs[0])
    for s, sl in enumerate(subs):
        a = nxt + b_ref[...]
        if s + 1 < len(subs):
            nxt = front(subs[s + 1])
        a_ref[sl, :] = a[:, :D] * jax.nn.sigmoid(a[:, D:])


def _conv_glu(grp, x, mod, w, b):
    return pl.pallas_call(
        _conv_glu_kernel, grid=(grp.n // TM,),
        in_specs=[_rows(TM, D), _mod_spec(grp), _const(w.shape), _const(b.shape)],
        out_specs=_rows(TM, D), out_shape=jax.ShapeDtypeStruct((grp.n, D), f32),
        compiler_params=_cp(40), name="conv_glu",
    )(x, mod, w, b)


def _conv_close_kernel(seq_subs, ap_ref, a_ref, an_ref, x_ref, mod_ref, lnp_ref, wdw_ref, bdw_ref, ng_ref, nb_ref,
                       w2_ref, b2_ref, rwt_ref, x1_ref, h2_ref, aff_ref, pad_ref, act_ref, cout_ref, shift_ref):
    i = pl.program_id(0)
    nsub = TM // SUB
    bdw, ng, nb = bdw_ref[...], ng_ref[...], nb_ref[...]
    m, lnp = mod_ref[...], lnp_ref[...]
    for s, sl in enumerate(_subs()):
        gsub = i * nsub + s
        prev = a_ref[sl.start - HALO:sl.start, :] if s > 0 else ap_ref[...]
        nxt = a_ref[sl.stop:sl.stop + HALO, :] if s < nsub - 1 else an_ref[...]
        pad = pad_ref.at[s]
        pad[0:HALO, :] = jnp.where((gsub % seq_subs) != 0, prev, 0.0)
        pad[HALO:HALO + SUB, :] = a_ref[sl, :]
        pad[HALO + SUB:, :] = jnp.where((gsub % seq_subs) != seq_subs - 1, nxt, 0.0)
        span = SUB + 24
        for r in range(1, 8):
            shift_ref[s, r - 1, :, :] = pad[r:r + span, :]
        for rb in range(SUB // CONV_RB):
            r0 = rb * CONV_RB
            for lc in range(D // CONV_LW):
                lanes = slice(lc * CONV_LW, (lc + 1) * CONV_LW)
                acc = jnp.zeros((CONV_RB, CONV_LW), f32)
                for k in range(CONV_W):
                    mm, r = (k + 1) // 8, (k + 1) % 8
                    rows = slice(r0 + 8 * mm, r0 + 8 * mm + CONV_RB)
                    win = pad[rows, lanes] if r == 0 else shift_ref[s, r - 1, rows, lanes]
                    acc = acc + wdw_ref[k:k + 1, lanes] * win
                cout_ref[s, r0:r0 + CONV_RB, lanes] = acc
            y = _silu(_ln_rows(cout_ref[s, r0:r0 + CONV_RB, :] + bdw, ng, nb))
            act_ref[sl.start + r0:sl.start + r0 + CONV_RB, :] = y.astype(bf16)
        d = _dot(act_ref[sl, :], w2_ref[...]) + b2_ref[...]
        _close(d, x_ref[sl, :], m, lnp, rwt_ref, x1_ref, h2_ref, aff_ref, sl)


def _conv_close(grp, a, x, mod, lnp, w, rwt):
    n = grp.n
    hb = TM // HALO
    last = n // HALO - 1
    shapes, specs = _close_outs(n)
    return pl.pallas_call(
        functools.partial(_conv_close_kernel, grp.seq // SUB), grid=(n // TM,),
        in_specs=[pl.BlockSpec((HALO, D), lambda i: (jnp.maximum(i * hb - 1, 0), 0)),
                  _rows(TM, D),
                  pl.BlockSpec((HALO, D), lambda i: (jnp.minimum((i + 1) * hb, last), 0)),
                  _rows(TM, D), _mod_spec(grp), _const((8, D))]
        + [_const(t.shape) for t in w] + [_const((D, LANE))],
        out_specs=specs, out_shape=shapes,
        scratch_shapes=[pltpu.VMEM((TM // SUB, SUB + 2 * HALO, D), f32), pltpu.VMEM((TM, D), bf16),
                        pltpu.VMEM((TM // SUB, SUB, D), f32), pltpu.VMEM((TM // SUB, 7, SUB + 24, D), f32)],
        compiler_params=_cp(48), name="conv_close",
    )(a, a, a, x, mod, lnp, *w, rwt)


def _swa_pre_kernel(rope, x_ref, mod_ref, wq_ref, wk_ref, wv_ref, *rest):
    if rope:
        cos_ref, sin_ref, qrot_ref, qraw_ref, kd_ref, vd_ref = rest
    else:
        qraw_ref, kd_ref, vd_ref, k_ref, v_ref = rest
    m = mod_ref[...]

    def front(sl):
        h = (x_ref[sl, :] * (1.0 + m[1:2]) + m[0:1]).astype(bf16)
        return _dot(h, wq_ref[...]), _dot(h, wk_ref[...]), _dot(h, wv_ref[...])

    subs = _subs()
    nxt = front(subs[0])
    for s, sl in enumerate(subs):
        q, kd, vd = nxt
        if s + 1 < len(subs):
            nxt = front(subs[s + 1])
        qraw_ref[sl, :] = q.astype(bf16)
        vd_ref[sl, :] = vd.astype(bf16)
        if rope:
            cos, sin = cos_ref[sl, :], sin_ref[sl, :]
            qrot_ref[sl, :] = _rope(q, cos, sin).astype(bf16)
            kd_ref[sl, :] = _rope(kd, cos, sin).astype(bf16)
        else:
            kd_ref[sl, :] = kd.astype(bf16)
            lane = lax.broadcasted_iota(i32, (SUB, LANE), 1)
            for j in range(SWA_KV // 2):
                lo, hi = 2 * j * LANE, (2 * j + 1) * LANE
                k_ref[sl, j * LANE:(j + 1) * LANE] = jnp.where(lane < SWA_HD, kd[:, lo:lo + LANE], kd[:, hi:hi + LANE])
                v_ref[sl, j * LANE:(j + 1) * LANE] = jnp.where(lane < SWA_HD, vd[:, lo:lo + LANE], vd[:, hi:hi + LANE])


def _swa_pre(grp, x, mod, w, rope_tabs):
    n = grp.n
    ins = [x, mod] + list(w)
    specs = [_rows(TM, D), _mod_spec(grp)] + [_const(a.shape) for a in w]
    kw = 2 * SWA_KV * SWA_HD
    qs, ks = jax.ShapeDtypeStruct((n, D), bf16), jax.ShapeDtypeStruct((n, kw), bf16)
    if rope_tabs is not None:
        ins += list(rope_tabs)
        specs += [_tab_spec(), _tab_spec()]
        shapes = (qs, qs, ks, ks)
        ospecs = (_rows(TM, D), _rows(TM, D), _rows(TM, kw), _rows(TM, kw))
    else:
        nat = jax.ShapeDtypeStruct((n, SWA_KV * SWA_HD), f32)
        shapes = (qs, ks, ks, nat, nat)
        ospecs = (_rows(TM, D), _rows(TM, kw), _rows(TM, kw), _rows(TM, SWA_KV * SWA_HD), _rows(TM, SWA_KV * SWA_HD))
    return pl.pallas_call(
        functools.partial(_swa_pre_kernel, rope_tabs is not None),
        grid=(n // TM,), in_specs=specs, out_specs=ospecs, out_shape=shapes,
        compiler_params=_cp(40), name="swa_pre",
    )(*ins)


def _half_mask(x, parity):
    lane = lax.broadcasted_iota(i32, x.shape, 1)
    keep = (lane < SWA_HD) if parity == 0 else (lane >= SWA_HD)
    return jnp.where(keep, x, jnp.zeros_like(x))


def _swa_attn_p_kernel(sink_ref, q_ref, kd_ref, vd_ref, o_ref):
    items = [(slice(sq * SEQ, (sq + 1) * SEQ), hd) for sq in range(CTX_SEQS) for hd in range(SWA_HEADS)]

    def scores(item):
        rows, hd = item
        pair, par = hd // 2, hd % 2
        g = hd // (SWA_HEADS // SWA_KV)
        return _dot_nt(q_ref[rows, pair * LANE:(pair + 1) * LANE], _half_mask(kd_ref[rows, g * LANE:(g + 1) * LANE], par))

    nxt = scores(items[0])
    acc = None
    for n, (rows, hd) in enumerate(items):
        pair, par = hd // 2, hd % 2
        g = hd // (SWA_HEADS // SWA_KV)
        s = nxt
        if n + 1 < len(items):
            nxt = scores(items[n + 1])
        sk = sink_ref[hd] * (1.0 / SWA_SCALE)
        m = jnp.maximum(jnp.max(s, axis=-1, keepdims=True), sk)
        e = jnp.exp2((s - m) * SWA_C2)
        l = jnp.sum(e, axis=-1, keepdims=True) + jnp.exp2((sk - m) * SWA_C2)
        o = _dot(e.astype(bf16), _half_mask(vd_ref[rows, g * LANE:(g + 1) * LANE], par)) / l
        acc = o if par == 0 else acc + o
        if par == 1:
            o_ref[rows, pair * LANE:(pair + 1) * LANE] = acc.astype(bf16)


def _swa_attn_p(sink, q, kd, vd):
    n = q.shape[0]
    kw = kd.shape[1]
    rows = CTX_SEQS * SEQ
    return pl.pallas_call(
        _swa_attn_p_kernel,
        grid_spec=pltpu.PrefetchScalarGridSpec(
            num_scalar_prefetch=1, grid=(n // rows,),
            in_specs=[_rows(rows, D), _rows(rows, kw), _rows(rows, kw)],
            out_specs=_rows(rows, D)),
        out_shape=jax.ShapeDtypeStruct((n, D), bf16),
        compiler_params=_cp(32), name="swa_attn_ctx",
    )(sink, q, kd, vd)


SWA_QB = 128


def _swa_attn_s_kernel(sink_ref, qr_ref, qw_ref, kd_ref, vd_ref, kc_ref, vc_ref, o_ref):
    nblk = pl.program_id(1)
    span = 3 * SWA_QB
    start = pl.multiple_of(jnp.clip((nblk - 1) * SWA_QB, 0, DEC_SEQ - span), SWA_QB)
    qpos = nblk * SWA_QB + lax.broadcasted_iota(i32, (SWA_QB, span), 0)
    kpos = start + lax.broadcasted_iota(i32, (SWA_QB, span), 1)
    band = jnp.abs(kpos - qpos) <= SWA_WIN

    def scores(hd):
        pair, par = hd // 2, hd % 2
        g = hd // (SWA_HEADS // SWA_KV)
        cols = slice(g * LANE, (g + 1) * LANE)
        s1 = _dot_nt(qr_ref[:, pair * LANE:(pair + 1) * LANE], _half_mask(kd_ref[pl.ds(start, span), cols], par))
        s2 = _dot_nt(qw_ref[:, pair * LANE:(pair + 1) * LANE], _half_mask(kc_ref[:, cols], par))
        return s1, s2

    nxt = scores(0)
    acc = None
    for hd in range(SWA_HEADS):
        pair, par = hd // 2, hd % 2
        g = hd // (SWA_HEADS // SWA_KV)
        cols = slice(g * LANE, (g + 1) * LANE)
        s1, s2 = nxt
        if hd + 1 < SWA_HEADS:
            nxt = scores(hd + 1)
        s1 = jnp.where(band, s1, NEG_INF)
        sk = sink_ref[hd] * (1.0 / SWA_SCALE)
        m = jnp.maximum(jnp.maximum(jnp.max(s1, axis=-1, keepdims=True), jnp.max(s2, axis=-1, keepdims=True)), sk)
        e1, e2 = jnp.exp2((s1 - m) * SWA_C2), jnp.exp2((s2 - m) * SWA_C2)
        l = (jnp.sum(e1, axis=-1, keepdims=True) + jnp.sum(e2, axis=-1, keepdims=True)
             + jnp.exp2((sk - m) * SWA_C2))
        o = (_dot(e1.astype(bf16), _half_mask(vd_ref[pl.ds(start, span), cols], par))
             + _dot(e2.astype(bf16), _half_mask(vc_ref[:, cols], par))) / l
        acc = o if par == 0 else acc + o
        if par == 1:
            o_ref[:, pair * LANE:(pair + 1) * LANE] = acc.astype(bf16)


def _swa_attn_s(sink, qr, qw, kd, vd, kc, vc):
    nq = DEC_SEQ // SWA_QB
    kw = kd.shape[1]
    qs = pl.BlockSpec((SWA_QB, D), lambda b, i, *_: (b * nq + i, 0))
    full = pl.BlockSpec((DEC_SEQ, kw), lambda b, i, *_: (b, 0))
    ctx = pl.BlockSpec((PAST, kw), lambda b, i, *_: (b, 0))
    return pl.pallas_call(
        _swa_attn_s_kernel,
        grid_spec=pltpu.PrefetchScalarGridSpec(
            num_scalar_prefetch=1, grid=(DEC_BATCH, nq),
            in_specs=[qs, qs, full, full, ctx, ctx],
            out_specs=pl.BlockSpec((SWA_QB, D), lambda b, i, *_: (b * nq + i, 0))),
        out_shape=jax.ShapeDtypeStruct((NS_TOK, D), bf16),
        compiler_params=_cp(32, 2), name="swa_attn_lat",
    )(sink, qr, qw, kd, vd, kc, vc)


def _excl_prefix(mask_f, nb, tri, blk):
    m2 = mask_f.reshape(nb * NE, RT)
    within = _dot(m2.astype(bf16), tri)
    tot = jnp.sum(m2, axis=1, keepdims=True)
    totb = jnp.broadcast_to(tot, (nb * NE, LANE)).astype(bf16)
    offs = _dot(blk, totb)
    return (within + offs[:, 0:1]).reshape(nb, NE, RT), offs.reshape(nb, NE, LANE)


def _route_kernel(cap, nb, aff_ref, dest_ref, gate_ref, ws_ref, nq_ref, destt_ref):
    a = aff_ref[...]

    def as_f32(bits):
        return pltpu.bitcast(bits, f32)[None]

    def count_ge(th):
        c = jnp.sum(jnp.where(a >= as_f32(th), 1.0, 0.0), axis=0)
        return jnp.sum(c, axis=1, keepdims=True)

    def body(_, c):
        lo, hi = c
        mid = lo + ((hi - lo + 1) >> 1)
        ok = count_ge(mid) >= cap
        return jnp.where(ok, mid, lo), jnp.where(ok, hi, mid - 1)

    lo0 = jnp.zeros((NE, 1), i32)
    hi0 = jnp.full((NE, 1), 0x7F800000, i32)
    thr, _ = lax.fori_loop(0, 31, body, (lo0, hi0))

    r = lax.broadcasted_iota(i32, (RT, RT), 0)
    c = lax.broadcasted_iota(i32, (RT, RT), 1)
    tri = jnp.where(r < c, 1.0, 0.0).astype(bf16)
    rr = lax.broadcasted_iota(i32, (nb * NE, nb * NE), 0)
    cc = lax.broadcasted_iota(i32, (nb * NE, nb * NE), 1)
    blk = jnp.where(((rr % NE) == (cc % NE)) & (cc < rr), 1.0, 0.0).astype(bf16)

    gt = a > as_f32(thr)
    eq = a == as_f32(thr)
    n_gt = jnp.sum(jnp.sum(jnp.where(gt, 1.0, 0.0), axis=0), axis=1, keepdims=True)
    need = cap - n_gt
    tie_rank, _ = _excl_prefix(jnp.where(eq, 1.0, 0.0), nb, tri, blk)
    sel = gt | (eq & (tie_rank < need[None]))
    pos, offs = _excl_prefix(jnp.where(sel, 1.0, 0.0), nb, tri, blk)
    dest_ref[...] = jnp.where(sel, pos, -1.0).astype(i32)
    gate_ref[...] = jnp.where(sel, a, 0.0)
    ct_end = jnp.concatenate([offs[1:], jnp.full((1, NE, LANE), float(cap), f32)], axis=0)
    ws = jnp.floor(offs * (1.0 / 16)) * 16.0
    ws_ref[...] = ws.astype(i32)
    nchunk = jnp.floor((ct_end - ws + (WIN - 1)) * (1.0 / WIN))
    nq_ref[...] = jnp.broadcast_to(jnp.max(nchunk, axis=1, keepdims=True), (nb, NE, LANE)).astype(i32)
    destp = jnp.where(sel, pos + 1.0, 0.0)
    zpad = jnp.zeros((LANE - NE, RT), f32)
    for b in range(nb):
        destt_ref[b * RT:(b + 1) * RT, :] = jnp.transpose(jnp.concatenate([destp[b], zpad], axis=0))


def _route(grp, aff3):
    nb = grp.n // RT
    full = pl.BlockSpec((nb, NE, RT), lambda: (0, 0, 0))
    small = pl.BlockSpec((nb, NE, LANE), lambda: (0, 0, 0))
    dest3, gate3, ws3, nq3, destt = pl.pallas_call(
        functools.partial(_route_kernel, grp.cap, nb),
        in_specs=[full], out_specs=(full, full, small, small, pl.BlockSpec((grp.n, LANE), lambda: (0, 0))),
        out_shape=(jax.ShapeDtypeStruct((nb, NE, RT), i32), jax.ShapeDtypeStruct((nb, NE, RT), f32),
                   jax.ShapeDtypeStruct((nb, NE, LANE), i32), jax.ShapeDtypeStruct((nb, NE, LANE), i32),
                   jax.ShapeDtypeStruct((grp.n, LANE), f32)),
        compiler_params=pltpu.CompilerParams(vmem_limit_bytes=48 * MIB), name="route",
    )(aff3)
    return dest3, gate3, ws3[:, :, 0], nq3[:, 0, 0], destt


def _dispatch_kernel(cap, eg, ws_ref, nq_ref, h2_ref, dest_ref, gate_ref, xe_ref, gcol_ref, hot_ref):
    g, step = pl.program_id(0), pl.program_id(1)

    @pl.when(step == 0)
    def _():
        xe_ref[...] = jnp.zeros_like(xe_ref)
        gcol_ref[...] = jnp.zeros_like(gcol_ref)

    wrow = lax.broadcasted_iota(i32, (WIN, RT), 0)
    for b in range(DISPATCH_BLOCKS):
        tb = step * DISPATCH_BLOCKS + b
        hot = hot_ref.at[b]

        def body(q, carry, tb=tb, b=b, hot=hot):
            starts, gsums = [], []
            for el in range(eg):
                e = g * eg + el
                want = ws_ref[tb, e] + q * WIN
                st = pl.multiple_of(jnp.minimum(want, cap - WIN), 16)
                row = st + wrow
                hit = dest_ref[b, pl.ds(e, 1), :] == jnp.where(row >= want, row, -7)
                hot[el * WIN:(el + 1) * WIN, :] = jnp.where(hit, 1.0, 0.0).astype(bf16)
                gsums.append(jnp.sum(jnp.where(hit, gate_ref[b, pl.ds(e, 1), :], 0.0), axis=1, keepdims=True))
                starts.append(st)
            part = _dot(hot[...], h2_ref[b * RT:(b + 1) * RT, :])
            for el in range(eg):
                dst = pl.ds(el * cap + starts[el], WIN)
                xe_ref[dst, :] = xe_ref[dst, :] + part[el * WIN:(el + 1) * WIN, :].astype(bf16)
                gcol_ref[dst, :] = gcol_ref[dst, :] + jnp.broadcast_to(gsums[el], (WIN, LANE))
            return carry

        lax.fori_loop(0, nq_ref[tb], body, 0)


def _dispatch(grp, wstart, nq, h2, dest3, gate3):
    eg = ROWS_PER_PASS // grp.cap
    nb = grp.n // (RT * DISPATCH_BLOCKS)
    tab = pl.BlockSpec((DISPATCH_BLOCKS, NE, RT), lambda g, t, *_: (t, 0, 0))
    return pl.pallas_call(
        functools.partial(_dispatch_kernel, grp.cap, eg),
        grid_spec=pltpu.PrefetchScalarGridSpec(
            num_scalar_prefetch=2, grid=(NE // eg, nb),
            in_specs=[pl.BlockSpec((RT * DISPATCH_BLOCKS, D), lambda g, t, *_: (t, 0)), tab, tab],
            out_specs=(pl.BlockSpec((ROWS_PER_PASS, D), lambda g, t, *_: (g, 0)),
                       pl.BlockSpec((ROWS_PER_PASS, LANE), lambda g, t, *_: (g, 0))),
            scratch_shapes=[pltpu.VMEM((DISPATCH_BLOCKS, eg * WIN, RT), bf16)]),
        out_shape=(jax.ShapeDtypeStruct((NE * grp.cap, D), bf16), jax.ShapeDtypeStruct((NE * grp.cap, LANE), f32)),
        compiler_params=_cp(56, 2), name="moe_dispatch",
    )(wstart, nq, h2, dest3, gate3)


FFN_RB = 256


def _ffn_kernel(xp_ref, gp_ref, xs_ref, gs_ref, wg_ref, wu_ref, wd_ref, yp_ref, ys_ref, wgb, wub, wdb):
    wgb[...] = wg_ref[...].astype(bf16)
    wub[...] = wu_ref[...].astype(bf16)
    wdb[...] = wd_ref[...].astype(bf16)
    for x_ref, g_ref, y_ref, cap in ((xp_ref, gp_ref, yp_ref, CTX.cap), (xs_ref, gs_ref, ys_ref, LAT.cap)):
        for r in range(cap // FFN_RB):
            sl = slice(r * FFN_RB, (r + 1) * FFN_RB)
            x = x_ref[sl, :]
            hid = (_silu(_dot(x, wgb[...])) * _dot(x, wub[...])).astype(bf16)
            gate = jnp.concatenate([g_ref[sl, :]] * (D // LANE), axis=1)
            y_ref[sl, :] = (_dot(hid, wdb[...]) * gate).astype(bf16)


def _ffn(layer, xp, gp, xs, gs, wg, wu, wd):
    wspec = pl.BlockSpec((None, None, D, FF), lambda e: (layer, e, 0, 0))
    cp, cs = CTX.cap, LAT.cap
    return pl.pallas_call(
        _ffn_kernel, grid=(NE,),
        in_specs=[pl.BlockSpec((cp, D), lambda e: (e, 0)), pl.BlockSpec((cp, LANE), lambda e: (e, 0)),
                  pl.BlockSpec((cs, D), lambda e: (e, 0)), pl.BlockSpec((cs, LANE), lambda e: (e, 0)),
                  wspec, wspec, pl.BlockSpec((None, None, FF, D), lambda e: (layer, e, 0, 0))],
        out_specs=(pl.BlockSpec((cp, D), lambda e: (e, 0)), pl.BlockSpec((cs, D), lambda e: (e, 0))),
        out_shape=(jax.ShapeDtypeStruct((NE * cp, D), bf16), jax.ShapeDtypeStruct((NE * cs, D), bf16)),
        scratch_shapes=[pltpu.VMEM((D, FF), bf16), pltpu.VMEM((D, FF), bf16), pltpu.VMEM((FF, D), bf16)],
        compiler_params=_cp(56), name="moe_ffn",
    )(xp, gp, xs, gs, wg, wu, wd)


def _combine_kernel(cap, ws_ref, nq_ref, yg_ref, dt_ref, x1_ref, mod_ref, lnp_ref, x2_ref, win_ref, y_ref):
    step = pl.program_id(0)
    kw = NE * WIN
    lane = lax.broadcasted_iota(i32, (1, kw), 1)
    lane_e, lane_w = lane // WIN, lane % WIN
    er = lax.broadcasted_iota(i32, (2 * LANE, kw), 0)
    ec = lax.broadcasted_iota(i32, (2 * LANE, kw), 1) // WIN
    spread = jnp.where(er == ec, 32.0, jnp.where(er - LANE == ec, 1.0, 0.0)).astype(bf16)

    def spread_rows(rows):
        d = dt_ref[rows, :]
        hi = jnp.floor(d * (1.0 / 32))
        hilo = jnp.concatenate([hi, d - 32.0 * hi], axis=1).astype(bf16)
        return _dot(hilo, spread)

    def chunk(b, tb, q, destp):
        win = win_ref.at[b]
        tgt = jnp.full((1, kw), -1, i32)
        for e in range(NE):
            want = ws_ref[tb, e] + q * WIN
            st = pl.multiple_of(jnp.minimum(want, cap - WIN), 16)
            win[e * WIN:(e + 1) * WIN, :] = yg_ref[pl.ds(e * cap + st, WIN), :]
            row = st + lane_w
            tgt = jnp.where(lane_e == e, jnp.where(row >= want, row + 1, -1), tgt)
        onehot = jnp.where(destp == tgt.astype(f32), 1.0, 0.0).astype(bf16)
        return _dot(onehot, win[...])

    blocks = [(b, step * BLOCKS_PER_STEP + b, slice(b * RT, (b + 1) * RT)) for b in range(BLOCKS_PER_STEP)]
    m, lnp = mod_ref[...], lnp_ref[...]

    def close(rows, y):
        x2_ref[rows, :] = _ln_rows(ALPHA * x1_ref[rows, :] + m[5:6] * y, lnp[2:3], lnp[3:4])

    ys = []
    for b, tb, rows in blocks:
        ys.append(chunk(b, tb, 0, spread_rows(rows)))
    for (b, tb, rows), y in zip(blocks, ys):
        y_ref[b] = y
        close(rows, y)
    for b, tb, rows in blocks:
        @pl.when(nq_ref[tb] > 1)
        def _(b=b, tb=tb, rows=rows):
            destp = spread_rows(rows)

            def body(q, carry):
                y_ref[b] = y_ref[b] + chunk(b, tb, q, destp)
                return carry

            lax.fori_loop(1, nq_ref[tb], body, 0)
            close(rows, y_ref[b])


def _combine(grp, wstart, nq, yg, destab, x1, mod, lnp):
    rt = RT * BLOCKS_PER_STEP
    seg = pl.BlockSpec((None, 8, D), lambda i, *_: (grp.seg0 + (i * rt) // grp.seq if grp.seq > rt else grp.seg0, 0, 0))
    return pl.pallas_call(
        functools.partial(_combine_kernel, grp.cap),
        grid_spec=pltpu.PrefetchScalarGridSpec(
            num_scalar_prefetch=2, grid=(grp.n // rt,),
            in_specs=[_const(yg.shape), _rows(rt, LANE), _rows(rt, D), seg, _const((8, D))],
            out_specs=_rows(rt, D),
            scratch_shapes=[pltpu.VMEM((BLOCKS_PER_STEP, NE * WIN, D), bf16), pltpu.VMEM((BLOCKS_PER_STEP, RT, D), f32)]),
        out_shape=jax.ShapeDtypeStruct((grp.n, D), f32),
        compiler_params=_cp(52), name="moe_combine",
    )(wstart, nq, yg, destab, x1, mod, lnp)


def _moe(layer, closed, mod, lnp, wg, wu, wd):
    disp = []
    for grp, (x1, h2, aff3) in zip((CTX, LAT), closed):
        dest3, gate3, wstart, nq, destab = _route(grp, aff3)
        xe, gcol = _dispatch(grp, wstart, nq, h2, dest3, gate3)
        disp.append((xe, gcol, wstart, nq, destab, x1))
    ys = _ffn(layer, disp[0][0], disp[0][1], disp[1][0], disp[1][1], wg, wu, wd)
    return tuple(_combine(grp, d[2], d[3], y, d[4], d[5], mod, lnp)
                 for grp, d, y in zip((CTX, LAT), disp, ys))


def _mla_weights(wq_a, q_norm, wq_b, wkv_a, kv_norm, wkv_b):
    wqb = wq_b.reshape(MLA_Q_LORA, MLA_HEADS, MLA_NOPE + MLA_ROPE)
    wqb = jnp.concatenate([wqb[:, :, :MLA_NOPE].reshape(MLA_Q_LORA, -1), wqb[:, :, MLA_NOPE:].reshape(MLA_Q_LORA, -1)], axis=1)
    wkva = jnp.concatenate([wkv_a, wkv_a[:, MLA_KV_LORA:]], axis=1)
    wkvb = wkv_b.reshape(MLA_KV_LORA, MLA_HEADS, MLA_NOPE + MLA_V)
    wkvb = jnp.concatenate([wkvb[:, :, :MLA_NOPE].reshape(MLA_KV_LORA, -1), wkvb[:, :, MLA_NOPE:].reshape(MLA_KV_LORA, -1)], axis=1)
    return (wq_a.astype(bf16), q_norm.reshape(1, -1), wqb.astype(bf16), wkva.astype(bf16), kv_norm.reshape(1, -1),
            wkvb.astype(bf16))


def _dup_heads(w):
    w = w.reshape(w.shape[0], SWA_KV, 1, SWA_HD)
    return jnp.broadcast_to(w, (w.shape[0], SWA_KV, 2, SWA_HD)).reshape(w.shape[0], 2 * SWA_KV * SWA_HD)


def kernel(x_prompt, x_sample, c, cache_mla_ckv, cache_mla_kpe, cache_swa_k, cache_swa_v, c_ctx, w_mod, b_mod, ln_gain, ln_bias, router_w, moe_w_gate, moe_w_up, moe_w_down, mla_wq_a, mla_q_norm, mla_wq_b, mla_wkv_a, mla_kv_norm, mla_wkv_b, mla_wo, gm_w_in, gm_b_in, gm_v_norm_g, gm_v_norm_b, gm_w_s, gm_b_s, gm_w_out, gm_b_out, cv_w_pw1, cv_b_pw1, cv_w_dw, cv_b_dw, cv_norm_g, cv_norm_b, cv_w_pw2, cv_b_pw2, swa_wq, swa_wk, swa_wv, swa_sink, swa_wo):
    groups = (CTX, LAT)
    xs = (x_prompt.reshape(NP_TOK, D), x_sample.reshape(NS_TOK, D))
    cond8 = jnp.concatenate([c_ctx[None, :], c, jnp.zeros((5, D), f32)], axis=0)
    mods = _modulation(cond8, w_mod, b_mod)
    lnps = jnp.concatenate([jnp.stack([ln_gain[:, 0], ln_bias[:, 0], ln_gain[:, 1], ln_bias[:, 1]], axis=1),
                            jnp.zeros((DEPTH, 4, D), f32)], axis=1)
    rwts = jnp.pad(router_w, ((0, 0), (0, 0), (0, LANE - NE)))
    tabs = _rope_tables(DEC_SEQ)

    def moe(i, closed):
        return _moe(i, closed, mods[i], lnps[i], moe_w_gate, moe_w_up, moe_w_down)

    wm = _mla_weights(mla_wq_a[0], mla_q_norm[0], mla_wq_b[0], mla_wkv_a[0], mla_kv_norm[0], mla_wkv_b[0])
    qc_p, kc_p, v_p, ckv_p, kpe_p = _mla_pre(CTX, xs[0], mods[0], wm, None)
    ql_s, qc_s, kc_s, v_s = _mla_pre(LAT, xs[1], mods[0], wm, tabs)
    cache_kpe = cache_mla_kpe[:, 0].reshape(DEC_BATCH * PAST, MLA_ROPE)
    kc_c, v_c = _mla_ctx(cache_mla_ckv[:, 0].reshape(DEC_BATCH * PAST, MLA_KV_LORA),
                         jnp.concatenate([cache_kpe, cache_kpe], axis=1), wm[5])
    os_ = (_mla_attn_p(qc_p, kc_p, v_p), _mla_attn_s(ql_s, qc_s, kc_s, v_s, kc_c, v_c))
    wo = mla_wo[0].astype(bf16)
    xs = moe(0, [_proj_close(g, o, x, mods[0], lnps[0], wo, rwts[0]) for g, o, x in zip(groups, os_, xs)])

    wgm = (gm_w_in[0].astype(bf16), gm_b_in[0].reshape(1, -1), gm_v_norm_g[0].reshape(1, -1),
           gm_v_norm_b[0].reshape(1, -1), gm_w_s[0].astype(bf16), gm_b_s[0].T, gm_w_out[0].astype(bf16),
           gm_b_out[0].reshape(1, -1))
    xs = moe(1, [_gmlp_layer(g, x, mods[1], lnps[1], wgm, rwts[1]) for g, x in zip(groups, xs)])

    w1, b1 = cv_w_pw1[0].astype(bf16), cv_b_pw1[0].reshape(1, -1)
    wcv = (cv_w_dw[0], cv_b_dw[0].reshape(1, -1), cv_norm_g[0].reshape(1, -1), cv_norm_b[0].reshape(1, -1),
           cv_w_pw2[0].astype(bf16), cv_b_pw2[0].reshape(1, -1))
    xs = moe(2, [_conv_close(g, _conv_glu(g, x, mods[2], w1, b1), x, mods[2], lnps[2], wcv, rwts[2])
                 for g, x in zip(groups, xs)])

    wsw = (swa_wq[0].astype(bf16), _dup_heads(swa_wk[0]).astype(bf16), _dup_heads(swa_wv[0]).astype(bf16))
    q_p, kd_p, vd_p, k_p, v_p = _swa_pre(CTX, xs[0], mods[3], wsw, None)
    qr_s, qw_s, kd_s, vd_s = _swa_pre(LAT, xs[1], mods[3], wsw, tabs)

    def dup_cache(t):
        return _dup_heads(t[:, 0].reshape(DEC_BATCH * PAST, SWA_KV * SWA_HD)).astype(bf16)

    sink = swa_sink[0]
    os_ = (_swa_attn_p(sink, q_p, kd_p, vd_p),
           _swa_attn_s(sink, qr_s, qw_s, kd_s, vd_s, dup_cache(cache_swa_k), dup_cache(cache_swa_v)))
    wo = swa_wo[0].astype(bf16)
    xs = moe(3, [_proj_close(g, o, x, mods[3], lnps[3], wo, rwts[3]) for g, o, x in zip(groups, os_, xs)])

    return (xs[0].reshape(BATCH, SEQ, D), xs[1].reshape(DEC_BATCH, DEC_SEQ, D),
            ckv_p.reshape(BATCH, 1, SEQ, MLA_KV_LORA), kpe_p.reshape(BATCH, 1, SEQ, MLA_ROPE),
            k_p.reshape(BATCH, 1, SEQ, SWA_KV, SWA_HD), v_p.reshape(BATCH, 1, SEQ, SWA_KV, SWA_HD))
```

```python
import functools
import math

import jax
import jax.numpy as jnp
from jax import lax
from jax.experimental import pallas as pl
from jax.experimental.pallas import tpu as pltpu

f32 = jnp.float32
bf16 = jnp.bfloat16
i32 = jnp.int32

D = 1024
BATCH, SEQ = 32, 256
DEC_BATCH, DEC_SEQ = 2, 2048
PAST = 256
DEPTH = 4
GRID_W = 64
ALPHA = (2 * DEPTH) ** 0.25
LN_EPS = 1e-5
RMS_EPS = 1e-6
ROPE_BASE = 10000.0
NEG_INF = -1e30
MLA_HEADS, MLA_NOPE, MLA_ROPE, MLA_V = 8, 128, 64, 128
MLA_Q_LORA, MLA_KV_LORA = 384, 256
MLA_SCALE = (MLA_NOPE + MLA_ROPE) ** -0.5
MLA_C2 = MLA_SCALE * math.log2(math.e)
GM_CHUNK, GM_HALF, GM_GROUPS = 128, 2048, 4
CONV_W = 31
SWA_HEADS, SWA_KV, SWA_HD, SWA_WIN = 16, 4, 64, 128
SWA_SCALE = SWA_HD ** -0.5
SWA_C2 = SWA_SCALE * math.log2(math.e)
NE = 16
FF = 1024

NP_TOK = BATCH * SEQ
NS_TOK = DEC_BATCH * DEC_SEQ

LANE = 128
TM = 512
SUB = 256
RT = 256
WIN = 64
BLOCKS_PER_STEP = 2
DISPATCH_BLOCKS = 4
ROWS_PER_PASS = 8192
MIB = 2 ** 20


class Group:
    def __init__(self, n_tok, seq, seg0):
        self.n = n_tok
        self.seq = seq
        self.cap = 2 * n_tok // NE
        self.seg0 = seg0

    def seg_of(self, i):
        return self.seg0 + (i * TM) // self.seq if self.seq > TM else self.seg0


CTX = Group(NP_TOK, SEQ, 0)
LAT = Group(NS_TOK, DEC_SEQ, 1)


def _cp(vmem_mb, n_axes=1):
    return pltpu.CompilerParams(dimension_semantics=("arbitrary",) * n_axes,
                                vmem_limit_bytes=int(vmem_mb * MIB))


def _const(shape):
    nd = len(shape)
    return pl.BlockSpec(shape, lambda *_: (0,) * nd, pipeline_mode=pl.Buffered(1))


def _rows(tm, c):
    return pl.BlockSpec((tm, c), lambda i, *_: (i, 0))


def _mod_spec(grp):
    return pl.BlockSpec((None, 8, D), lambda i, *_: (grp.seg_of(i), 0, 0))


def _dot(a, b):
    return jnp.dot(a, b, preferred_element_type=f32)


def _dot_nt(a, b):
    return lax.dot_general(a, b, (((1,), (1,)), ((), ())), preferred_element_type=f32)


def _silu(x):
    return x * jax.nn.sigmoid(x)


def _ln_rows(x, g, b):
    mu = jnp.mean(x, axis=-1, keepdims=True)
    xc = x - mu
    var = jnp.mean(xc * xc, axis=-1, keepdims=True)
    return xc * lax.rsqrt(var + LN_EPS) * g + b


def _rms_rows(x, g):
    return x * lax.rsqrt(jnp.mean(x * x, axis=-1, keepdims=True) + RMS_EPS) * g


def _split_bf16(x):
    hi = x.astype(bf16)
    lo = (x - hi.astype(f32)).astype(bf16)
    return hi, lo


def _subs():
    return [slice(s * SUB, (s + 1) * SUB) for s in range(TM // SUB)]


def _mod_kernel(c_ref, w_ref, b_ref, o_ref):
    x = _silu(c_ref[...])
    xh, xl = _split_bf16(x)
    wh = w_ref[...].astype(bf16)
    o_ref[...] = _dot(xh, wh) + _dot(xl, wh) + b_ref[...]


def _modulation(cond8, w_mod, b_mod):
    nk = 6
    out = pl.pallas_call(
        _mod_kernel,
        grid=(DEPTH, nk),
        in_specs=[pl.BlockSpec((8, D), lambda l, k: (0, 0)),
                  pl.BlockSpec((None, D, D), lambda l, k: (l, 0, k)),
                  pl.BlockSpec((None, None, 1, D), lambda l, k: (l, k, 0, 0))],
        out_specs=pl.BlockSpec((None, None, 8, D), lambda l, k: (l, k, 0, 0)),
        out_shape=jax.ShapeDtypeStruct((DEPTH, nk, 8, D), f32),
        compiler_params=_cp(32, 2),
        name="modulation",
    )(cond8, w_mod, b_mod.reshape(DEPTH, nk, 1, D))
    out = jnp.transpose(out, (0, 2, 1, 3))
    return jnp.pad(out, ((0, 0), (0, 0), (0, 2), (0, 0)))


def _router_aff_t(h2, rw_ref):
    hh, hl = _split_bf16(h2)
    wh, wl = _split_bf16(rw_ref[...])
    lg = _dot(hh, wh) + _dot(hl, wh) + _dot(hh, wl)
    lg = jnp.transpose(lg)[:NE, :]
    m = jnp.max(lg, axis=0, keepdims=True)
    e = jnp.exp(lg - m)
    return e / jnp.sum(e, axis=0, keepdims=True)


def _close(d, x, m, lnp, rwt_ref, x1_ref, h2_ref, aff_ref, sl):
    x1 = _ln_rows(ALPHA * x + m[2:3] * d, lnp[0:1], lnp[1:2])
    x1_ref[sl, :] = x1
    h2 = x1 * (1.0 + m[4:5]) + m[3:4]
    h2_ref[sl, :] = h2.astype(bf16)
    aff_ref[sl.start // RT] = _router_aff_t(h2, rwt_ref)


def _close_outs(n):
    shapes = (jax.ShapeDtypeStruct((n, D), f32), jax.ShapeDtypeStruct((n, D), bf16),
              jax.ShapeDtypeStruct((n // RT, NE, RT), f32))
    specs = (_rows(TM, D), _rows(TM, D),
             pl.BlockSpec((TM // RT, NE, RT), lambda i, *_: (i, 0, 0)))
    return shapes, specs


def _proj_close_kernel(o_ref, x_ref, mod_ref, lnp_ref, wo_ref, rwt_ref, x1_ref, h2_ref, aff_ref):
    m, lnp = mod_ref[...], lnp_ref[...]
    subs = _subs()
    nxt = _dot(o_ref[subs[0], :], wo_ref[...])
    for s, sl in enumerate(subs):
        d = nxt
        if s + 1 < len(subs):
            nxt = _dot(o_ref[subs[s + 1], :], wo_ref[...])
        _close(d, x_ref[sl, :], m, lnp, rwt_ref, x1_ref, h2_ref, aff_ref, sl)


def _proj_close(grp, o, x, mod, lnp, wo, rwt):
    shapes, specs = _close_outs(grp.n)
    return pl.pallas_call(
        _proj_close_kernel,
        grid=(grp.n // TM,),
        in_specs=[_rows(TM, o.shape[1]), _rows(TM, D), _mod_spec(grp), _const((8, D)),
                  _const(wo.shape), _const((D, LANE))],
        out_specs=specs, out_shape=shapes,
        compiler_params=_cp(40), name="proj_close",
    )(o, x, mod, lnp, wo, rwt)


def _rope(x, cos, sin):
    w = x.shape[1]
    reps = w // LANE
    c = jnp.concatenate([cos] * reps, axis=1) if reps > 1 else cos
    s = jnp.concatenate([sin] * reps, axis=1) if reps > 1 else sin
    lane = lax.broadcasted_iota(i32, x.shape, 1)
    up = pltpu.roll(x, w - 16, 1)
    dn = pltpu.roll(x, 16, 1)
    partner = jnp.where((lane % 32) < 16, up, dn)
    return x * c + partner * s


def _rope_tables(length):
    t = jnp.arange(length)
    rows, cols = (t // GRID_W).astype(f32), (t % GRID_W).astype(f32)
    inv = ROPE_BASE ** (-jnp.arange(16, dtype=f32) / 16)
    ar, ac = rows[:, None] * inv[None, :], cols[:, None] * inv[None, :]
    cos = jnp.concatenate([jnp.cos(ar), jnp.cos(ar), jnp.cos(ac), jnp.cos(ac)], axis=1)
    sin = jnp.concatenate([-jnp.sin(ar), jnp.sin(ar), -jnp.sin(ac), jnp.sin(ac)], axis=1)
    return jnp.concatenate([cos, cos], axis=1), jnp.concatenate([sin, sin], axis=1)


def _tab_spec():
    per = DEC_SEQ // TM
    return pl.BlockSpec((TM, LANE), lambda i, *_: (i % per, 0))


def _mla_pre_kernel(rope, x_ref, mod_ref, wqa_ref, qn_ref, wqb_ref, wkva_ref, kvn_ref, wkvb_ref, *rest):
    if rope:
        cos_ref, sin_ref, qlat_ref, qctx_ref, kcat_ref, v_ref = rest
    else:
        qctx_ref, kcat_ref, v_ref, ckv_ref, kpe_ref = rest
    m = mod_ref[...]

    def front(sl):
        h = (x_ref[sl, :] * (1.0 + m[1:2]) + m[0:1]).astype(bf16)
        return _dot(h, wqa_ref[...]), _dot(h, wkva_ref[...])

    subs = _subs()
    nxt = front(subs[0])
    for s, sl in enumerate(subs):
        qa, kv = nxt
        if s + 1 < len(subs):
            nxt = front(subs[s + 1])
        qa = _rms_rows(qa, qn_ref[...])
        q = _dot(qa.astype(bf16), wqb_ref[...])
        ckv = _rms_rows(kv[:, :MLA_KV_LORA], kvn_ref[...])
        kpe2 = kv[:, MLA_KV_LORA:MLA_KV_LORA + LANE]
        kvb = _dot(ckv.astype(bf16), wkvb_ref[...])
        q_pe = q[:, MLA_HEADS * MLA_NOPE:]
        if rope:
            cos, sin = cos_ref[sl, :], sin_ref[sl, :]
            q_pe_rot = _rope(q_pe, cos, sin).astype(bf16)
            kpe2 = _rope(kpe2, cos, sin)
        else:
            ckv_ref[sl, :] = ckv
            kpe_ref[sl, :] = kv[:, MLA_KV_LORA:MLA_KV_LORA + MLA_ROPE]
        q_pe = q_pe.astype(bf16)
        qn = q[:, :MLA_HEADS * MLA_NOPE].astype(bf16)
        kn = kvb[:, :MLA_HEADS * MLA_NOPE].astype(bf16)
        v_ref[sl, :] = kvb[:, MLA_HEADS * MLA_NOPE:].astype(bf16)
        lane = lax.broadcasted_iota(i32, kpe2.shape, 1)
        kpe_lo = jnp.where(lane < MLA_ROPE, kpe2, 0.0).astype(bf16)
        kpe_hi = jnp.where(lane >= MLA_ROPE, kpe2, 0.0).astype(bf16)
        for hh in range(MLA_HEADS):
            a, b = hh * 256, hh * 256 + LANE
            pr = (hh // 2) * LANE
            qctx_ref[sl, a:b] = qn[:, hh * LANE:(hh + 1) * LANE]
            qctx_ref[sl, b:b + LANE] = q_pe[:, pr:pr + LANE]
            if rope:
                qlat_ref[sl, a:b] = qn[:, hh * LANE:(hh + 1) * LANE]
                qlat_ref[sl, b:b + LANE] = q_pe_rot[:, pr:pr + LANE]
            kcat_ref[sl, a:b] = kn[:, hh * LANE:(hh + 1) * LANE]
            kcat_ref[sl, b:b + LANE] = kpe_lo if hh % 2 == 0 else kpe_hi


def _mla_pre(grp, x, mod, w, rope_tabs):
    n = grp.n
    ins = [x, mod] + list(w)
    specs = [_rows(TM, D), _mod_spec(grp)] + [_const(a.shape) for a in w]
    wide = jax.ShapeDtypeStruct((n, 2 * D), bf16)
    if rope_tabs is not None:
        ins += list(rope_tabs)
        specs += [_tab_spec(), _tab_spec()]
        shapes = (wide, wide, wide, jax.ShapeDtypeStruct((n, D), bf16))
        ospecs = (_rows(TM, 2 * D), _rows(TM, 2 * D), _rows(TM, 2 * D), _rows(TM, D))
    else:
        shapes = (wide, wide, jax.ShapeDtypeStruct((n, D), bf16),
                  jax.ShapeDtypeStruct((n, MLA_KV_LORA), f32), jax.ShapeDtypeStruct((n, MLA_ROPE), f32))
        ospecs = (_rows(TM, 2 * D), _rows(TM, 2 * D), _rows(TM, D), _rows(TM, MLA_KV_LORA), _rows(TM, MLA_ROPE))
    return pl.pallas_call(
        functools.partial(_mla_pre_kernel, rope_tabs is not None),
        grid=(n // TM,), in_specs=specs, out_specs=ospecs, out_shape=shapes,
        compiler_params=_cp(48), name="mla_pre",
    )(*ins)


def _mla_ctx_kernel(ckv_ref, kpe2_ref, wkvb_ref, kcat_ref, v_ref):
    kvb = _dot(ckv_ref[...].astype(bf16), wkvb_ref[...])
    kn = kvb[:, :MLA_HEADS * MLA_NOPE].astype(bf16)
    v_ref[...] = kvb[:, MLA_HEADS * MLA_NOPE:].astype(bf16)
    kpe2 = kpe2_ref[...]
    lane = lax.broadcasted_iota(i32, kpe2.shape, 1)
    kpe_lo = jnp.where(lane < MLA_ROPE, kpe2, 0.0).astype(bf16)
    kpe_hi = jnp.where(lane >= MLA_ROPE, kpe2, 0.0).astype(bf16)
    for hh in range(MLA_HEADS):
        a, b = hh * 256, hh * 256 + LANE
        kcat_ref[:, a:b] = kn[:, hh * LANE:(hh + 1) * LANE]
        kcat_ref[:, b:b + LANE] = kpe_lo if hh % 2 == 0 else kpe_hi


def _mla_ctx(ckv, kpe2, wkvb):
    n = ckv.shape[0]
    return pl.pallas_call(
        _mla_ctx_kernel, grid=(n // PAST,),
        in_specs=[_rows(PAST, MLA_KV_LORA), _rows(PAST, LANE), _const(wkvb.shape)],
        out_specs=(_rows(PAST, 2 * D), _rows(PAST, D)),
        out_shape=(jax.ShapeDtypeStruct((n, 2 * D), bf16), jax.ShapeDtypeStruct((n, D), bf16)),
        compiler_params=_cp(24), name="mla_ctx",
    )(ckv, kpe2, wkvb)


CTX_SEQS = 2


def _mla_attn_p_kernel(q_ref, k_ref, v_ref, o_ref):
    items = [(slice(sq * SEQ, (sq + 1) * SEQ), hh) for sq in range(CTX_SEQS) for hh in range(MLA_HEADS)]

    def scores(item):
        rows, hh = item
        return _dot_nt(q_ref[rows, hh * 256:(hh + 1) * 256], k_ref[rows, hh * 256:(hh + 1) * 256])

    nxt = scores(items[0])
    for n, (rows, hh) in enumerate(items):
        s = nxt
        if n + 1 < len(items):
            nxt = scores(items[n + 1])
        e = jnp.exp2((s - jnp.max(s, axis=-1, keepdims=True)) * MLA_C2)
        l = jnp.sum(e, axis=-1, keepdims=True)
        o = _dot(e.astype(bf16), v_ref[rows, hh * LANE:(hh + 1) * LANE]) / l
        o_ref[rows, hh * LANE:(hh + 1) * LANE] = o.astype(bf16)


def _mla_attn_p(q, k, v):
    n = q.shape[0]
    rows = CTX_SEQS * SEQ
    return pl.pallas_call(
        _mla_attn_p_kernel, grid=(n // rows,),
        in_specs=[_rows(rows, 2 * D), _rows(rows, 2 * D), _rows(rows, D)],
        out_specs=_rows(rows, D), out_shape=jax.ShapeDtypeStruct((n, D), bf16),
        compiler_params=_cp(32), name="mla_attn_ctx",
    )(q, k, v)


MLA_QT = 256


def _mla_attn_s_kernel(ql_ref, qc_ref, kl_ref, vl_ref, kc_ref, vc_ref, o_ref):
    def scores(hh):
        a, b = hh * 256, (hh + 1) * 256
        return (_dot_nt(ql_ref[:, a:b], kl_ref[:, a:b]), _dot_nt(qc_ref[:, a:b], kc_ref[:, a:b]))

    nxt = scores(0)
    for hh in range(MLA_HEADS):
        s1, s2 = nxt
        if hh + 1 < MLA_HEADS:
            nxt = scores(hh + 1)
        m = jnp.maximum(jnp.max(s1, axis=-1, keepdims=True), jnp.max(s2, axis=-1, keepdims=True))
        e1, e2 = jnp.exp2((s1 - m) * MLA_C2), jnp.exp2((s2 - m) * MLA_C2)
        l = jnp.sum(e1, axis=-1, keepdims=True) + jnp.sum(e2, axis=-1, keepdims=True)
        o = (_dot(e1.astype(bf16), vl_ref[:, hh * LANE:(hh + 1) * LANE])
             + _dot(e2.astype(bf16), vc_ref[:, hh * LANE:(hh + 1) * LANE])) / l
        o_ref[:, hh * LANE:(hh + 1) * LANE] = o.astype(bf16)


def _mla_attn_s(ql, qc, kl, vl, kc, vc):
    nq = DEC_SEQ // MLA_QT
    qs = pl.BlockSpec((MLA_QT, 2 * D), lambda b, i: (b * nq + i, 0))
    return pl.pallas_call(
        _mla_attn_s_kernel, grid=(DEC_BATCH, nq),
        in_specs=[qs, qs,
                  pl.BlockSpec((DEC_SEQ, 2 * D), lambda b, i: (b, 0)),
                  pl.BlockSpec((DEC_SEQ, D), lambda b, i: (b, 0)),
                  pl.BlockSpec((PAST, 2 * D), lambda b, i: (b, 0)),
                  pl.BlockSpec((PAST, D), lambda b, i: (b, 0))],
        out_specs=pl.BlockSpec((MLA_QT, D), lambda b, i: (b * nq + i, 0)),
        out_shape=jax.ShapeDtypeStruct((NS_TOK, D), bf16),
        compiler_params=_cp(48, 2), name="mla_attn_lat",
    )(ql, qc, kl, vl, kc, vc)


def _gelu_tanh(x):
    return 0.5 * x * (1.0 + jnp.tanh(math.sqrt(2.0 / math.pi) * (x + 0.044715 * (x * x * x))))


GM_CW = GM_HALF // GM_GROUPS


def _gmlp_kernel(x_ref, mod_ref, lnp_ref, win_ref, bin_ref, vg_ref, vb_ref, ws_ref, bs_ref, wout_ref, bout_ref,
                 rwt_ref, x1_ref, h2_ref, aff_ref, gated_ref, vz_ref):
    m, lnp = mod_ref[...], lnp_ref[...]
    for s, sl in enumerate(_subs()):
        x = x_ref[sl, :]
        h = (x * (1.0 + m[1:2]) + m[0:1]).astype(bf16)

        def mm(j):
            return _dot(h, win_ref[:, j * GM_CW:(j + 1) * GM_CW]) + bin_ref[:, j * GM_CW:(j + 1) * GM_CW]

        s1 = jnp.zeros((SUB, 1), f32)
        s2 = jnp.zeros((SUB, 1), f32)
        nxt = mm(GM_GROUPS)
        for g in range(GM_GROUPS):
            cur = nxt
            nxt = mm(GM_GROUPS + g + 1) if g + 1 < GM_GROUPS else mm(0)
            z = _gelu_tanh(cur)
            s1 = s1 + jnp.sum(z, axis=-1, keepdims=True)
            s2 = s2 + jnp.sum(z * z, axis=-1, keepdims=True)
            vz_ref[s, :, g * GM_CW:(g + 1) * GM_CW] = z
        mu = s1 * (1.0 / GM_HALF)
        rstd = lax.rsqrt(s2 * (1.0 / GM_HALF) - mu * mu + LN_EPS)
        for g in range(GM_GROUPS):
            cur = nxt
            if g + 1 < GM_GROUPS:
                nxt = mm(g + 1)
            cols = slice(g * GM_CW, (g + 1) * GM_CW)
            u = _gelu_tanh(cur)
            v = ((vz_ref[s, :, cols] - mu) * rstd * vg_ref[:, cols] + vb_ref[:, cols]).astype(bf16)
            for c in range(SUB // GM_CHUNK):
                r0, r1 = c * GM_CHUNK, (c + 1) * GM_CHUNK
                sv = _dot(ws_ref[g], v[r0:r1, :]) + bs_ref[:, g:g + 1]
                gated_ref[sl.start + r0:sl.start + r1, cols] = (u[r0:r1, :] * sv).astype(bf16)
        d = _dot(gated_ref[sl, :], wout_ref[...]) + bout_ref[...]
        _close(d, x, m, lnp, rwt_ref, x1_ref, h2_ref, aff_ref, sl)


def _gmlp_layer(grp, x, mod, lnp, w, rwt):
    shapes, specs = _close_outs(grp.n)
    return pl.pallas_call(
        _gmlp_kernel, grid=(grp.n // TM,),
        in_specs=[_rows(TM, D), _mod_spec(grp), _const((8, D))] + [_const(a.shape) for a in w]
        + [_const((D, LANE))],
        out_specs=specs, out_shape=shapes,
        scratch_shapes=[pltpu.VMEM((TM, GM_HALF), bf16), pltpu.VMEM((TM // SUB, SUB, GM_HALF), f32)],
        compiler_params=_cp(56), name="gmlp_layer",
    )(x, mod, lnp, *w, rwt)


HALO = 16
CONV_RB = 64
CONV_LW = 256


def _conv_glu_kernel(x_ref, mod_ref, w_ref, b_ref, a_ref):
    m = mod_ref[...]

    def front(sl):
        h = (x_ref[sl, :] * (1.0 + m[1:2]) + m[0:1]).astype(bf16)
        return _dot(h, w_ref[...])

    subs = _subs()
    nxt = front(subs[0])
    for s, sl in enumerate(subs):
        a = nxt + b_ref[...]
        if s + 1 < len(subs):
            nxt = front(subs[s + 1])
        a_ref[sl, :] = a[:, :D] * jax.nn.sigmoid(a[:, D:])


def _conv_glu(grp, x, mod, w, b):
    return pl.pallas_call(
        _conv_glu_kernel, grid=(grp.n // TM,),
        in_specs=[_rows(TM, D), _mod_spec(grp), _const(w.shape), _const(b.shape)],
        out_specs=_rows(TM, D), out_shape=jax.ShapeDtypeStruct((grp.n, D), f32),
        compiler_params=_cp(40), name="conv_glu",
    )(x, mod, w, b)


def _conv_close_kernel(seq_subs, ap_ref, a_ref, an_ref, x_ref, mod_ref, lnp_ref, wdw_ref, bdw_ref, ng_ref, nb_ref,
                       w2_ref, b2_ref, rwt_ref, x1_ref, h2_ref, aff_ref, pad_ref, act_ref, cout_ref, shift_ref):
    i = pl.program_id(0)
    nsub = TM // SUB
    bdw, ng, nb = bdw_ref[...], ng_ref[...], nb_ref[...]
    m, lnp = mod_ref[...], lnp_ref[...]
    for s, sl in enumerate(_subs()):
        gsub = i * nsub + s
        prev = a_ref[sl.start - HALO:sl.start, :] if s > 0 else ap_ref[...]
        nxt = a_ref[sl.stop:sl.stop + HALO, :] if s < nsub - 1 else an_ref[...]
        pad = pad_ref.at[s]
        pad[0:HALO, :] = jnp.where((gsub % seq_subs) != 0, prev, 0.0)
        pad[HALO:HALO + SUB, :] = a_ref[sl, :]
        pad[HALO + SUB:, :] = jnp.where((gsub % seq_subs) != seq_subs - 1, nxt, 0.0)
        span = SUB + 24
        for r in range(1, 8):
            shift_ref[s, r - 1, :, :] = pad[r:r + span, :]
        for rb in range(SUB // CONV_RB):
            r0 = rb * CONV_RB
            for lc in range(D // CONV_LW):
                lanes = slice(lc * CONV_LW, (lc + 1) * CONV_LW)
                acc = jnp.zeros((CONV_RB, CONV_LW), f32)
                for k in range(CONV_W):
                    mm, r = (k + 1) // 8, (k + 1) % 8
                    rows = slice(r0 + 8 * mm, r0 + 8 * mm + CONV_RB)
                    win = pad[rows, lanes] if r == 0 else shift_ref[s, r - 1, rows, lanes]
                    acc = acc + wdw_ref[k:k + 1, lanes] * win
                cout_ref[s, r0:r0 + CONV_RB, lanes] = acc
            y = _silu(_ln_rows(cout_ref[s, r0:r0 + CONV_RB, :] + bdw, ng, nb))
            act_ref[sl.start + r0:sl.start + r0 + CONV_RB, :] = y.astype(bf16)
        d = _dot(act_ref[sl, :], w2_ref[...]) + b2_ref[...]
        _close(d, x_ref[sl, :], m, lnp, rwt_ref, x1_ref, h2_ref, aff_ref, sl)


def _conv_close(grp, a, x, mod, lnp, w, rwt):
    n = grp.n
    hb = TM // HALO
    last = n // HALO - 1
    shapes, specs = _close_outs(n)
    return pl.pallas_call(
        functools.partial(_conv_close_kernel, grp.seq // SUB), grid=(n // TM,),
        in_specs=[pl.BlockSpec((HALO, D), lambda i: (jnp.maximum(i * hb - 1, 0), 0)),
                  _rows(TM, D),
                  pl.BlockSpec((HALO, D), lambda i: (jnp.minimum((i + 1) * hb, last), 0)),
                  _rows(TM, D), _mod_spec(grp), _const((8, D))]
        + [_const(t.shape) for t in w] + [_const((D, LANE))],
        out_specs=specs, out_shape=shapes,
        scratch_shapes=[pltpu.VMEM((TM // SUB, SUB + 2 * HALO, D), f32), pltpu.VMEM((TM, D), bf16),
                        pltpu.VMEM((TM // SUB, SUB, D), f32), pltpu.VMEM((TM // SUB, 7, SUB + 24, D), f32)],
        compiler_params=_cp(48), name="conv_close",
    )(a, a, a, x, mod, lnp, *w, rwt)


def _swa_pre_kernel(rope, x_ref, mod_ref, wq_ref, wk_ref, wv_ref, *rest):
    if rope:
        cos_ref, sin_ref, qrot_ref, qraw_ref, kd_ref, vd_ref = rest
    else:
        qraw_ref, kd_ref, vd_ref, k_ref, v_ref = rest
    m = mod_ref[...]

    def front(sl):
        h = (x_ref[sl, :] * (1.0 + m[1:2]) + m[0:1]).astype(bf16)
        return _dot(h, wq_ref[...]), _dot(h, wk_ref[...]), _dot(h, wv_ref[...])

    subs = _subs()
    nxt = front(subs[0])
    for s, sl in enumerate(subs):
        q, kd, vd = nxt
        if s + 1 < len(subs):
            nxt = front(subs[s + 1])
        qraw_ref[sl, :] = q.astype(bf16)
        vd_ref[sl, :] = vd.astype(bf16)
        if rope:
            cos, sin = cos_ref[sl, :], sin_ref[sl, :]
            qrot_ref[sl, :] = _rope(q, cos, sin).astype(bf16)
            kd_ref[sl, :] = _rope(kd, cos, sin).astype(bf16)
        else:
            kd_ref[sl, :] = kd.astype(bf16)
            lane = lax.broadcasted_iota(i32, (SUB, LANE), 1)
            for j in range(SWA_KV // 2):
                lo, hi = 2 * j * LANE, (2 * j + 1) * LANE
                k_ref[sl, j * LANE:(j + 1) * LANE] = jnp.where(lane < SWA_HD, kd[:, lo:lo + LANE], kd[:, hi:hi + LANE])
                v_ref[sl, j * LANE:(j + 1) * LANE] = jnp.where(lane < SWA_HD, vd[:, lo:lo + LANE], vd[:, hi:hi + LANE])


def _swa_pre(grp, x, mod, w, rope_tabs):
    n = grp.n
    ins = [x, mod] + list(w)
    specs = [_rows(TM, D), _mod_spec(grp)] + [_const(a.shape) for a in w]
    kw = 2 * SWA_KV * SWA_HD
    qs, ks = jax.ShapeDtypeStruct((n, D), bf16), jax.ShapeDtypeStruct((n, kw), bf16)
    if rope_tabs is not None:
        ins += list(rope_tabs)
        specs += [_tab_spec(), _tab_spec()]
        shapes = (qs, qs, ks, ks)
        ospecs = (_rows(TM, D), _rows(TM, D), _rows(TM, kw), _rows(TM, kw))
    else:
        nat = jax.ShapeDtypeStruct((n, SWA_KV * SWA_HD), f32)
        shapes = (qs, ks, ks, nat, nat)
        ospecs = (_rows(TM, D), _rows(TM, kw), _rows(TM, kw), _rows(TM, SWA_KV * SWA_HD), _rows(TM, SWA_KV * SWA_HD))
    return pl.pallas_call(
        functools.partial(_swa_pre_kernel, rope_tabs is not None),
        grid=(n // TM,), in_specs=specs, out_specs=ospecs, out_shape=shapes,
        compiler_params=_cp(40), name="swa_pre",
    )(*ins)


def _half_mask(x, parity):
    lane = lax.broadcasted_iota(i32, x.shape, 1)
    keep = (lane < SWA_HD) if parity == 0 else (lane >= SWA_HD)
    return jnp.where(keep, x, jnp.zeros_like(x))


def _swa_attn_p_kernel(sink_ref, q_ref, kd_ref, vd_ref, o_ref):
    items = [(slice(sq * SEQ, (sq + 1) * SEQ), hd) for sq in range(CTX_SEQS) for hd in range(SWA_HEADS)]

    def scores(item):
        rows, hd = item
        pair, par = hd // 2, hd % 2
        g = hd // (SWA_HEADS // SWA_KV)
        return _dot_nt(q_ref[rows, pair * LANE:(pair + 1) * LANE], _half_mask(kd_ref[rows, g * LANE:(g + 1) * LANE], par))

    nxt = scores(items[0])
    acc = None
    for n, (rows, hd) in enumerate(items):
        pair, par = hd // 2, hd % 2
        g = hd // (SWA_HEADS // SWA_KV)
        s = nxt
        if n + 1 < len(items):
            nxt = scores(items[n + 1])
        sk = sink_ref[hd] * (1.0 / SWA_SCALE)
        m = jnp.maximum(jnp.max(s, axis=-1, keepdims=True), sk)
        e = jnp.exp2((s - m) * SWA_C2)
        l = jnp.sum(e, axis=-1, keepdims=True) + jnp.exp2((sk - m) * SWA_C2)
        o = _dot(e.astype(bf16), _half_mask(vd_ref[rows, g * LANE:(g + 1) * LANE], par)) / l
        acc = o if par == 0 else acc + o
        if par == 1:
            o_ref[rows, pair * LANE:(pair + 1) * LANE] = acc.astype(bf16)


def _swa_attn_p(sink, q, kd, vd):
    n = q.shape[0]
    kw = kd.shape[1]
    rows = CTX_SEQS * SEQ
    return pl.pallas_call(
        _swa_attn_p_kernel,
        grid_spec=pltpu.PrefetchScalarGridSpec(
            num_scalar_prefetch=1, grid=(n // rows,),
            in_specs=[_rows(rows, D), _rows(rows, kw), _rows(rows, kw)],
            out_specs=_rows(rows, D)),
        out_shape=jax.ShapeDtypeStruct((n, D), bf16),
        compiler_params=_cp(32), name="swa_attn_ctx",
    )(sink, q, kd, vd)


SWA_QB = 128


SWA_QBLOCKS = 2


def _swa_attn_s_kernel(sink_ref, qr_ref, qw_ref, kd_ref, vd_ref, kc_ref, vc_ref, o_ref):
    span = 3 * SWA_QB
    blocks = []
    for bq in range(SWA_QBLOCKS):
        nblk = pl.program_id(1) * SWA_QBLOCKS + bq
        start = pl.multiple_of(jnp.clip((nblk - 1) * SWA_QB, 0, DEC_SEQ - span), SWA_QB)
        qpos = nblk * SWA_QB + lax.broadcasted_iota(i32, (SWA_QB, span), 0)
        kpos = start + lax.broadcasted_iota(i32, (SWA_QB, span), 1)
        blocks.append((slice(bq * SWA_QB, (bq + 1) * SWA_QB), start, jnp.abs(kpos - qpos) <= SWA_WIN))
    items = [(bq, hd) for bq in range(SWA_QBLOCKS) for hd in range(SWA_HEADS)]

    def scores(item):
        bq, hd = item
        rows, start, _ = blocks[bq]
        pair, par = hd // 2, hd % 2
        g = hd // (SWA_HEADS // SWA_KV)
        cols = slice(g * LANE, (g + 1) * LANE)
        s1 = _dot_nt(qr_ref[rows, pair * LANE:(pair + 1) * LANE], _half_mask(kd_ref[pl.ds(start, span), cols], par))
        s2 = _dot_nt(qw_ref[rows, pair * LANE:(pair + 1) * LANE], _half_mask(kc_ref[:, cols], par))
        return s1, s2

    nxt = scores(items[0])
    acc = None
    for n, (bq, hd) in enumerate(items):
        rows, start, band = blocks[bq]
        pair, par = hd // 2, hd % 2
        g = hd // (SWA_HEADS // SWA_KV)
        cols = slice(g * LANE, (g + 1) * LANE)
        s1, s2 = nxt
        if n + 1 < len(items):
            nxt = scores(items[n + 1])
        s1 = jnp.where(band, s1, NEG_INF)
        sk = sink_ref[hd] * (1.0 / SWA_SCALE)
        m = jnp.maximum(jnp.maximum(jnp.max(s1, axis=-1, keepdims=True), jnp.max(s2, axis=-1, keepdims=True)), sk)
        e1, e2 = jnp.exp2((s1 - m) * SWA_C2), jnp.exp2((s2 - m) * SWA_C2)
        l = (jnp.sum(e1, axis=-1, keepdims=True) + jnp.sum(e2, axis=-1, keepdims=True)
             + jnp.exp2((sk - m) * SWA_C2))
        o = (_dot(e1.astype(bf16), _half_mask(vd_ref[pl.ds(start, span), cols], par))
             + _dot(e2.astype(bf16), _half_mask(vc_ref[:, cols], par))) / l
        acc = o if par == 0 else acc + o
        if par == 1:
            o_ref[rows, pair * LANE:(pair + 1) * LANE] = acc.astype(bf16)


def _swa_attn_s(sink, qr, qw, kd, vd, kc, vc):
    qrows = SWA_QB * SWA_QBLOCKS
    nq = DEC_SEQ // qrows
    kw = kd.shape[1]
    qs = pl.BlockSpec((qrows, D), lambda b, i, *_: (b * nq + i, 0))
    full = pl.BlockSpec((DEC_SEQ, kw), lambda b, i, *_: (b, 0))
    ctx = pl.BlockSpec((PAST, kw), lambda b, i, *_: (b, 0))
    return pl.pallas_call(
        _swa_attn_s_kernel,
        grid_spec=pltpu.PrefetchScalarGridSpec(
            num_scalar_prefetch=1, grid=(DEC_BATCH, nq),
            in_specs=[qs, qs, full, full, ctx, ctx],
            out_specs=pl.BlockSpec((qrows, D), lambda b, i, *_: (b * nq + i, 0))),
        out_shape=jax.ShapeDtypeStruct((NS_TOK, D), bf16),
        compiler_params=_cp(32, 2), name="swa_attn_lat",
    )(sink, qr, qw, kd, vd, kc, vc)


def _excl_prefix(mask_f, nb, tri, blk):
    m2 = mask_f.reshape(nb * NE, RT)
    within = _dot(m2.astype(bf16), tri)
    tot = jnp.sum(m2, axis=1, keepdims=True)
    totb = jnp.broadcast_to(tot, (nb * NE, LANE)).astype(bf16)
    offs = _dot(blk, totb)
    return (within + offs[:, 0:1]).reshape(nb, NE, RT), offs.reshape(nb, NE, LANE)


def _route_kernel(cap, nb, aff_ref, dest_ref, gate_ref, ws_ref, nq_ref, destt_ref):
    a = aff_ref[...]

    def as_f32(bits):
        return pltpu.bitcast(bits, f32)[None]

    def count_ge(th):
        c = jnp.sum(jnp.where(a >= as_f32(th), 1.0, 0.0), axis=0)
        return jnp.sum(c, axis=1, keepdims=True)

    def body(_, c):
        lo, hi = c
        mid = lo + ((hi - lo + 1) >> 1)
        ok = count_ge(mid) >= cap
        return jnp.where(ok, mid, lo), jnp.where(ok, hi, mid - 1)

    lo0 = jnp.zeros((NE, 1), i32)
    hi0 = jnp.full((NE, 1), 0x7F800000, i32)
    thr, _ = lax.fori_loop(0, 31, body, (lo0, hi0))

    r = lax.broadcasted_iota(i32, (RT, RT), 0)
    c = lax.broadcasted_iota(i32, (RT, RT), 1)
    tri = jnp.where(r < c, 1.0, 0.0).astype(bf16)
    rr = lax.broadcasted_iota(i32, (nb * NE, nb * NE), 0)
    cc = lax.broadcasted_iota(i32, (nb * NE, nb * NE), 1)
    blk = jnp.where(((rr % NE) == (cc % NE)) & (cc < rr), 1.0, 0.0).astype(bf16)

    gt = a > as_f32(thr)
    eq = a == as_f32(thr)
    n_gt = jnp.sum(jnp.sum(jnp.where(gt, 1.0, 0.0), axis=0), axis=1, keepdims=True)
    need = cap - n_gt
    tie_rank, _ = _excl_prefix(jnp.where(eq, 1.0, 0.0), nb, tri, blk)
    sel = gt | (eq & (tie_rank < need[None]))
    pos, offs = _excl_prefix(jnp.where(sel, 1.0, 0.0), nb, tri, blk)
    dest_ref[...] = jnp.where(sel, pos, -1.0).astype(i32)
    gate_ref[...] = jnp.where(sel, a, 0.0)
    ct_end = jnp.concatenate([offs[1:], jnp.full((1, NE, LANE), float(cap), f32)], axis=0)
    ws = jnp.floor(offs * (1.0 / 16)) * 16.0
    ws_ref[...] = ws.astype(i32)
    nchunk = jnp.floor((ct_end - ws + (WIN - 1)) * (1.0 / WIN))
    nq_ref[...] = jnp.broadcast_to(jnp.max(nchunk, axis=1, keepdims=True), (nb, NE, LANE)).astype(i32)
    destp = jnp.where(sel, pos + 1.0, 0.0)
    zpad = jnp.zeros((LANE - NE, RT), f32)
    for b in range(nb):
        destt_ref[b * RT:(b + 1) * RT, :] = jnp.transpose(jnp.concatenate([destp[b], zpad], axis=0))


def _route(grp, aff3):
    nb = grp.n // RT
    full = pl.BlockSpec((nb, NE, RT), lambda: (0, 0, 0))
    small = pl.BlockSpec((nb, NE, LANE), lambda: (0, 0, 0))
    dest3, gate3, ws3, nq3, destt = pl.pallas_call(
        functools.partial(_route_kernel, grp.cap, nb),
        in_specs=[full], out_specs=(full, full, small, small, pl.BlockSpec((grp.n, LANE), lambda: (0, 0))),
        out_shape=(jax.ShapeDtypeStruct((nb, NE, RT), i32), jax.ShapeDtypeStruct((nb, NE, RT), f32),
                   jax.ShapeDtypeStruct((nb, NE, LANE), i32), jax.ShapeDtypeStruct((nb, NE, LANE), i32),
                   jax.ShapeDtypeStruct((grp.n, LANE), f32)),
        compiler_params=pltpu.CompilerParams(vmem_limit_bytes=48 * MIB), name="route",
    )(aff3)
    return dest3, gate3, ws3[:, :, 0], nq3[:, 0, 0], destt


def _dispatch_kernel(cap, eg, ws_ref, nq_ref, h2_ref, dest_ref, gate_ref, xe_ref, gcol_ref, hot_ref):
    g, step = pl.program_id(0), pl.program_id(1)

    @pl.when(step == 0)
    def _():
        xe_ref[...] = jnp.zeros_like(xe_ref)
        gcol_ref[...] = jnp.zeros_like(gcol_ref)

    wrow = lax.broadcasted_iota(i32, (WIN, RT), 0)
    for b in range(DISPATCH_BLOCKS):
        tb = step * DISPATCH_BLOCKS + b
        hot = hot_ref.at[b]

        def body(q, carry, tb=tb, b=b, hot=hot):
            starts, gsums = [], []
            for el in range(eg):
                e = g * eg + el
                want = ws_ref[tb, e] + q * WIN
                st = pl.multiple_of(jnp.minimum(want, cap - WIN), 16)
                row = st + wrow
                hit = dest_ref[b, pl.ds(e, 1), :] == jnp.where(row >= want, row, -7)
                hot[el * WIN:(el + 1) * WIN, :] = jnp.where(hit, 1.0, 0.0).astype(bf16)
                gsums.append(jnp.sum(jnp.where(hit, gate_ref[b, pl.ds(e, 1), :], 0.0), axis=1, keepdims=True))
                starts.append(st)
            part = _dot(hot[...], h2_ref[b * RT:(b + 1) * RT, :])
            for el in range(eg):
                dst = pl.ds(el * cap + starts[el], WIN)
                xe_ref[dst, :] = xe_ref[dst, :] + part[el * WIN:(el + 1) * WIN, :].astype(bf16)
                gcol_ref[dst, :] = gcol_ref[dst, :] + jnp.broadcast_to(gsums[el], (WIN, LANE))
            return carry

        lax.fori_loop(0, nq_ref[tb], body, 0)


def _dispatch(grp, wstart, nq, h2, dest3, gate3):
    eg = ROWS_PER_PASS // grp.cap
    nb = grp.n // (RT * DISPATCH_BLOCKS)
    tab = pl.BlockSpec((DISPATCH_BLOCKS, NE, RT), lambda g, t, *_: (t, 0, 0))
    return pl.pallas_call(
        functools.partial(_dispatch_kernel, grp.cap, eg),
        grid_spec=pltpu.PrefetchScalarGridSpec(
            num_scalar_prefetch=2, grid=(NE // eg, nb),
            in_specs=[pl.BlockSpec((RT * DISPATCH_BLOCKS, D), lambda g, t, *_: (t, 0)), tab, tab],
            out_specs=(pl.BlockSpec((ROWS_PER_PASS, D), lambda g, t, *_: (g, 0)),
                       pl.BlockSpec((ROWS_PER_PASS, LANE), lambda g, t, *_: (g, 0))),
            scratch_shapes=[pltpu.VMEM((DISPATCH_BLOCKS, eg * WIN, RT), bf16)]),
        out_shape=(jax.ShapeDtypeStruct((NE * grp.cap, D), bf16), jax.ShapeDtypeStruct((NE * grp.cap, LANE), f32)),
        compiler_params=_cp(56, 2), name="moe_dispatch",
    )(wstart, nq, h2, dest3, gate3)


FFN_RB = 256


def _ffn_kernel(xp_ref, gp_ref, xs_ref, gs_ref, wg_ref, wu_ref, wd_ref, yp_ref, ys_ref, wgb, wub, wdb):
    wgb[...] = wg_ref[...].astype(bf16)
    wub[...] = wu_ref[...].astype(bf16)
    wdb[...] = wd_ref[...].astype(bf16)
    for x_ref, g_ref, y_ref, cap in ((xp_ref, gp_ref, yp_ref, CTX.cap), (xs_ref, gs_ref, ys_ref, LAT.cap)):
        for r in range(cap // FFN_RB):
            sl = slice(r * FFN_RB, (r + 1) * FFN_RB)
            x = x_ref[sl, :]
            hid = (_silu(_dot(x, wgb[...])) * _dot(x, wub[...])).astype(bf16)
            gate = jnp.concatenate([g_ref[sl, :]] * (D // LANE), axis=1)
            y_ref[sl, :] = (_dot(hid, wdb[...]) * gate).astype(bf16)


def _ffn(layer, xp, gp, xs, gs, wg, wu, wd):
    wspec = pl.BlockSpec((None, None, D, FF), lambda e: (layer, e, 0, 0))
    cp, cs = CTX.cap, LAT.cap
    return pl.pallas_call(
        _ffn_kernel, grid=(NE,),
        in_specs=[pl.BlockSpec((cp, D), lambda e: (e, 0)), pl.BlockSpec((cp, LANE), lambda e: (e, 0)),
                  pl.BlockSpec((cs, D), lambda e: (e, 0)), pl.BlockSpec((cs, LANE), lambda e: (e, 0)),
                  wspec, wspec, pl.BlockSpec((None, None, FF, D), lambda e: (layer, e, 0, 0))],
        out_specs=(pl.BlockSpec((cp, D), lambda e: (e, 0)), pl.BlockSpec((cs, D), lambda e: (e, 0))),
        out_shape=(jax.ShapeDtypeStruct((NE * cp, D), bf16), jax.ShapeDtypeStruct((NE * cs, D), bf16)),
        scratch_shapes=[pltpu.VMEM((D, FF), bf16), pltpu.VMEM((D, FF), bf16), pltpu.VMEM((FF, D), bf16)],
        compiler_params=_cp(56), name="moe_ffn",
    )(xp, gp, xs, gs, wg, wu, wd)


def _combine_kernel(cap, ws_ref, nq_ref, yg_ref, dt_ref, x1_ref, mod_ref, lnp_ref, x2_ref, win_ref, y_ref):
    step = pl.program_id(0)
    kw = NE * WIN
    lane = lax.broadcasted_iota(i32, (1, kw), 1)
    lane_e, lane_w = lane // WIN, lane % WIN
    er = lax.broadcasted_iota(i32, (2 * LANE, kw), 0)
    ec = lax.broadcasted_iota(i32, (2 * LANE, kw), 1) // WIN
    spread = jnp.where(er == ec, 32.0, jnp.where(er - LANE == ec, 1.0, 0.0)).astype(bf16)

    def spread_rows(rows):
        d = dt_ref[rows, :]
        hi = jnp.floor(d * (1.0 / 32))
        hilo = jnp.concatenate([hi, d - 32.0 * hi], axis=1).astype(bf16)
        return _dot(hilo, spread)

    def chunk(b, tb, q, destp):
        win = win_ref.at[b]
        tgt = jnp.full((1, kw), -1, i32)
        for e in range(NE):
            want = ws_ref[tb, e] + q * WIN
            st = pl.multiple_of(jnp.minimum(want, cap - WIN), 16)
            win[e * WIN:(e + 1) * WIN, :] = yg_ref[pl.ds(e * cap + st, WIN), :]
            row = st + lane_w
            tgt = jnp.where(lane_e == e, jnp.where(row >= want, row + 1, -1), tgt)
        onehot = jnp.where(destp == tgt.astype(f32), 1.0, 0.0).astype(bf16)
        return _dot(onehot, win[...])

    blocks = [(b, step * BLOCKS_PER_STEP + b, slice(b * RT, (b + 1) * RT)) for b in range(BLOCKS_PER_STEP)]
    m, lnp = mod_ref[...], lnp_ref[...]

    def close(rows, y):
        x2_ref[rows, :] = _ln_rows(ALPHA * x1_ref[rows, :] + m[5:6] * y, lnp[2:3], lnp[3:4])

    ys = []
    for b, tb, rows in blocks:
        ys.append(chunk(b, tb, 0, spread_rows(rows)))
    for (b, tb, rows), y in zip(blocks, ys):
        y_ref[b] = y
        close(rows, y)
    for b, tb, rows in blocks:
        @pl.when(nq_ref[tb] > 1)
        def _(b=b, tb=tb, rows=rows):
            destp = spread_rows(rows)

            def body(q, carry):
                y_ref[b] = y_ref[b] + chunk(b, tb, q, destp)
                return carry

            lax.fori_loop(1, nq_ref[tb], body, 0)
            close(rows, y_ref[b])


def _combine(grp, wstart, nq, yg, destab, x1, mod, lnp):
    rt = RT * BLOCKS_PER_STEP
    seg = pl.BlockSpec((None, 8, D), lambda i, *_: (grp.seg0 + (i * rt) // grp.seq if grp.seq > rt else grp.seg0, 0, 0))
    return pl.pallas_call(
        functools.partial(_combine_kernel, grp.cap),
        grid_spec=pltpu.PrefetchScalarGridSpec(
            num_scalar_prefetch=2, grid=(grp.n // rt,),
            in_specs=[_const(yg.shape), _rows(rt, LANE), _rows(rt, D), seg, _const((8, D))],
            out_specs=_rows(rt, D),
            scratch_shapes=[pltpu.VMEM((BLOCKS_PER_STEP, NE * WIN, D), bf16), pltpu.VMEM((BLOCKS_PER_STEP, RT, D), f32)]),
        out_shape=jax.ShapeDtypeStruct((grp.n, D), f32),
        compiler_params=_cp(52), name="moe_combine",
    )(wstart, nq, yg, destab, x1, mod, lnp)


def _moe(layer, closed, mod, lnp, wg, wu, wd):
    disp = []
    for grp, (x1, h2, aff3) in zip((CTX, LAT), closed):
        dest3, gate3, wstart, nq, destab = _route(grp, aff3)
        xe, gcol = _dispatch(grp, wstart, nq, h2, dest3, gate3)
        disp.append((xe, gcol, wstart, nq, destab, x1))
    ys = _ffn(layer, disp[0][0], disp[0][1], disp[1][0], disp[1][1], wg, wu, wd)
    return tuple(_combine(grp, d[2], d[3], y, d[4], d[5], mod, lnp)
                 for grp, d, y in zip((CTX, LAT), disp, ys))


def _mla_weights(wq_a, q_norm, wq_b, wkv_a, kv_norm, wkv_b):
    wqb = wq_b.reshape(MLA_Q_LORA, MLA_HEADS, MLA_NOPE + MLA_ROPE)
    wqb = jnp.concatenate([wqb[:, :, :MLA_NOPE].reshape(MLA_Q_LORA, -1), wqb[:, :, MLA_NOPE:].reshape(MLA_Q_LORA, -1)], axis=1)
    wkva = jnp.concatenate([wkv_a, wkv_a[:, MLA_KV_LORA:]], axis=1)
    wkvb = wkv_b.reshape(MLA_KV_LORA, MLA_HEADS, MLA_NOPE + MLA_V)
    wkvb = jnp.concatenate([wkvb[:, :, :MLA_NOPE].reshape(MLA_KV_LORA, -1), wkvb[:, :, MLA_NOPE:].reshape(MLA_KV_LORA, -1)], axis=1)
    return (wq_a.astype(bf16), q_norm.reshape(1, -1), wqb.astype(bf16), wkva.astype(bf16), kv_norm.reshape(1, -1),
            wkvb.astype(bf16))


def _dup_heads(w):
    w = w.reshape(w.shape[0], SWA_KV, 1, SWA_HD)
    return jnp.broadcast_to(w, (w.shape[0], SWA_KV, 2, SWA_HD)).reshape(w.shape[0], 2 * SWA_KV * SWA_HD)


def kernel(x_prompt, x_sample, c, cache_mla_ckv, cache_mla_kpe, cache_swa_k, cache_swa_v, c_ctx, w_mod, b_mod, ln_gain, ln_bias, router_w, moe_w_gate, moe_w_up, moe_w_down, mla_wq_a, mla_q_norm, mla_wq_b, mla_wkv_a, mla_kv_norm, mla_wkv_b, mla_wo, gm_w_in, gm_b_in, gm_v_norm_g, gm_v_norm_b, gm_w_s, gm_b_s, gm_w_out, gm_b_out, cv_w_pw1, cv_b_pw1, cv_w_dw, cv_b_dw, cv_norm_g, cv_norm_b, cv_w_pw2, cv_b_pw2, swa_wq, swa_wk, swa_wv, swa_sink, swa_wo):
    groups = (CTX, LAT)
    xs = (x_prompt.reshape(NP_TOK, D), x_sample.reshape(NS_TOK, D))
    cond8 = jnp.concatenate([c_ctx[None, :], c, jnp.zeros((5, D), f32)], axis=0)
    mods = _modulation(cond8, w_mod, b_mod)
    lnps = jnp.concatenate([jnp.stack([ln_gain[:, 0], ln_bias[:, 0], ln_gain[:, 1], ln_bias[:, 1]], axis=1),
                            jnp.zeros((DEPTH, 4, D), f32)], axis=1)
    rwts = jnp.pad(router_w, ((0, 0), (0, 0), (0, LANE - NE)))
    tabs = _rope_tables(DEC_SEQ)

    def moe(i, closed):
        return _moe(i, closed, mods[i], lnps[i], moe_w_gate, moe_w_up, moe_w_down)

    wm = _mla_weights(mla_wq_a[0], mla_q_norm[0], mla_wq_b[0], mla_wkv_a[0], mla_kv_norm[0], mla_wkv_b[0])
    qc_p, kc_p, v_p, ckv_p, kpe_p = _mla_pre(CTX, xs[0], mods[0], wm, None)
    ql_s, qc_s, kc_s, v_s = _mla_pre(LAT, xs[1], mods[0], wm, tabs)
    cache_kpe = cache_mla_kpe[:, 0].reshape(DEC_BATCH * PAST, MLA_ROPE)
    kc_c, v_c = _mla_ctx(cache_mla_ckv[:, 0].reshape(DEC_BATCH * PAST, MLA_KV_LORA),
                         jnp.concatenate([cache_kpe, cache_kpe], axis=1), wm[5])
    os_ = (_mla_attn_p(qc_p, kc_p, v_p), _mla_attn_s(ql_s, qc_s, kc_s, v_s, kc_c, v_c))
    wo = mla_wo[0].astype(bf16)
    xs = moe(0, [_proj_close(g, o, x, mods[0], lnps[0], wo, rwts[0]) for g, o, x in zip(groups, os_, xs)])

    wgm = (gm_w_in[0].astype(bf16), gm_b_in[0].reshape(1, -1), gm_v_norm_g[0].reshape(1, -1),
           gm_v_norm_b[0].reshape(1, -1), gm_w_s[0].astype(bf16), gm_b_s[0].T, gm_w_out[0].astype(bf16),
           gm_b_out[0].reshape(1, -1))
    xs = moe(1, [_gmlp_layer(g, x, mods[1], lnps[1], wgm, rwts[1]) for g, x in zip(groups, xs)])

    w1, b1 = cv_w_pw1[0].astype(bf16), cv_b_pw1[0].reshape(1, -1)
    wcv = (cv_w_dw[0], cv_b_dw[0].reshape(1, -1), cv_norm_g[0].reshape(1, -1), cv_norm_b[0].reshape(1, -1),
           cv_w_pw2[0].astype(bf16), cv_b_pw2[0].reshape(1, -1))
    xs = moe(2, [_conv_close(g, _conv_glu(g, x, mods[2], w1, b1), x, mods[2], lnps[2], wcv, rwts[2])
                 for g, x in zip(groups, xs)])

    wsw = (swa_wq[0].astype(bf16), _dup_heads(swa_wk[0]).astype(bf16), _dup_heads(swa_wv[0]).astype(bf16))
    q_p, kd_p, vd_p, k_p, v_p = _swa_pre(CTX, xs[0], mods[3], wsw, None)
    qr_s, qw_s, kd_s, vd_s = _swa_pre(LAT, xs[1], mods[3], wsw, tabs)

    def dup_cache(t):
        return _dup_heads(t[:, 0].reshape(DEC_BATCH * PAST, SWA_KV * SWA_HD)).astype(bf16)

    sink = swa_sink[0]
    os_ = (_swa_attn_p(sink, q_p, kd_p, vd_p),
           _swa_attn_s(sink, qr_s, qw_s, kd_s, vd_s, dup_cache(cache_swa_k), dup_cache(cache_swa_v)))
    wo = swa_wo[0].astype(bf16)
    xs = moe(3, [_proj_close(g, o, x, mods[3], lnps[3], wo, rwts[3]) for g, o, x in zip(groups, os_, xs)])

    return (xs[0].reshape(BATCH, SEQ, D), xs[1].reshape(DEC_BATCH, DEC_SEQ, D),
            ckv_p.reshape(BATCH, 1, SEQ, MLA_KV_LORA), kpe_p.reshape(BATCH, 1, SEQ, MLA_ROPE),
            k_p.reshape(BATCH, 1, SEQ, SWA_KV, SWA_HD), v_p.reshape(BATCH, 1, SEQ, SWA_KV, SWA_HD))
```

```python
import functools
import math

import jax
import jax.numpy as jnp
from jax import lax
from jax.experimental import pallas as pl
from jax.experimental.pallas import tpu as pltpu

f32 = jnp.float32
bf16 = jnp.bfloat16
i32 = jnp.int32

D = 1024
BATCH, SEQ = 32, 256
DEC_BATCH, DEC_SEQ = 2, 2048
PAST = 256
DEPTH = 4
GRID_W = 64
ALPHA = (2 * DEPTH) ** 0.25
LN_EPS = 1e-5
RMS_EPS = 1e-6
ROPE_BASE = 10000.0
NEG_INF = -1e30
MLA_HEADS, MLA_NOPE, MLA_ROPE, MLA_V = 8, 128, 64, 128
MLA_Q_LORA, MLA_KV_LORA = 384, 256
MLA_SCALE = (MLA_NOPE + MLA_ROPE) ** -0.5
MLA_C2 = MLA_SCALE * math.log2(math.e)
GM_CHUNK, GM_HALF, GM_GROUPS = 128, 2048, 4
CONV_W = 31
SWA_HEADS, SWA_KV, SWA_HD, SWA_WIN = 16, 4, 64, 128
SWA_SCALE = SWA_HD ** -0.5
SWA_C2 = SWA_SCALE * math.log2(math.e)
NE = 16
FF = 1024

NP_TOK = BATCH * SEQ
NS_TOK = DEC_BATCH * DEC_SEQ

LANE = 128
TM = 512
SUB = 256
RT = 256
WIN = 64
BLOCKS_PER_STEP = 2
DISPATCH_BLOCKS = 8
ROWS_PER_PASS = 8192
MIB = 2 ** 20


class Group:
    def __init__(self, n_tok, seq, seg0):
        self.n = n_tok
        self.seq = seq
        self.cap = 2 * n_tok // NE
        self.seg0 = seg0

    def seg_of(self, i):
        return self.seg0 + (i * TM) // self.seq if self.seq > TM else self.seg0


CTX = Group(NP_TOK, SEQ, 0)
LAT = Group(NS_TOK, DEC_SEQ, 1)


def _cp(vmem_mb, n_axes=1):
    return pltpu.CompilerParams(dimension_semantics=("arbitrary",) * n_axes,
                                vmem_limit_bytes=int(vmem_mb * MIB))


def _const(shape):
    nd = len(shape)
    return pl.BlockSpec(shape, lambda *_: (0,) * nd, pipeline_mode=pl.Buffered(1))


def _rows(tm, c):
    return pl.BlockSpec((tm, c), lambda i, *_: (i, 0))


def _mod_spec(grp):
    return pl.BlockSpec((None, 8, D), lambda i, *_: (grp.seg_of(i), 0, 0))


def _dot(a, b):
    return jnp.dot(a, b, preferred_element_type=f32)


def _dot_nt(a, b):
    return lax.dot_general(a, b, (((1,), (1,)), ((), ())), preferred_element_type=f32)


def _silu(x):
    return x * jax.nn.sigmoid(x)


def _ln_rows(x, g, b):
    mu = jnp.mean(x, axis=-1, keepdims=True)
    xc = x - mu
    var = jnp.mean(xc * xc, axis=-1, keepdims=True)
    return xc * lax.rsqrt(var + LN_EPS) * g + b


def _rms_rows(x, g):
    return x * lax.rsqrt(jnp.mean(x * x, axis=-1, keepdims=True) + RMS_EPS) * g


def _split_bf16(x):
    hi = x.astype(bf16)
    lo = (x - hi.astype(f32)).astype(bf16)
    return hi, lo


def _subs():
    return [slice(s * SUB, (s + 1) * SUB) for s in range(TM // SUB)]


def _mod_kernel(c_ref, w_ref, b_ref, o_ref):
    x = _silu(c_ref[...])
    xh, xl = _split_bf16(x)
    wh = w_ref[...].astype(bf16)
    o_ref[...] = _dot(xh, wh) + _dot(xl, wh) + b_ref[...]


def _modulation(cond8, w_mod, b_mod):
    nk = 6
    out = pl.pallas_call(
        _mod_kernel,
        grid=(DEPTH, nk),
        in_specs=[pl.BlockSpec((8, D), lambda l, k: (0, 0)),
                  pl.BlockSpec((None, D, D), lambda l, k: (l, 0, k)),
                  pl.BlockSpec((None, None, 1, D), lambda l, k: (l, k, 0, 0))],
        out_specs=pl.BlockSpec((None, None, 8, D), lambda l, k: (l, k, 0, 0)),
        out_shape=jax.ShapeDtypeStruct((DEPTH, nk, 8, D), f32),
        compiler_params=_cp(32, 2),
        name="modulation",
    )(cond8, w_mod, b_mod.reshape(DEPTH, nk, 1, D))
    out = jnp.transpose(out, (0, 2, 1, 3))
    return jnp.pad(out, ((0, 0), (0, 0), (0, 2), (0, 0)))


def _router_aff_t(h2, rw_ref):
    hh, hl = _split_bf16(h2)
    wh, wl = _split_bf16(rw_ref[...])
    lg = _dot(hh, wh) + _dot(hl, wh) + _dot(hh, wl)
    lg = jnp.transpose(lg)[:NE, :]
    m = jnp.max(lg, axis=0, keepdims=True)
    e = jnp.exp(lg - m)
    return e / jnp.sum(e, axis=0, keepdims=True)


def _close(d, x, m, lnp, rwt_ref, x1_ref, h2_ref, aff_ref, sl):
    x1 = _ln_rows(ALPHA * x + m[2:3] * d, lnp[0:1], lnp[1:2])
    x1_ref[sl, :] = x1
    h2 = x1 * (1.0 + m[4:5]) + m[3:4]
    h2_ref[sl, :] = h2.astype(bf16)
    aff_ref[sl.start // RT] = _router_aff_t(h2, rwt_ref)


def _close_outs(n):
    shapes = (jax.ShapeDtypeStruct((n, D), f32), jax.ShapeDtypeStruct((n, D), bf16),
              jax.ShapeDtypeStruct((n // RT, NE, RT), f32))
    specs = (_rows(TM, D), _rows(TM, D),
             pl.BlockSpec((TM // RT, NE, RT), lambda i, *_: (i, 0, 0)))
    return shapes, specs


def _proj_close_kernel(o_ref, x_ref, mod_ref, lnp_ref, wo_ref, rwt_ref, x1_ref, h2_ref, aff_ref):
    m, lnp = mod_ref[...], lnp_ref[...]
    subs = _subs()
    nxt = _dot(o_ref[subs[0], :], wo_ref[...])
    for s, sl in enumerate(subs):
        d = nxt
        if s + 1 < len(subs):
            nxt = _dot(o_ref[subs[s + 1], :], wo_ref[...])
        _close(d, x_ref[sl, :], m, lnp, rwt_ref, x1_ref, h2_ref, aff_ref, sl)


def _proj_close(grp, o, x, mod, lnp, wo, rwt):
    shapes, specs = _close_outs(grp.n)
    return pl.pallas_call(
        _proj_close_kernel,
        grid=(grp.n // TM,),
        in_specs=[_rows(TM, o.shape[1]), _rows(TM, D), _mod_spec(grp), _const((8, D)),
                  _const(wo.shape), _const((D, LANE))],
        out_specs=specs, out_shape=shapes,
        compiler_params=_cp(40), name="proj_close",
    )(o, x, mod, lnp, wo, rwt)


def _rope(x, cos, sin):
    w = x.shape[1]
    reps = w // LANE
    c = jnp.concatenate([cos] * reps, axis=1) if reps > 1 else cos
    s = jnp.concatenate([sin] * reps, axis=1) if reps > 1 else sin
    lane = lax.broadcasted_iota(i32, x.shape, 1)
    up = pltpu.roll(x, w - 16, 1)
    dn = pltpu.roll(x, 16, 1)
    partner = jnp.where((lane % 32) < 16, up, dn)
    return x * c + partner * s


def _rope_tables(length):
    t = jnp.arange(length)
    rows, cols = (t // GRID_W).astype(f32), (t % GRID_W).astype(f32)
    inv = ROPE_BASE ** (-jnp.arange(16, dtype=f32) / 16)
    ar, ac = rows[:, None] * inv[None, :], cols[:, None] * inv[None, :]
    cos = jnp.concatenate([jnp.cos(ar), jnp.cos(ar), jnp.cos(ac), jnp.cos(ac)], axis=1)
    sin = jnp.concatenate([-jnp.sin(ar), jnp.sin(ar), -jnp.sin(ac), jnp.sin(ac)], axis=1)
    return jnp.concatenate([cos, cos], axis=1), jnp.concatenate([sin, sin], axis=1)


def _tab_spec():
    per = DEC_SEQ // TM
    return pl.BlockSpec((TM, LANE), lambda i, *_: (i % per, 0))


def _mla_pre_kernel(rope, x_ref, mod_ref, wqa_ref, qn_ref, wqb_ref, wkva_ref, kvn_ref, wkvb_ref, *rest):
    if rope:
        cos_ref, sin_ref, qlat_ref, qctx_ref, kcat_ref, v_ref = rest
    else:
        qctx_ref, kcat_ref, v_ref, ckv_ref, kpe_ref = rest
    m = mod_ref[...]

    def front(sl):
        h = (x_ref[sl, :] * (1.0 + m[1:2]) + m[0:1]).astype(bf16)
        return _dot(h, wqa_ref[...]), _dot(h, wkva_ref[...])

    subs = _subs()
    nxt = front(subs[0])
    for s, sl in enumerate(subs):
        qa, kv = nxt
        if s + 1 < len(subs):
            nxt = front(subs[s + 1])
        qa = _rms_rows(qa, qn_ref[...])
        q = _dot(qa.astype(bf16), wqb_ref[...])
        ckv = _rms_rows(kv[:, :MLA_KV_LORA], kvn_ref[...])
        kpe2 = kv[:, MLA_KV_LORA:MLA_KV_LORA + LANE]
        kvb = _dot(ckv.astype(bf16), wkvb_ref[...])
        q_pe = q[:, MLA_HEADS * MLA_NOPE:]
        if rope:
            cos, sin = cos_ref[sl, :], sin_ref[sl, :]
            q_pe_rot = _rope(q_pe, cos, sin).astype(bf16)
            kpe2 = _rope(kpe2, cos, sin)
        else:
            ckv_ref[sl, :] = ckv
            kpe_ref[sl, :] = kv[:, MLA_KV_LORA:MLA_KV_LORA + MLA_ROPE]
        q_pe = q_pe.astype(bf16)
        qn = q[:, :MLA_HEADS * MLA_NOPE].astype(bf16)
        kn = kvb[:, :MLA_HEADS * MLA_NOPE].astype(bf16)
        v_ref[sl, :] = kvb[:, MLA_HEADS * MLA_NOPE:].astype(bf16)
        lane = lax.broadcasted_iota(i32, kpe2.shape, 1)
        kpe_lo = jnp.where(lane < MLA_ROPE, kpe2, 0.0).astype(bf16)
        kpe_hi = jnp.where(lane >= MLA_ROPE, kpe2, 0.0).astype(bf16)
        for hh in range(MLA_HEADS):
            a, b = hh * 256, hh * 256 + LANE
            pr = (hh // 2) * LANE
            qctx_ref[sl, a:b] = qn[:, hh * LANE:(hh + 1) * LANE]
            qctx_ref[sl, b:b + LANE] = q_pe[:, pr:pr + LANE]
            if rope:
                qlat_ref[sl, a:b] = qn[:, hh * LANE:(hh + 1) * LANE]
                qlat_ref[sl, b:b + LANE] = q_pe_rot[:, pr:pr + LANE]
            kcat_ref[sl, a:b] = kn[:, hh * LANE:(hh + 1) * LANE]
            kcat_ref[sl, b:b + LANE] = kpe_lo if hh % 2 == 0 else kpe_hi


def _mla_pre(grp, x, mod, w, rope_tabs):
    n = grp.n
    ins = [x, mod] + list(w)
    specs = [_rows(TM, D), _mod_spec(grp)] + [_const(a.shape) for a in w]
    wide = jax.ShapeDtypeStruct((n, 2 * D), bf16)
    if rope_tabs is not None:
        ins += list(rope_tabs)
        specs += [_tab_spec(), _tab_spec()]
        shapes = (wide, wide, wide, jax.ShapeDtypeStruct((n, D), bf16))
        ospecs = (_rows(TM, 2 * D), _rows(TM, 2 * D), _rows(TM, 2 * D), _rows(TM, D))
    else:
        shapes = (wide, wide, jax.ShapeDtypeStruct((n, D), bf16),
                  jax.ShapeDtypeStruct((n, MLA_KV_LORA), f32), jax.ShapeDtypeStruct((n, MLA_ROPE), f32))
        ospecs = (_rows(TM, 2 * D), _rows(TM, 2 * D), _rows(TM, D), _rows(TM, MLA_KV_LORA), _rows(TM, MLA_ROPE))
    return pl.pallas_call(
        functools.partial(_mla_pre_kernel, rope_tabs is not None),
        grid=(n // TM,), in_specs=specs, out_specs=ospecs, out_shape=shapes,
        compiler_params=_cp(48), name="mla_pre",
    )(*ins)


def _mla_ctx_kernel(ckv_ref, kpe2_ref, wkvb_ref, kcat_ref, v_ref):
    kvb = _dot(ckv_ref[...].astype(bf16), wkvb_ref[...])
    kn = kvb[:, :MLA_HEADS * MLA_NOPE].astype(bf16)
    v_ref[...] = kvb[:, MLA_HEADS * MLA_NOPE:].astype(bf16)
    kpe2 = kpe2_ref[...]
    lane = lax.broadcasted_iota(i32, kpe2.shape, 1)
    kpe_lo = jnp.where(lane < MLA_ROPE, kpe2, 0.0).astype(bf16)
    kpe_hi = jnp.where(lane >= MLA_ROPE, kpe2, 0.0).astype(bf16)
    for hh in range(MLA_HEADS):
        a, b = hh * 256, hh * 256 + LANE
        kcat_ref[:, a:b] = kn[:, hh * LANE:(hh + 1) * LANE]
        kcat_ref[:, b:b + LANE] = kpe_lo if hh % 2 == 0 else kpe_hi


def _mla_ctx(ckv, kpe2, wkvb):
    n = ckv.shape[0]
    return pl.pallas_call(
        _mla_ctx_kernel, grid=(n // PAST,),
        in_specs=[_rows(PAST, MLA_KV_LORA), _rows(PAST, LANE), _const(wkvb.shape)],
        out_specs=(_rows(PAST, 2 * D), _rows(PAST, D)),
        out_shape=(jax.ShapeDtypeStruct((n, 2 * D), bf16), jax.ShapeDtypeStruct((n, D), bf16)),
        compiler_params=_cp(24), name="mla_ctx",
    )(ckv, kpe2, wkvb)


CTX_SEQS = 2


def _mla_attn_p_kernel(q_ref, k_ref, v_ref, o_ref):
    items = [(slice(sq * SEQ, (sq + 1) * SEQ), hh) for sq in range(CTX_SEQS) for hh in range(MLA_HEADS)]

    def scores(item):
        rows, hh = item
        return _dot_nt(q_ref[rows, hh * 256:(hh + 1) * 256], k_ref[rows, hh * 256:(hh + 1) * 256])

    nxt = scores(items[0])
    for n, (rows, hh) in enumerate(items):
        s = nxt
        if n + 1 < len(items):
            nxt = scores(items[n + 1])
        e = jnp.exp2((s - jnp.max(s, axis=-1, keepdims=True)) * MLA_C2)
        l = jnp.sum(e, axis=-1, keepdims=True)
        o = _dot(e.astype(bf16), v_ref[rows, hh * LANE:(hh + 1) * LANE]) / l
        o_ref[rows, hh * LANE:(hh + 1) * LANE] = o.astype(bf16)


def _mla_attn_p(q, k, v):
    n = q.shape[0]
    rows = CTX_SEQS * SEQ
    return pl.pallas_call(
        _mla_attn_p_kernel, grid=(n // rows,),
        in_specs=[_rows(rows, 2 * D), _rows(rows, 2 * D), _rows(rows, D)],
        out_specs=_rows(rows, D), out_shape=jax.ShapeDtypeStruct((n, D), bf16),
        compiler_params=_cp(32), name="mla_attn_ctx",
    )(q, k, v)


MLA_QT = 256


def _mla_attn_s_kernel(ql_ref, qc_ref, kl_ref, vl_ref, kc_ref, vc_ref, o_ref):
    def scores(hh):
        a, b = hh * 256, (hh + 1) * 256
        return (_dot_nt(ql_ref[:, a:b], kl_ref[:, a:b]), _dot_nt(qc_ref[:, a:b], kc_ref[:, a:b]))

    nxt = scores(0)
    for hh in range(MLA_HEADS):
        s1, s2 = nxt
        if hh + 1 < MLA_HEADS:
            nxt = scores(hh + 1)
        m = jnp.maximum(jnp.max(s1, axis=-1, keepdims=True), jnp.max(s2, axis=-1, keepdims=True))
        e1, e2 = jnp.exp2((s1 - m) * MLA_C2), jnp.exp2((s2 - m) * MLA_C2)
        l = jnp.sum(e1, axis=-1, keepdims=True) + jnp.sum(e2, axis=-1, keepdims=True)
        o = (_dot(e1.astype(bf16), vl_ref[:, hh * LANE:(hh + 1) * LANE])
             + _dot(e2.astype(bf16), vc_ref[:, hh * LANE:(hh + 1) * LANE])) / l
        o_ref[:, hh * LANE:(hh + 1) * LANE] = o.astype(bf16)


def _mla_attn_s(ql, qc, kl, vl, kc, vc):
    nq = DEC_SEQ // MLA_QT
    qs = pl.BlockSpec((MLA_QT, 2 * D), lambda b, i: (b * nq + i, 0))
    return pl.pallas_call(
        _mla_attn_s_kernel, grid=(DEC_BATCH, nq),
        in_specs=[qs, qs,
                  pl.BlockSpec((DEC_SEQ, 2 * D), lambda b, i: (b, 0)),
                  pl.BlockSpec((DEC_SEQ, D), lambda b, i: (b, 0)),
                  pl.BlockSpec((PAST, 2 * D), lambda b, i: (b, 0)),
                  pl.BlockSpec((PAST, D), lambda b, i: (b, 0))],
        out_specs=pl.BlockSpec((MLA_QT, D), lambda b, i: (b * nq + i, 0)),
        out_shape=jax.ShapeDtypeStruct((NS_TOK, D), bf16),
        compiler_params=_cp(48, 2), name="mla_attn_lat",
    )(ql, qc, kl, vl, kc, vc)


def _gelu_tanh(x):
    return 0.5 * x * (1.0 + jnp.tanh(math.sqrt(2.0 / math.pi) * (x + 0.044715 * (x * x * x))))


GM_CW = GM_HALF // GM_GROUPS


def _gmlp_kernel(x_ref, mod_ref, lnp_ref, win_ref, bin_ref, vg_ref, vb_ref, ws_ref, bs_ref, wout_ref, bout_ref,
                 rwt_ref, x1_ref, h2_ref, aff_ref, gated_ref, vz_ref):
    m, lnp = mod_ref[...], lnp_ref[...]
    for s, sl in enumerate(_subs()):
        x = x_ref[sl, :]
        h = (x * (1.0 + m[1:2]) + m[0:1]).astype(bf16)

        def mm(j):
            return _dot(h, win_ref[:, j * GM_CW:(j + 1) * GM_CW]) + bin_ref[:, j * GM_CW:(j + 1) * GM_CW]

        s1 = jnp.zeros((SUB, 1), f32)
        s2 = jnp.zeros((SUB, 1), f32)
        nxt = mm(GM_GROUPS)
        for g in range(GM_GROUPS):
            cur = nxt
            nxt = mm(GM_GROUPS + g + 1) if g + 1 < GM_GROUPS else mm(0)
            z = _gelu_tanh(cur)
            s1 = s1 + jnp.sum(z, axis=-1, keepdims=True)
            s2 = s2 + jnp.sum(z * z, axis=-1, keepdims=True)
            vz_ref[s, :, g * GM_CW:(g + 1) * GM_CW] = z
        mu = s1 * (1.0 / GM_HALF)
        rstd = lax.rsqrt(s2 * (1.0 / GM_HALF) - mu * mu + LN_EPS)
        for g in range(GM_GROUPS):
            cur = nxt
            if g + 1 < GM_GROUPS:
                nxt = mm(g + 1)
            cols = slice(g * GM_CW, (g + 1) * GM_CW)
            u = _gelu_tanh(cur)
            v = ((vz_ref[s, :, cols] - mu) * rstd * vg_ref[:, cols] + vb_ref[:, cols]).astype(bf16)
            for c in range(SUB // GM_CHUNK):
                r0, r1 = c * GM_CHUNK, (c + 1) * GM_CHUNK
                sv = _dot(ws_ref[g], v[r0:r1, :]) + bs_ref[:, g:g + 1]
                gated_ref[sl.start + r0:sl.start + r1, cols] = (u[r0:r1, :] * sv).astype(bf16)
        d = _dot(gated_ref[sl, :], wout_ref[...]) + bout_ref[...]
        _close(d, x, m, lnp, rwt_ref, x1_ref, h2_ref, aff_ref, sl)


def _gmlp_layer(grp, x, mod, lnp, w, rwt):
    shapes, specs = _close_outs(grp.n)
    return pl.pallas_call(
        _gmlp_kernel, grid=(grp.n // TM,),
        in_specs=[_rows(TM, D), _mod_spec(grp), _const((8, D))] + [_const(a.shape) for a in w]
        + [_const((D, LANE))],
        out_specs=specs, out_shape=shapes,
        scratch_shapes=[pltpu.VMEM((TM, GM_HALF), bf16), pltpu.VMEM((TM // SUB, SUB, GM_HALF), f32)],
        compiler_params=_cp(56), name="gmlp_layer",
    )(x, mod, lnp, *w, rwt)


HALO = 16
CONV_RB = 64
CONV_LW = 256


def _conv_glu_kernel(x_ref, mod_ref, w_ref, b_ref, a_ref):
    m = mod_ref[...]

    def front(sl):
        h = (x_ref[sl, :] * (1.0 + m[1:2]) + m[0:1]).astype(bf16)
        return _dot(h, w_ref[...])

    subs = _subs()
    nxt = front(subs[0])
    for s, sl in enumerate(subs):
        a = nxt + b_ref[...]
        if s + 1 < len(subs):
            nxt = front(subs[s + 1])
        a_ref[sl, :] = a[:, :D] * jax.nn.sigmoid(a[:, D:])


def _conv_glu(grp, x, mod, w, b):
    return pl.pallas_call(
        _conv_glu_kernel, grid=(grp.n // TM,),
        in_specs=[_rows(TM, D), _mod_spec(grp), _const(w.shape), _const(b.shape)],
        out_specs=_rows(TM, D), out_shape=jax.ShapeDtypeStruct((grp.n, D), f32),
        compiler_params=_cp(40), name="conv_glu",
    )(x, mod, w, b)


def _conv_close_kernel(seq_subs, ap_ref, a_ref, an_ref, x_ref, mod_ref, lnp_ref, wdw_ref, bdw_ref, ng_ref, nb_ref,
                       w2_ref, b2_ref, rwt_ref, x1_ref, h2_ref, aff_ref, pad_ref, act_ref, cout_ref, shift_ref):
    i = pl.program_id(0)
    nsub = TM // SUB
    bdw, ng, nb = bdw_ref[...], ng_ref[...], nb_ref[...]
    m, lnp = mod_ref[...], lnp_ref[...]
    for s, sl in enumerate(_subs()):
        gsub = i * nsub + s
        prev = a_ref[sl.start - HALO:sl.start, :] if s > 0 else ap_ref[...]
        nxt = a_ref[sl.stop:sl.stop + HALO, :] if s < nsub - 1 else an_ref[...]
        pad = pad_ref.at[s]
        pad[0:HALO, :] = jnp.where((gsub % seq_subs) != 0, prev, 0.0)
        pad[HALO:HALO + SUB, :] = a_ref[sl, :]
        pad[HALO + SUB:, :] = jnp.where((gsub % seq_subs) != seq_subs - 1, nxt, 0.0)
        span = SUB + 24
        for r in range(1, 8):
            shift_ref[s, r - 1, :, :] = pad[r:r + span, :]
        for rb in range(SUB // CONV_RB):
            r0 = rb * CONV_RB
            for lc in range(D // CONV_LW):
                lanes = slice(lc * CONV_LW, (lc + 1) * CONV_LW)
                acc = jnp.zeros((CONV_RB, CONV_LW), f32)
                for k in range(CONV_W):
                    mm, r = (k + 1) // 8, (k + 1) % 8
                    rows = slice(r0 + 8 * mm, r0 + 8 * mm + CONV_RB)
                    win = pad[rows, lanes] if r == 0 else shift_ref[s, r - 1, rows, lanes]
                    acc = acc + wdw_ref[k:k + 1, lanes] * win
                cout_ref[s, r0:r0 + CONV_RB, lanes] = acc
            y = _silu(_ln_rows(cout_ref[s, r0:r0 + CONV_RB, :] + bdw, ng, nb))
            act_ref[sl.start + r0:sl.start + r0 + CONV_RB, :] = y.astype(bf16)
        d = _dot(act_ref[sl, :], w2_ref[...]) + b2_ref[...]
        _close(d, x_ref[sl, :], m, lnp, rwt_ref, x1_ref, h2_ref, aff_ref, sl)


def _conv_close(grp, a, x, mod, lnp, w, rwt):
    n = grp.n
    hb = TM // HALO
    last = n // HALO - 1
    shapes, specs = _close_outs(n)
    return pl.pallas_call(
        functools.partial(_conv_close_kernel, grp.seq // SUB), grid=(n // TM,),
        in_specs=[pl.BlockSpec((HALO, D), lambda i: (jnp.maximum(i * hb - 1, 0), 0)),
                  _rows(TM, D),
                  pl.BlockSpec((HALO, D), lambda i: (jnp.minimum((i + 1) * hb, last), 0)),
                  _rows(TM, D), _mod_spec(grp), _const((8, D))]
        + [_const(t.shape) for t in w] + [_const((D, LANE))],
        out_specs=specs, out_shape=shapes,
        scratch_shapes=[pltpu.VMEM((TM // SUB, SUB + 2 * HALO, D), f32), pltpu.VMEM((TM, D), bf16),
                        pltpu.VMEM((TM // SUB, SUB, D), f32), pltpu.VMEM((TM // SUB, 7, SUB + 24, D), f32)],
        compiler_params=_cp(48), name="conv_close",
    )(a, a, a, x, mod, lnp, *w, rwt)


def _swa_pre_kernel(rope, x_ref, mod_ref, wq_ref, wk_ref, wv_ref, *rest):
    if rope:
        cos_ref, sin_ref, qrot_ref, qraw_ref, kd_ref, vd_ref = rest
    else:
        qraw_ref, kd_ref, vd_ref, k_ref, v_ref = rest
    m = mod_ref[...]

    def front(sl):
        h = (x_ref[sl, :] * (1.0 + m[1:2]) + m[0:1]).astype(bf16)
        return _dot(h, wq_ref[...]), _dot(h, wk_ref[...]), _dot(h, wv_ref[...])

    subs = _subs()
    nxt = front(subs[0])
    for s, sl in enumerate(subs):
        q, kd, vd = nxt
        if s + 1 < len(subs):
            nxt = front(subs[s + 1])
        qraw_ref[sl, :] = q.astype(bf16)
        vd_ref[sl, :] = vd.astype(bf16)
        if rope:
            cos, sin = cos_ref[sl, :], sin_ref[sl, :]
            qrot_ref[sl, :] = _rope(q, cos, sin).astype(bf16)
            kd_ref[sl, :] = _rope(kd, cos, sin).astype(bf16)
        else:
            kd_ref[sl, :] = kd.astype(bf16)
            lane = lax.broadcasted_iota(i32, (SUB, LANE), 1)
            for j in range(SWA_KV // 2):
                lo, hi = 2 * j * LANE, (2 * j + 1) * LANE
                k_ref[sl, j * LANE:(j + 1) * LANE] = jnp.where(lane < SWA_HD, kd[:, lo:lo + LANE], kd[:, hi:hi + LANE])
                v_ref[sl, j * LANE:(j + 1) * LANE] = jnp.where(lane < SWA_HD, vd[:, lo:lo + LANE], vd[:, hi:hi + LANE])


def _swa_pre(grp, x, mod, w, rope_tabs):
    n = grp.n
    ins = [x, mod] + list(w)
    specs = [_rows(TM, D), _mod_spec(grp)] + [_const(a.shape) for a in w]
    kw = 2 * SWA_KV * SWA_HD
    qs, ks = jax.ShapeDtypeStruct((n, D), bf16), jax.ShapeDtypeStruct((n, kw), bf16)
    if rope_tabs is not None:
        ins += list(rope_tabs)
        specs += [_tab_spec(), _tab_spec()]
        shapes = (qs, qs, ks, ks)
        ospecs = (_rows(TM, D), _rows(TM, D), _rows(TM, kw), _rows(TM, kw))
    else:
        nat = jax.ShapeDtypeStruct((n, SWA_KV * SWA_HD), f32)
        shapes = (qs, ks, ks, nat, nat)
        ospecs = (_rows(TM, D), _rows(TM, kw), _rows(TM, kw), _rows(TM, SWA_KV * SWA_HD), _rows(TM, SWA_KV * SWA_HD))
    return pl.pallas_call(
        functools.partial(_swa_pre_kernel, rope_tabs is not None),
        grid=(n // TM,), in_specs=specs, out_specs=ospecs, out_shape=shapes,
        compiler_params=_cp(40), name="swa_pre",
    )(*ins)


def _half_mask(x, parity):
    lane = lax.broadcasted_iota(i32, x.shape, 1)
    keep = (lane < SWA_HD) if parity == 0 else (lane >= SWA_HD)
    return jnp.where(keep, x, jnp.zeros_like(x))


def _swa_attn_p_kernel(sink_ref, q_ref, kd_ref, vd_ref, o_ref):
    items = [(slice(sq * SEQ, (sq + 1) * SEQ), hd) for sq in range(CTX_SEQS) for hd in range(SWA_HEADS)]

    def scores(item):
        rows, hd = item
        pair, par = hd // 2, hd % 2
        g = hd // (SWA_HEADS // SWA_KV)
        return _dot_nt(q_ref[rows, pair * LANE:(pair + 1) * LANE], _half_mask(kd_ref[rows, g * LANE:(g + 1) * LANE], par))

    nxt = scores(items[0])
    acc = None
    for n, (rows, hd) in enumerate(items):
        pair, par = hd // 2, hd % 2
        g = hd // (SWA_HEADS // SWA_KV)
        s = nxt
        if n + 1 < len(items):
            nxt = scores(items[n + 1])
        sk = sink_ref[hd] * (1.0 / SWA_SCALE)
        m = jnp.maximum(jnp.max(s, axis=-1, keepdims=True), sk)
        e = jnp.exp2((s - m) * SWA_C2)
        l = jnp.sum(e, axis=-1, keepdims=True) + jnp.exp2((sk - m) * SWA_C2)
        o = _dot(e.astype(bf16), _half_mask(vd_ref[rows, g * LANE:(g + 1) * LANE], par)) / l
        acc = o if par == 0 else acc + o
        if par == 1:
            o_ref[rows, pair * LANE:(pair + 1) * LANE] = acc.astype(bf16)


def _swa_attn_p(sink, q, kd, vd):
    n = q.shape[0]
    kw = kd.shape[1]
    rows = CTX_SEQS * SEQ
    return pl.pallas_call(
        _swa_attn_p_kernel,
        grid_spec=pltpu.PrefetchScalarGridSpec(
            num_scalar_prefetch=1, grid=(n // rows,),
            in_specs=[_rows(rows, D), _rows(rows, kw), _rows(rows, kw)],
            out_specs=_rows(rows, D)),
        out_shape=jax.ShapeDtypeStruct((n, D), bf16),
        compiler_params=_cp(32), name="swa_attn_ctx",
    )(sink, q, kd, vd)


SWA_QB = 128


SWA_QBLOCKS = 2


def _swa_attn_s_kernel(sink_ref, qr_ref, qw_ref, kd_ref, vd_ref, kc_ref, vc_ref, o_ref):
    span = 3 * SWA_QB
    blocks = []
    for bq in range(SWA_QBLOCKS):
        nblk = pl.program_id(1) * SWA_QBLOCKS + bq
        start = pl.multiple_of(jnp.clip((nblk - 1) * SWA_QB, 0, DEC_SEQ - span), SWA_QB)
        qpos = nblk * SWA_QB + lax.broadcasted_iota(i32, (SWA_QB, span), 0)
        kpos = start + lax.broadcasted_iota(i32, (SWA_QB, span), 1)
        blocks.append((slice(bq * SWA_QB, (bq + 1) * SWA_QB), start, jnp.abs(kpos - qpos) <= SWA_WIN))
    items = [(bq, hd) for bq in range(SWA_QBLOCKS) for hd in range(SWA_HEADS)]

    def scores(item):
        bq, hd = item
        rows, start, _ = blocks[bq]
        pair, par = hd // 2, hd % 2
        g = hd // (SWA_HEADS // SWA_KV)
        cols = slice(g * LANE, (g + 1) * LANE)
        s1 = _dot_nt(qr_ref[rows, pair * LANE:(pair + 1) * LANE], _half_mask(kd_ref[pl.ds(start, span), cols], par))
        s2 = _dot_nt(qw_ref[rows, pair * LANE:(pair + 1) * LANE], _half_mask(kc_ref[:, cols], par))
        return s1, s2

    nxt = scores(items[0])
    acc = None
    for n, (bq, hd) in enumerate(items):
        rows, start, band = blocks[bq]
        pair, par = hd // 2, hd % 2
        g = hd // (SWA_HEADS // SWA_KV)
        cols = slice(g * LANE, (g + 1) * LANE)
        s1, s2 = nxt
        if n + 1 < len(items):
            nxt = scores(items[n + 1])
        s1 = jnp.where(band, s1, NEG_INF)
        sk = sink_ref[hd] * (1.0 / SWA_SCALE)
        m = jnp.maximum(jnp.maximum(jnp.max(s1, axis=-1, keepdims=True), jnp.max(s2, axis=-1, keepdims=True)), sk)
        e1, e2 = jnp.exp2((s1 - m) * SWA_C2), jnp.exp2((s2 - m) * SWA_C2)
        l = (jnp.sum(e1, axis=-1, keepdims=True) + jnp.sum(e2, axis=-1, keepdims=True)
             + jnp.exp2((sk - m) * SWA_C2))
        o = (_dot(e1.astype(bf16), _half_mask(vd_ref[pl.ds(start, span), cols], par))
             + _dot(e2.astype(bf16), _half_mask(vc_ref[:, cols], par))) / l
        acc = o if par == 0 else acc + o
        if par == 1:
            o_ref[rows, pair * LANE:(pair + 1) * LANE] = acc.astype(bf16)


def _swa_attn_s(sink, qr, qw, kd, vd, kc, vc):
    qrows = SWA_QB * SWA_QBLOCKS
    nq = DEC_SEQ // qrows
    kw = kd.shape[1]
    qs = pl.BlockSpec((qrows, D), lambda b, i, *_: (b * nq + i, 0))
    full = pl.BlockSpec((DEC_SEQ, kw), lambda b, i, *_: (b, 0))
    ctx = pl.BlockSpec((PAST, kw), lambda b, i, *_: (b, 0))
    return pl.pallas_call(
        _swa_attn_s_kernel,
        grid_spec=pltpu.PrefetchScalarGridSpec(
            num_scalar_prefetch=1, grid=(DEC_BATCH, nq),
            in_specs=[qs, qs, full, full, ctx, ctx],
            out_specs=pl.BlockSpec((qrows, D), lambda b, i, *_: (b * nq + i, 0))),
        out_shape=jax.ShapeDtypeStruct((NS_TOK, D), bf16),
        compiler_params=_cp(32, 2), name="swa_attn_lat",
    )(sink, qr, qw, kd, vd, kc, vc)


def _excl_prefix(mask_f, nb, tri, blk):
    m2 = mask_f.reshape(nb * NE, RT)
    within = _dot(m2.astype(bf16), tri)
    tot = jnp.sum(m2, axis=1, keepdims=True)
    totb = jnp.broadcast_to(tot, (nb * NE, LANE)).astype(bf16)
    offs = _dot(blk, totb)
    return (within + offs[:, 0:1]).reshape(nb, NE, RT), offs.reshape(nb, NE, LANE)


def _route_kernel(cap, nb, aff_ref, dest_ref, gate_ref, ws_ref, nq_ref, destt_ref):
    a = aff_ref[...]

    def as_f32(bits):
        return pltpu.bitcast(bits, f32)[None]

    def count_ge(th):
        c = jnp.sum(jnp.where(a >= as_f32(th), 1.0, 0.0), axis=0)
        return jnp.sum(c, axis=1, keepdims=True)

    def body(_, c):
        lo, hi = c
        mid = lo + ((hi - lo + 1) >> 1)
        ok = count_ge(mid) >= cap
        return jnp.where(ok, mid, lo), jnp.where(ok, hi, mid - 1)

    lo0 = jnp.zeros((NE, 1), i32)
    hi0 = jnp.full((NE, 1), 0x7F800000, i32)
    thr, _ = lax.fori_loop(0, 31, body, (lo0, hi0))

    r = lax.broadcasted_iota(i32, (RT, RT), 0)
    c = lax.broadcasted_iota(i32, (RT, RT), 1)
    tri = jnp.where(r < c, 1.0, 0.0).astype(bf16)
    rr = lax.broadcasted_iota(i32, (nb * NE, nb * NE), 0)
    cc = lax.broadcasted_iota(i32, (nb * NE, nb * NE), 1)
    blk = jnp.where(((rr % NE) == (cc % NE)) & (cc < rr), 1.0, 0.0).astype(bf16)

    gt = a > as_f32(thr)
    eq = a == as_f32(thr)
    n_gt = jnp.sum(jnp.sum(jnp.where(gt, 1.0, 0.0), axis=0), axis=1, keepdims=True)
    need = cap - n_gt
    tie_rank, _ = _excl_prefix(jnp.where(eq, 1.0, 0.0), nb, tri, blk)
    sel = gt | (eq & (tie_rank < need[None]))
    pos, offs = _excl_prefix(jnp.where(sel, 1.0, 0.0), nb, tri, blk)
    dest_ref[...] = jnp.where(sel, pos, -1.0).astype(i32)
    gate_ref[...] = jnp.where(sel, a, 0.0)
    ct_end = jnp.concatenate([offs[1:], jnp.full((1, NE, LANE), float(cap), f32)], axis=0)
    ws = jnp.floor(offs * (1.0 / 16)) * 16.0
    ws_ref[...] = ws.astype(i32)
    nchunk = jnp.floor((ct_end - ws + (WIN - 1)) * (1.0 / WIN))
    nq_ref[...] = jnp.broadcast_to(jnp.max(nchunk, axis=1, keepdims=True), (nb, NE, LANE)).astype(i32)
    destp = jnp.where(sel, pos + 1.0, 0.0)
    zpad = jnp.zeros((LANE - NE, RT), f32)
    for b in range(nb):
        destt_ref[b * RT:(b + 1) * RT, :] = jnp.transpose(jnp.concatenate([destp[b], zpad], axis=0))


def _route(grp, aff3):
    nb = grp.n // RT
    full = pl.BlockSpec((nb, NE, RT), lambda: (0, 0, 0))
    small = pl.BlockSpec((nb, NE, LANE), lambda: (0, 0, 0))
    dest3, gate3, ws3, nq3, destt = pl.pallas_call(
        functools.partial(_route_kernel, grp.cap, nb),
        in_specs=[full], out_specs=(full, full, small, small, pl.BlockSpec((grp.n, LANE), lambda: (0, 0))),
        out_shape=(jax.ShapeDtypeStruct((nb, NE, RT), i32), jax.ShapeDtypeStruct((nb, NE, RT), f32),
                   jax.ShapeDtypeStruct((nb, NE, LANE), i32), jax.ShapeDtypeStruct((nb, NE, LANE), i32),
                   jax.ShapeDtypeStruct((grp.n, LANE), f32)),
        compiler_params=pltpu.CompilerParams(vmem_limit_bytes=48 * MIB), name="route",
    )(aff3)
    return dest3, gate3, ws3[:, :, 0], nq3[:, 0, 0], destt


def _dispatch_kernel(cap, eg, ws_ref, nq_ref, h2_ref, dest_ref, gate_ref, xe_ref, gcol_ref, hot_ref):
    g, step = pl.program_id(0), pl.program_id(1)

    @pl.when(step == 0)
    def _():
        xe_ref[...] = jnp.zeros_like(xe_ref)
        gcol_ref[...] = jnp.zeros_like(gcol_ref)

    wrow = lax.broadcasted_iota(i32, (WIN, RT), 0)
    for b in range(DISPATCH_BLOCKS):
        tb = step * DISPATCH_BLOCKS + b
        hot = hot_ref.at[b]

        def body(q, carry, tb=tb, b=b, hot=hot):
            starts, gsums = [], []
            for el in range(eg):
                e = g * eg + el
                want = ws_ref[tb, e] + q * WIN
                st = pl.multiple_of(jnp.minimum(want, cap - WIN), 16)
                row = st + wrow
                hit = dest_ref[b, pl.ds(e, 1), :] == jnp.where(row >= want, row, -7)
                hot[el * WIN:(el + 1) * WIN, :] = jnp.where(hit, 1.0, 0.0).astype(bf16)
                gsums.append(jnp.sum(jnp.where(hit, gate_ref[b, pl.ds(e, 1), :], 0.0), axis=1, keepdims=True))
                starts.append(st)
            part = _dot(hot[...], h2_ref[b * RT:(b + 1) * RT, :])
            for el in range(eg):
                dst = pl.ds(el * cap + starts[el], WIN)
                xe_ref[dst, :] = xe_ref[dst, :] + part[el * WIN:(el + 1) * WIN, :].astype(bf16)
                gcol_ref[dst, :] = gcol_ref[dst, :] + jnp.broadcast_to(gsums[el], (WIN, LANE))
            return carry

        lax.fori_loop(0, nq_ref[tb], body, 0)


def _dispatch(grp, wstart, nq, h2, dest3, gate3):
    eg = ROWS_PER_PASS // grp.cap
    nb = grp.n // (RT * DISPATCH_BLOCKS)
    tab = pl.BlockSpec((DISPATCH_BLOCKS, NE, RT), lambda g, t, *_: (t, 0, 0))
    return pl.pallas_call(
        functools.partial(_dispatch_kernel, grp.cap, eg),
        grid_spec=pltpu.PrefetchScalarGridSpec(
            num_scalar_prefetch=2, grid=(NE // eg, nb),
            in_specs=[pl.BlockSpec((RT * DISPATCH_BLOCKS, D), lambda g, t, *_: (t, 0)), tab, tab],
            out_specs=(pl.BlockSpec((ROWS_PER_PASS, D), lambda g, t, *_: (g, 0)),
                       pl.BlockSpec((ROWS_PER_PASS, LANE), lambda g, t, *_: (g, 0))),
            scratch_shapes=[pltpu.VMEM((DISPATCH_BLOCKS, eg * WIN, RT), bf16)]),
        out_shape=(jax.ShapeDtypeStruct((NE * grp.cap, D), bf16), jax.ShapeDtypeStruct((NE * grp.cap, LANE), f32)),
        compiler_params=_cp(56, 2), name="moe_dispatch",
    )(wstart, nq, h2, dest3, gate3)


FFN_RB = 256


def _ffn_kernel(xp_ref, gp_ref, xs_ref, gs_ref, wg_ref, wu_ref, wd_ref, yp_ref, ys_ref, wgb, wub, wdb):
    wgb[...] = wg_ref[...].astype(bf16)
    wub[...] = wu_ref[...].astype(bf16)
    wdb[...] = wd_ref[...].astype(bf16)
    for x_ref, g_ref, y_ref, cap in ((xp_ref, gp_ref, yp_ref, CTX.cap), (xs_ref, gs_ref, ys_ref, LAT.cap)):
        for r in range(cap // FFN_RB):
            sl = slice(r * FFN_RB, (r + 1) * FFN_RB)
            x = x_ref[sl, :]
            hid = (_silu(_dot(x, wgb[...])) * _dot(x, wub[...])).astype(bf16)
            gate = jnp.concatenate([g_ref[sl, :]] * (D // LANE), axis=1)
            y_ref[sl, :] = (_dot(hid, wdb[...]) * gate).astype(bf16)


def _ffn(layer, xp, gp, xs, gs, wg, wu, wd):
    wspec = pl.BlockSpec((None, None, D, FF), lambda e: (layer, e, 0, 0))
    cp, cs = CTX.cap, LAT.cap
    return pl.pallas_call(
        _ffn_kernel, grid=(NE,),
        in_specs=[pl.BlockSpec((cp, D), lambda e: (e, 0)), pl.BlockSpec((cp, LANE), lambda e: (e, 0)),
                  pl.BlockSpec((cs, D), lambda e: (e, 0)), pl.BlockSpec((cs, LANE), lambda e: (e, 0)),
                  wspec, wspec, pl.BlockSpec((None, None, FF, D), lambda e: (layer, e, 0, 0))],
        out_specs=(pl.BlockSpec((cp, D), lambda e: (e, 0)), pl.BlockSpec((cs, D), lambda e: (e, 0))),
        out_shape=(jax.ShapeDtypeStruct((NE * cp, D), bf16), jax.ShapeDtypeStruct((NE * cs, D), bf16)),
        scratch_shapes=[pltpu.VMEM((D, FF), bf16), pltpu.VMEM((D, FF), bf16), pltpu.VMEM((FF, D), bf16)],
        compiler_params=_cp(56), name="moe_ffn",
    )(xp, gp, xs, gs, wg, wu, wd)


def _combine_kernel(cap, ws_ref, nq_ref, yg_ref, dt_ref, x1_ref, mod_ref, lnp_ref, x2_ref, win_ref, y_ref):
    step = pl.program_id(0)
    kw = NE * WIN
    lane = lax.broadcasted_iota(i32, (1, kw), 1)
    lane_e, lane_w = lane // WIN, lane % WIN
    er = lax.broadcasted_iota(i32, (2 * LANE, kw), 0)
    ec = lax.broadcasted_iota(i32, (2 * LANE, kw), 1) // WIN
    spread = jnp.where(er == ec, 32.0, jnp.where(er - LANE == ec, 1.0, 0.0)).astype(bf16)

    def spread_rows(rows):
        d = dt_ref[rows, :]
        hi = jnp.floor(d * (1.0 / 32))
        hilo = jnp.concatenate([hi, d - 32.0 * hi], axis=1).astype(bf16)
        return _dot(hilo, spread)

    def chunk(b, tb, q, destp):
        win = win_ref.at[b]
        tgt = jnp.full((1, kw), -1, i32)
        for e in range(NE):
            want = ws_ref[tb, e] + q * WIN
            st = pl.multiple_of(jnp.minimum(want, cap - WIN), 16)
            win[e * WIN:(e + 1) * WIN, :] = yg_ref[pl.ds(e * cap + st, WIN), :]
            row = st + lane_w
            tgt = jnp.where(lane_e == e, jnp.where(row >= want, row + 1, -1), tgt)
        onehot = jnp.where(destp == tgt.astype(f32), 1.0, 0.0).astype(bf16)
        return _dot(onehot, win[...])

    blocks = [(b, step * BLOCKS_PER_STEP + b, slice(b * RT, (b + 1) * RT)) for b in range(BLOCKS_PER_STEP)]
    m, lnp = mod_ref[...], lnp_ref[...]

    def close(rows, y):
        x2_ref[rows, :] = _ln_rows(ALPHA * x1_ref[rows, :] + m[5:6] * y, lnp[2:3], lnp[3:4])

    ys = []
    for b, tb, rows in blocks:
        ys.append(chunk(b, tb, 0, spread_rows(rows)))
    for (b, tb, rows), y in zip(blocks, ys):
        y_ref[b] = y
        close(rows, y)
    for b, tb, rows in blocks:
        @pl.when(nq_ref[tb] > 1)
        def _(b=b, tb=tb, rows=rows):
            destp = spread_rows(rows)

            def body(q, carry):
                y_ref[b] = y_ref[b] + chunk(b, tb, q, destp)
                return carry

            lax.fori_loop(1, nq_ref[tb], body, 0)
            close(rows, y_ref[b])


def _combine(grp, wstart, nq, yg, destab, x1, mod, lnp):
    rt = RT * BLOCKS_PER_STEP
    seg = pl.BlockSpec((None, 8, D), lambda i, *_: (grp.seg0 + (i * rt) // grp.seq if grp.seq > rt else grp.seg0, 0, 0))
    return pl.pallas_call(
        functools.partial(_combine_kernel, grp.cap),
        grid_spec=pltpu.PrefetchScalarGridSpec(
            num_scalar_prefetch=2, grid=(grp.n // rt,),
            in_specs=[_const(yg.shape), _rows(rt, LANE), _rows(rt, D), seg, _const((8, D))],
            out_specs=_rows(rt, D),
            scratch_shapes=[pltpu.VMEM((BLOCKS_PER_STEP, NE * WIN, D), bf16), pltpu.VMEM((BLOCKS_PER_STEP, RT, D), f32)]),
        out_shape=jax.ShapeDtypeStruct((grp.n, D), f32),
        compiler_params=_cp(52), name="moe_combine",
    )(wstart, nq, yg, destab, x1, mod, lnp)


def _moe(layer, closed, mod, lnp, wg, wu, wd):
    disp = []
    for grp, (x1, h2, aff3) in zip((CTX, LAT), closed):
        dest3, gate3, wstart, nq, destab = _route(grp, aff3)
        xe, gcol = _dispatch(grp, wstart, nq, h2, dest3, gate3)
        disp.append((xe, gcol, wstart, nq, destab, x1))
    ys = _ffn(layer, disp[0][0], disp[0][1], disp[1][0], disp[1][1], wg, wu, wd)
    return tuple(_combine(grp, d[2], d[3], y, d[4], d[5], mod, lnp)
                 for grp, d, y in zip((CTX, LAT), disp, ys))


def _mla_weights(wq_a, q_norm, wq_b, wkv_a, kv_norm, wkv_b):
    wqb = wq_b.reshape(MLA_Q_LORA, MLA_HEADS, MLA_NOPE + MLA_ROPE)
    wqb = jnp.concatenate([wqb[:, :, :MLA_NOPE].reshape(MLA_Q_LORA, -1), wqb[:, :, MLA_NOPE:].reshape(MLA_Q_LORA, -1)], axis=1)
    wkva = jnp.concatenate([wkv_a, wkv_a[:, MLA_KV_LORA:]], axis=1)
    wkvb = wkv_b.reshape(MLA_KV_LORA, MLA_HEADS, MLA_NOPE + MLA_V)
    wkvb = jnp.concatenate([wkvb[:, :, :MLA_NOPE].reshape(MLA_KV_LORA, -1), wkvb[:, :, MLA_NOPE:].reshape(MLA_KV_LORA, -1)], axis=1)
    return (wq_a.astype(bf16), q_norm.reshape(1, -1), wqb.astype(bf16), wkva.astype(bf16), kv_norm.reshape(1, -1),
            wkvb.astype(bf16))


def _dup_heads(w):
    w = w.reshape(w.shape[0], SWA_KV, 1, SWA_HD)
    return jnp.broadcast_to(w, (w.shape[0], SWA_KV, 2, SWA_HD)).reshape(w.shape[0], 2 * SWA_KV * SWA_HD)


def kernel(x_prompt, x_sample, c, cache_mla_ckv, cache_mla_kpe, cache_swa_k, cache_swa_v, c_ctx, w_mod, b_mod, ln_gain, ln_bias, router_w, moe_w_gate, moe_w_up, moe_w_down, mla_wq_a, mla_q_norm, mla_wq_b, mla_wkv_a, mla_kv_norm, mla_wkv_b, mla_wo, gm_w_in, gm_b_in, gm_v_norm_g, gm_v_norm_b, gm_w_s, gm_b_s, gm_w_out, gm_b_out, cv_w_pw1, cv_b_pw1, cv_w_dw, cv_b_dw, cv_norm_g, cv_norm_b, cv_w_pw2, cv_b_pw2, swa_wq, swa_wk, swa_wv, swa_sink, swa_wo):
    groups = (CTX, LAT)
    xs = (x_prompt.reshape(NP_TOK, D), x_sample.reshape(NS_TOK, D))
    cond8 = jnp.concatenate([c_ctx[None, :], c, jnp.zeros((5, D), f32)], axis=0)
    mods = _modulation(cond8, w_mod, b_mod)
    lnps = jnp.concatenate([jnp.stack([ln_gain[:, 0], ln_bias[:, 0], ln_gain[:, 1], ln_bias[:, 1]], axis=1),
                            jnp.zeros((DEPTH, 4, D), f32)], axis=1)
    rwts = jnp.pad(router_w, ((0, 0), (0, 0), (0, LANE - NE)))
    tabs = _rope_tables(DEC_SEQ)

    def moe(i, closed):
        return _moe(i, closed, mods[i], lnps[i], moe_w_gate, moe_w_up, moe_w_down)

    wm = _mla_weights(mla_wq_a[0], mla_q_norm[0], mla_wq_b[0], mla_wkv_a[0], mla_kv_norm[0], mla_wkv_b[0])
    qc_p, kc_p, v_p, ckv_p, kpe_p = _mla_pre(CTX, xs[0], mods[0], wm, None)
    ql_s, qc_s, kc_s, v_s = _mla_pre(LAT, xs[1], mods[0], wm, tabs)
    cache_kpe = cache_mla_kpe[:, 0].reshape(DEC_BATCH * PAST, MLA_ROPE)
    kc_c, v_c = _mla_ctx(cache_mla_ckv[:, 0].reshape(DEC_BATCH * PAST, MLA_KV_LORA),
                         jnp.concatenate([cache_kpe, cache_kpe], axis=1), wm[5])
    os_ = (_mla_attn_p(qc_p, kc_p, v_p), _mla_attn_s(ql_s, qc_s, kc_s, v_s, kc_c, v_c))
    wo = mla_wo[0].astype(bf16)
    xs = moe(0, [_proj_close(g, o, x, mods[0], lnps[0], wo, rwts[0]) for g, o, x in zip(groups, os_, xs)])

    wgm = (gm_w_in[0].astype(bf16), gm_b_in[0].reshape(1, -1), gm_v_norm_g[0].reshape(1, -1),
           gm_v_norm_b[0].reshape(1, -1), gm_w_s[0].astype(bf16), gm_b_s[0].T, gm_w_out[0].astype(bf16),
           gm_b_out[0].reshape(1, -1))
    xs = moe(1, [_gmlp_layer(g, x, mods[1], lnps[1], wgm, rwts[1]) for g, x in zip(groups, xs)])

    w1, b1 = cv_w_pw1[0].astype(bf16), cv_b_pw1[0].reshape(1, -1)
    wcv = (cv_w_dw[0], cv_b_dw[0].reshape(1, -1), cv_norm_g[0].reshape(1, -1), cv_norm_b[0].reshape(1, -1),
           cv_w_pw2[0].astype(bf16), cv_b_pw2[0].reshape(1, -1))
    xs = moe(2, [_conv_close(g, _conv_glu(g, x, mods[2], w1, b1), x, mods[2], lnps[2], wcv, rwts[2])
                 for g, x in zip(groups, xs)])

    wsw = (swa_wq[0].astype(bf16), _dup_heads(swa_wk[0]).astype(bf16), _dup_heads(swa_wv[0]).astype(bf16))
    q_p, kd_p, vd_p, k_p, v_p = _swa_pre(CTX, xs[0], mods[3], wsw, None)
    qr_s, qw_s, kd_s, vd_s = _swa_pre(LAT, xs[1], mods[3], wsw, tabs)

    def dup_cache(t):
        return _dup_heads(t[:, 0].reshape(DEC_BATCH * PAST, SWA_KV * SWA_HD)).astype(bf16)

    sink = swa_sink[0]
    os_ = (_swa_attn_p(sink, q_p, kd_p, vd_p),
           _swa_attn_s(sink, qr_s, qw_s, kd_s, vd_s, dup_cache(cache_swa_k), dup_cache(cache_swa_v)))
    wo = swa_wo[0].astype(bf16)
    xs = moe(3, [_proj_close(g, o, x, mods[3], lnps[3], wo, rwts[3]) for g, o, x in zip(groups, os_, xs)])

    return (xs[0].reshape(BATCH, SEQ, D), xs[1].reshape(DEC_BATCH, DEC_SEQ, D),
            ckv_p.reshape(BATCH, 1, SEQ, MLA_KV_LORA), kpe_p.reshape(BATCH, 1, SEQ, MLA_ROPE),
            k_p.reshape(BATCH, 1, SEQ, SWA_KV, SWA_HD), v_p.reshape(BATCH, 1, SEQ, SWA_KV, SWA_HD))
```
